```python
import jax, jax.numpy as jnp
from jax import lax
import numpy as np

D_MODEL = 2048
BATCH = 2
SEQ = 8192
DEPTH = 1

GRID_W = 64
CTX_LEN = 256

MLA_HEADS = 8
MLA_NOPE = 128
MLA_ROPE = 64
MLA_QK_DIM = MLA_NOPE + MLA_ROPE
MLA_V = 128
MLA_Q_RANK = 512
MLA_KV_RANK = 256
NA_HEADS = 8
NA_HEAD_DIM = 128
NA_KH = 8
NA_KW = 16

MLA_OUT = MLA_HEADS * MLA_V
NA_WIDTH = NA_HEADS * NA_HEAD_DIM
MIX_WIDTH = MLA_OUT + NA_WIDTH
IN_MLA = MLA_Q_RANK + MLA_KV_RANK + MLA_ROPE
IN_WIDTH = IN_MLA + 3 * NA_WIDTH

N_EXPERTS = 16
EC_CAPACITY_FACTOR = 2
D_EXPERT = 1024

ROPE_THETA = 10000.0
EPS = 1e-6
Q_BLOCK = 128
N_MOD = 6
MLA_SCALE = MLA_QK_DIM ** -0.5
NA_SCALE = NA_HEAD_DIM ** -0.5

kernel_name = "hybrid_mla_natten_ec_moe_dit"


def rms_norm(x, g):
    xf = x.astype(jnp.float32)
    y = xf * lax.rsqrt(jnp.mean(xf * xf, axis=-1, keepdims=True) + EPS)
    return (y * g.astype(jnp.float32)).astype(x.dtype)


def modulate(x, shift, scale):
    return x * (1 + scale) + shift


def axial_rope_tables(n_tokens):
    t = jnp.arange(n_tokens)
    row = (t // GRID_W).astype(jnp.float32)
    col = (t % GRID_W).astype(jnp.float32)
    n_freq = MLA_ROPE // 4
    inv = ROPE_THETA ** (-jnp.arange(n_freq, dtype=jnp.float32) / n_freq)
    ang = jnp.concatenate([row[:, None] * inv, col[:, None] * inv], axis=-1)
    return jnp.cos(ang), jnp.sin(ang)


def apply_rope(x, cos, sin):
    half = MLA_ROPE // 2
    c = cos[:, None, :].astype(x.dtype)
    s = sin[:, None, :].astype(x.dtype)
    x1, x2 = x[..., :half], x[..., half:]
    return jnp.concatenate([x1 * c - x2 * s, x1 * s + x2 * c], axis=-1)


def rope_tail(t, cos, sin):
    return jnp.concatenate([t[..., :MLA_NOPE], apply_rope(t[..., MLA_NOPE:], cos, sin)], axis=-1)


def mla_q(p, g_qa, w_qb, g_q, cos, sin):
    B, T, _ = p.shape
    cq = p[..., :MLA_Q_RANK]
    q = (rms_norm(cq, g_qa) @ w_qb).reshape(B, T, MLA_HEADS, MLA_QK_DIM)
    q = rms_norm(q, g_q)
    if cos is not None:
        q = rope_tail(q, cos, sin)
    return q


def mla_kv(p, g_kva, w_kvb, g_k, cos, sin):
    B, T, _ = p.shape
    ckv = p[..., MLA_Q_RANK:MLA_Q_RANK + MLA_KV_RANK]
    k_pe = p[..., MLA_Q_RANK + MLA_KV_RANK:IN_MLA]
    kv = (rms_norm(ckv, g_kva) @ w_kvb).reshape(B, T, MLA_HEADS, MLA_NOPE + MLA_V)
    k_nope, v = kv[..., :MLA_NOPE], kv[..., MLA_NOPE:]
    k_pe = jnp.broadcast_to(k_pe[:, :, None, :], (B, T, MLA_HEADS, MLA_ROPE))
    k = rms_norm(jnp.concatenate([k_nope, k_pe], axis=-1), g_k)
    if cos is not None:
        k = rope_tail(k, cos, sin)
    return k, v


def na_q(p, g_q):
    B, T, _ = p.shape
    q = p[..., IN_MLA:IN_MLA + NA_WIDTH].reshape(B, T, NA_HEADS, NA_HEAD_DIM)
    return rms_norm(q, g_q)


def na_kv(p, g_k):
    B, T, _ = p.shape
    k = p[..., IN_MLA + NA_WIDTH:IN_MLA + 2 * NA_WIDTH].reshape(B, T, NA_HEADS, NA_HEAD_DIM)
    v = p[..., IN_MLA + 2 * NA_WIDTH:IN_WIDTH].reshape(B, T, NA_HEADS, NA_HEAD_DIM)
    return rms_norm(k, g_k), v


def context_attn(q, k, v, scale):
    s = jnp.einsum('bqhd,bkhd->bhqk', q, k, preferred_element_type=jnp.float32) * scale
    p = jax.nn.softmax(s, axis=-1).astype(v.dtype)
    return jnp.einsum('bhqk,bkhd->bqhd', p, v)


def dense_attn_with_ctx(q, k, v, k_ctx, v_ctx, scale):
    B, T, H, Dk = q.shape
    Dv = v.shape[-1]
    nb = T // Q_BLOCK
    k_all = jnp.concatenate([k, k_ctx], axis=1)
    v_all = jnp.concatenate([v, v_ctx], axis=1)
    qb = q.reshape(B, nb, Q_BLOCK, H, Dk).transpose(1, 0, 2, 3, 4)

    def block(qi):
        s = jnp.einsum('bqhd,bkhd->bhqk', qi, k_all, preferred_element_type=jnp.float32) * scale
        p = jax.nn.softmax(s, axis=-1).astype(v_all.dtype)
        return jnp.einsum('bhqk,bkhd->bqhd', p, v_all)

    o = lax.map(block, qb)
    return o.transpose(1, 0, 2, 3, 4).reshape(B, T, H, Dv)


def neighbourhood_attn_with_ctx(q, k, v, k_ctx, v_ctx, rpb, rows):
    B, T, H, D = q.shape
    kh = min(NA_KH, rows)
    kw = NA_KW
    n_loc = kh * kw
    qg = q.reshape(B, rows, GRID_W, H, D)
    kg = k.reshape(B, rows, GRID_W, H, D)
    vg = v.reshape(B, rows, GRID_W, H, D)
    col = jnp.arange(GRID_W)
    col_start = jnp.clip(col - kw // 2, 0, GRID_W - kw)
    col_idx = col_start[:, None] + jnp.arange(kw)[None, :]
    col_off = col_idx - col[:, None] + (NA_KW - 1)

    def row_block(r):
        rs = jnp.clip(r - kh // 2, 0, rows - kh)
        qr = lax.dynamic_index_in_dim(qg, r, axis=1, keepdims=False)
        kband = lax.dynamic_slice_in_dim(kg, rs, kh, axis=1)
        vband = lax.dynamic_slice_in_dim(vg, rs, kh, axis=1)
        kwin = kband[:, :, col_idx]
        vwin = vband[:, :, col_idx]
        row_off = rs + jnp.arange(kh) - r + (NA_KH - 1)
        bias = rpb[:, row_off[None, :, None], col_off[:, None, :]]
        s_loc = jnp.einsum('bqhd,brqwhd->bhqrw', qr, kwin,
                           preferred_element_type=jnp.float32) * NA_SCALE
        s_loc = s_loc + bias[None].astype(jnp.float32)
        s_ctx = jnp.einsum('bqhd,bkhd->bhqk', qr, k_ctx,
                           preferred_element_type=jnp.float32) * NA_SCALE
        s = jnp.concatenate([s_loc.reshape(B, H, GRID_W, n_loc), s_ctx], axis=-1)
        p = jax.nn.softmax(s, axis=-1).astype(v.dtype)
        p_loc = p[..., :n_loc].reshape(B, H, GRID_W, kh, kw)
        p_ctx = p[..., n_loc:]
        return (jnp.einsum('bhqrw,brqwhd->bqhd', p_loc, vwin)
                + jnp.einsum('bhqk,bkhd->bqhd', p_ctx, v_ctx))

    o = lax.map(row_block, jnp.arange(rows))
    return o.transpose(1, 0, 2, 3, 4).reshape(B, T, H, D)


def merge_heads(o_mla, o_na, g_out_mla, g_out_na, w_out):
    B, T = o_mla.shape[:2]
    a = rms_norm(o_mla.reshape(B, T, MLA_OUT), g_out_mla)
    b = rms_norm(o_na.reshape(B, T, NA_WIDTH), g_out_na)
    return jnp.concatenate([a, b], axis=-1) @ w_out


def expert_choice_ffn(h, w_router, w_gate, w_up, w_down):
    B, T, D = h.shape
    cap = EC_CAPACITY_FACTOR * T // N_EXPERTS
    aff = jax.nn.softmax(jnp.einsum('btd,de->bte', h, w_router,
                                    preferred_element_type=jnp.float32), axis=-1)
    g, idx = lax.top_k(aff.transpose(0, 2, 1), cap)
    xs = jax.vmap(lambda hb, ib: hb[ib])(h, idx)
    a = jnp.einsum('becd,edf->becf', xs, w_gate)
    u = jnp.einsum('becd,edf->becf', xs, w_up)
    y = jnp.einsum('becf,efd->becd', jax.nn.silu(a) * u, w_down) * g[..., None].astype(h.dtype)
    return jax.vmap(lambda yb, ib: jnp.zeros((T, D), yb.dtype).at[ib.reshape(-1)].add(yb.reshape(-1, D)))(y, idx)


def setup_inputs(seed: int = 0) -> dict:
    key = jax.random.key(seed)
    ks = jax.random.split(key, 32)
    f32 = jnp.float32
    D = D_MODEL

    def nrm(k, shape, scale):
        return jax.random.normal(k, shape, f32) * scale

    def gain(k, shape):
        return 1.0 + 0.02 * jax.random.normal(k, shape, f32)

    return {
        "x": nrm(ks[0], (BATCH, SEQ, D), 1.0),
        "c": nrm(ks[1], (BATCH, D), 1.0),
        "ctx": nrm(ks[2], (BATCH, CTX_LEN, D), 1.0),
        "c_ctx": nrm(ks[3], (D,), 1.0),
        "w_mod": nrm(ks[4], (DEPTH, D, N_MOD * D), 0.5 * D ** -0.5),
        "b_mod": nrm(ks[5], (DEPTH, N_MOD * D), 0.01),
        "g_norm1": gain(ks[6], (DEPTH, D)),
        "w_in": nrm(ks[7], (DEPTH, D, IN_WIDTH), D ** -0.5),
        "g_qa": gain(ks[8], (DEPTH, MLA_Q_RANK)),
        "w_qb": nrm(ks[9], (DEPTH, MLA_Q_RANK, MLA_HEADS * MLA_QK_DIM), MLA_Q_RANK ** -0.5),
        "g_kva": gain(ks[10], (DEPTH, MLA_KV_RANK)),
        "w_kvb": nrm(ks[11], (DEPTH, MLA_KV_RANK, MLA_HEADS * (MLA_NOPE + MLA_V)), MLA_KV_RANK ** -0.5),
        "g_q_mla": gain(ks[12], (DEPTH, MLA_QK_DIM)),
        "g_k_mla": gain(ks[13], (DEPTH, MLA_QK_DIM)),
        "g_q_na": gain(ks[14], (DEPTH, NA_HEAD_DIM)),
        "g_k_na": gain(ks[15], (DEPTH, NA_HEAD_DIM)),
        "rpb_na": nrm(ks[16], (DEPTH, NA_HEADS, 2 * NA_KH - 1, 2 * NA_KW - 1), 0.1),
        "g_out_mla": gain(ks[17], (DEPTH, MLA_OUT)),
        "g_out_na": gain(ks[18], (DEPTH, NA_WIDTH)),
        "w_out": nrm(ks[19], (DEPTH, MIX_WIDTH, D), MIX_WIDTH ** -0.5),
        "g_norm2": gain(ks[20], (DEPTH, D)),
        "w_router": nrm(ks[21], (DEPTH, D, N_EXPERTS), D ** -0.5),
        "w_gate": nrm(ks[22], (DEPTH, N_EXPERTS, D, D_EXPERT), D ** -0.5),
        "w_up": nrm(ks[23], (DEPTH, N_EXPERTS, D, D_EXPERT), D ** -0.5),
        "w_down": nrm(ks[24], (DEPTH, N_EXPERTS, D_EXPERT, D), D_EXPERT ** -0.5),
    }


def reference(x, c, ctx, c_ctx, w_mod, b_mod, g_norm1, w_in, g_qa, w_qb, g_kva, w_kvb,
              g_q_mla, g_k_mla, g_q_na, g_k_na, rpb_na, g_out_mla, g_out_na, w_out,
              g_norm2, w_router, w_gate, w_up, w_down):
    B, T, D = x.shape
    rows = T // GRID_W
    cos, sin = axial_rope_tables(T)
    for l in range(DEPTH):
        last = l == DEPTH - 1
        mod_lat = (jax.nn.silu(c) @ w_mod[l] + b_mod[l]).reshape(B, N_MOD, 1, D)
        mod_ctx = (jax.nn.silu(c_ctx)[None] @ w_mod[l] + b_mod[l]).reshape(1, N_MOD, 1, D)

        h = modulate(rms_norm(x, g_norm1[l]), mod_lat[:, 0], mod_lat[:, 1])
        hc = modulate(rms_norm(ctx, g_norm1[l]), mod_ctx[:, 0], mod_ctx[:, 1])
        p_lat = h @ w_in[l]
        p_ctx = hc @ w_in[l]

        q_m = mla_q(p_lat, g_qa[l], w_qb[l], g_q_mla[l], cos, sin)
        k_m, v_m = mla_kv(p_lat, g_kva[l], w_kvb[l], g_k_mla[l], cos, sin)
        kc_m, vc_m = mla_kv(p_ctx, g_kva[l], w_kvb[l], g_k_mla[l], None, None)
        o_m = dense_attn_with_ctx(q_m, k_m, v_m, kc_m, vc_m, MLA_SCALE)

        q_n = na_q(p_lat, g_q_na[l])
        k_n, v_n = na_kv(p_lat, g_k_na[l])
        kc_n, vc_n = na_kv(p_ctx, g_k_na[l])
        o_n = neighbourhood_attn_with_ctx(q_n, k_n, v_n, kc_n, vc_n, rpb_na[l], rows)

        x = x + mod_lat[:, 2] * merge_heads(o_m, o_n, g_out_mla[l], g_out_na[l], w_out[l])

        if not last:
            oc_m = context_attn(mla_q(p_ctx, g_qa[l], w_qb[l], g_q_mla[l], None, None), kc_m, vc_m, MLA_SCALE)
            oc_n = context_attn(na_q(p_ctx, g_q_na[l]), kc_n, vc_n, NA_SCALE)
            ctx = ctx + mod_ctx[:, 2] * merge_heads(oc_m, oc_n, g_out_mla[l], g_out_na[l], w_out[l])

        h2 = modulate(rms_norm(x, g_norm2[l]), mod_lat[:, 3], mod_lat[:, 4])
        x = x + mod_lat[:, 5] * expert_choice_ffn(h2, w_router[l], w_gate[l], w_up[l], w_down[l])

        if not last:
            hc2 = modulate(rms_norm(ctx, g_norm2[l]), mod_ctx[:, 3], mod_ctx[:, 4])
            ctx = ctx + mod_ctx[:, 5] * expert_choice_ffn(hc2, w_router[l], w_gate[l], w_up[l], w_down[l])
    return x
```

```python
import functools

import numpy as np
import jax
import jax.numpy as jnp
from jax import lax
from jax.experimental import pallas as pl
from jax.experimental.pallas import tpu as pltpu

F32 = jnp.float32
BF16 = jnp.bfloat16

GRID_W = 64
MLA_HEADS = 8
MLA_NOPE = 128
MLA_ROPE = 64
MLA_QK_DIM = MLA_NOPE + MLA_ROPE
MLA_V = 128
MLA_Q_RANK = 512
MLA_KV_RANK = 256
NA_HEADS = 8
NA_HEAD_DIM = 128
NA_KH = 8
NA_KW = 16
N_EXPERTS = 16
EC_CAPACITY_FACTOR = 2
ROPE_THETA = 10000.0
EPS = 1e-6
N_MOD = 6
MLA_SCALE = MLA_QK_DIM ** -0.5
NA_SCALE = NA_HEAD_DIM ** -0.5
LOG2E = 1.4426950408889634

LANES = 128
MLA_QK_PAD = 2 * LANES
NA_W = NA_HEADS * NA_HEAD_DIM
P_WIDTH = 3 * NA_W + 1024
VMEM_LIMIT = 56 * 1024 * 1024
NEG = -1e30


def _f32(v):
    return jnp.asarray(v, dtype=F32)


def _mult(v, m):
    return v if isinstance(v, int) else pl.multiple_of(v, m)


def _cparams(sem, vmem=VMEM_LIMIT):
    return pltpu.CompilerParams(dimension_semantics=sem, vmem_limit_bytes=vmem)


def _mod_kernel(ct_ref, w_ref, b_ref, o_ref):
    ct = ct_ref[...]
    s = ct * (1.0 / (1.0 + jnp.exp(-ct)))
    w = w_ref[...]
    rows = []
    for m in range(o_ref.shape[0]):
        rows.append(jnp.sum(w * s[:, m:m + 1], axis=0, keepdims=True))
    o_ref[...] = jnp.concatenate(rows, axis=0) + b_ref[...]


def _mod_call(ct, w_mod, b_mod, n_rows):
    D, N = w_mod.shape
    tn = 512
    return pl.pallas_call(
        _mod_kernel,
        out_shape=jax.ShapeDtypeStruct((n_rows, N), F32),
        grid=(N // tn,),
        in_specs=[pl.BlockSpec((D, 8), lambda j: (0, 0)),
                  pl.BlockSpec((D, tn), lambda j: (0, j)),
                  pl.BlockSpec((1, tn), lambda j: (0, j))],
        out_specs=pl.BlockSpec((n_rows, tn), lambda j: (0, j)),
        compiler_params=_cparams(("arbitrary",)),
        name="mod",
    )(ct, w_mod, b_mod)


def _inproj_kernel(x_ref, mod_ref, g_ref, w_ref, o_ref, hn_ref):
    @pl.when(pl.program_id(1) == 0)
    def _():
        x = x_ref[...]
        y = x * lax.rsqrt(jnp.mean(x * x, axis=-1, keepdims=True) + EPS) * g_ref[...]
        h = y * (1.0 + mod_ref[0, 1:2, :]) + mod_ref[0, 0:1, :]
        hn_ref[...] = h.astype(BF16)

    o_ref[...] = jnp.dot(hn_ref[...], w_ref[...], preferred_element_type=F32).astype(BF16)


def _inproj_call(x2, mod, g, w, rows_per_mod, tm):
    M, D = x2.shape
    N = w.shape[1]
    tn = 1024
    per = rows_per_mod // tm
    return pl.pallas_call(
        _inproj_kernel,
        out_shape=jax.ShapeDtypeStruct((M, N), BF16),
        grid=(M // tm, N // tn),
        in_specs=[pl.BlockSpec((tm, D), lambda i, j: (i, 0)),
                  pl.BlockSpec((1, N_MOD, D), lambda i, j: (i // per, 0, 0)),
                  pl.BlockSpec((1, D), lambda i, j: (0, 0)),
                  pl.BlockSpec((D, tn), lambda i, j: (0, j))],
        out_specs=pl.BlockSpec((tm, tn), lambda i, j: (i, j)),
        scratch_shapes=[pltpu.VMEM((tm, D), BF16)],
        compiler_params=_cparams(("arbitrary", "arbitrary")),
        name="inproj",
    )(x2, mod, g, w)


def _rms(x, g):
    return x * lax.rsqrt(jnp.mean(x * x, axis=-1, keepdims=True) + EPS) * g


def _prep_kernel(*refs, is_ctx):
    if is_ctx:
        (pk_ref, pm_ref, gkva_ref, wk_ref, wv_ref, gk0_ref, gka_ref, gkn_ref,
         km_ref, vm_ref, kn_ref) = refs
    else:
        (pq_ref, pk_ref, pm_ref, ct_ref, st_ref, gqa_ref, wq_ref, gkva_ref, wk_ref, wv_ref,
         gq0_ref, gqa2_ref, gqb2_ref, gk0_ref, gka_ref, gkb_ref, gqn_ref, gkn_ref,
         qm_ref, km_ref, vm_ref, qn_ref, kn_ref) = refs
    tm = pm_ref.shape[0]
    lane = lax.broadcasted_iota(jnp.int32, (1, LANES), 1)
    rope_mask = (lane < MLA_ROPE).astype(F32)
    pm = pm_ref[...].astype(F32)
    inv_qk = 1.0 / MLA_QK_DIM

    ckvn = _rms(pm[:, MLA_Q_RANK:MLA_Q_RANK + MLA_KV_RANK], gkva_ref[...]).astype(BF16)
    kn = jnp.dot(ckvn, wk_ref[...], preferred_element_type=F32)
    vv = jnp.dot(ckvn, wv_ref[...], preferred_element_type=F32)
    kpe = pm[:, MLA_Q_RANK + MLA_KV_RANK:MLA_Q_RANK + MLA_KV_RANK + LANES]
    ss_pe = jnp.sum(kpe * kpe * rope_mask, axis=-1, keepdims=True)
    if is_ctx:
        k_rope = kpe * gka_ref[...]
    else:
        ct = ct_ref[...]
        st = st_ref[...]
        k_rope = kpe * (ct * gka_ref[...]) + pltpu.roll(kpe, MLA_ROPE, 1) * (st * gkb_ref[...])
    for h in range(MLA_HEADS):
        k0 = kn[:, h * LANES:(h + 1) * LANES]
        r = lax.rsqrt((jnp.sum(k0 * k0, axis=-1, keepdims=True) + ss_pe) * inv_qk + EPS)
        km_ref[0, h] = jnp.concatenate([k0 * r * gk0_ref[...], k_rope * r], axis=-1).astype(BF16)
        vm_ref[0, h] = jnp.concatenate([vv[:, h * LANES:(h + 1) * LANES], jnp.ones((tm, LANES), F32)],
                                       axis=-1).astype(BF16)

    pk = pk_ref[...].astype(F32)
    for h in range(NA_HEADS):
        kn_ref[0, h] = _rms(pk[:, h * LANES:(h + 1) * LANES], gkn_ref[...]).astype(BF16)

    if is_ctx:
        return

    cqn = _rms(pm[:, :MLA_Q_RANK], gqa_ref[...]).astype(BF16)
    q = jnp.dot(cqn, wq_ref[...], preferred_element_type=F32)
    qa = ct * gqa2_ref[...]
    qb = st * gqb2_ref[...]
    for h in range(MLA_HEADS):
        t0 = q[:, h * MLA_QK_PAD:h * MLA_QK_PAD + LANES]
        t1 = q[:, h * MLA_QK_PAD + LANES:(h + 1) * MLA_QK_PAD]
        ss = jnp.sum(t0 * t0, axis=-1, keepdims=True) + jnp.sum(t1 * t1 * rope_mask, axis=-1, keepdims=True)
        r = lax.rsqrt(ss * inv_qk + EPS) * (MLA_SCALE * LOG2E)
        o1 = t1 * qa + pltpu.roll(t1, MLA_ROPE, 1) * qb
        qm_ref[0, h] = jnp.concatenate([t0 * r * gq0_ref[...], o1 * r], axis=-1).astype(BF16)

    pq = pq_ref[...].astype(F32)
    for h in range(NA_HEADS):
        qn_ref[0, h] = (_rms(pq[:, h * LANES:(h + 1) * LANES], gqn_ref[...]) * NA_SCALE).astype(BF16)


def _prep_call(p, nb, tok, tm, consts, is_ctx):
    per = tok // tm
    H = MLA_HEADS
    row = lambda i: (i, 0)
    full = lambda i: (0, 0)
    hm = lambda i: (i // per, 0, i % per, 0)
    p_spec = lambda c: pl.BlockSpec((tm, 1024), lambda i, c=c: (i, c))
    vec = lambda n: pl.BlockSpec((1, n), full)
    km_s = jax.ShapeDtypeStruct((nb, H, tok, MLA_QK_PAD), BF16)
    h128_s = jax.ShapeDtypeStruct((nb, H, tok, LANES), BF16)
    km_o = pl.BlockSpec((1, H, tm, MLA_QK_PAD), hm)
    h128_o = pl.BlockSpec((1, H, tm, LANES), hm)
    c = consts
    if is_ctx:
        ins = [p, p, c["g_kva"], c["wk"], c["wv"], c["gk0"], c["gka"], c["g_k_na"]]
        in_specs = [p_spec(1), p_spec(3), vec(MLA_KV_RANK),
                    pl.BlockSpec(c["wk"].shape, full), pl.BlockSpec(c["wv"].shape, full),
                    vec(LANES), vec(LANES), vec(LANES)]
        out_shape = [km_s, km_s, h128_s]
        out_specs = [km_o, km_o, h128_o]
    else:
        ins = [p, p, p, c["ct"], c["st"], c["g_qa"], c["wq"], c["g_kva"], c["wk"], c["wv"],
               c["gq0"], c["gqa"], c["gqb"], c["gk0"], c["gka"], c["gkb"], c["g_q_na"], c["g_k_na"]]
        tab = pl.BlockSpec((tm, LANES), lambda i: (i % per, 0))
        in_specs = [p_spec(0), p_spec(1), p_spec(3), tab, tab, vec(MLA_Q_RANK),
                    pl.BlockSpec(c["wq"].shape, full), vec(MLA_KV_RANK),
                    pl.BlockSpec(c["wk"].shape, full), pl.BlockSpec(c["wv"].shape, full),
                    vec(LANES), vec(LANES), vec(LANES), vec(LANES), vec(LANES), vec(LANES),
                    vec(LANES), vec(LANES)]
        out_shape = [km_s, km_s, km_s, h128_s, h128_s]
        out_specs = [km_o, km_o, km_o, h128_o, h128_o]
    return pl.pallas_call(
        functools.partial(_prep_kernel, is_ctx=is_ctx),
        out_shape=out_shape,
        grid=(nb * per,),
        in_specs=in_specs,
        out_specs=out_specs,
        compiler_params=_cparams(("arbitrary",)),
        name="prep_ctx" if is_ctx else "prep_lat",
    )(*ins)


MLA_SUB = 2


def _mla_kernel(q_ref, k_ref, v_ref, kc_ref, vc_ref, o_ref, m_ref, acc_ref, sa_ref, sb_ref, sc_ref, *, tk):
    ts = q_ref.shape[2] // MLA_SUB
    n = k_ref.shape[2] // tk
    m_ref[...] = jnp.full(m_ref.shape, NEG, F32)
    acc_ref[...] = jnp.zeros(acc_ref.shape, F32)
    dn = (((1,), (1,)), ((), ()))

    def scores(c):
        return lax.dot_general(q_ref[0, 0], k_ref[0, 0, pl.ds(_mult(c * tk, tk), tk), :], dn,
                               preferred_element_type=F32)

    def values(c):
        return v_ref[0, 0, pl.ds(_mult(c * tk, tk), tk), :]

    def softmax_pv(s_ref, v):
        for u in range(MLA_SUB):
            rows = slice(u * ts, (u + 1) * ts)
            m_old = m_ref[rows, :]
            m_new = jnp.maximum(m_old, jnp.max(s_ref[rows, :], axis=-1, keepdims=True))
            p = jnp.exp2(s_ref[rows, :] - m_new).astype(BF16)
            acc_ref[rows, :] = (jnp.exp2(m_old - m_new) * acc_ref[rows, :]
                                + jnp.dot(p, v, preferred_element_type=F32))
            m_ref[rows, :] = m_new

    sa_ref[...] = scores(0)

    def body(i, carry):
        c = 2 * i
        sb_ref[...] = scores(c + 1)
        softmax_pv(sa_ref, values(c))
        sa_ref[...] = scores(c + 2)
        softmax_pv(sb_ref, values(c + 1))
        return carry

    lax.fori_loop(0, n // 2 - 1, body, 0)
    sb_ref[...] = scores(n - 1)
    softmax_pv(sa_ref, values(n - 2))
    sc_ref[...] = lax.dot_general(q_ref[0, 0], kc_ref[0, 0], dn, preferred_element_type=F32)
    softmax_pv(sb_ref, values(n - 1))
    softmax_pv(sc_ref, vc_ref[0, 0])
    o_ref[0] = (acc_ref[:, :MLA_V] / acc_ref[:, MLA_V:]).astype(o_ref.dtype)


def _mla_call(qm, km, vm, kmc, vmc, tq, tk):
    B, H, T, _ = qm.shape
    C = kmc.shape[2]
    VW = vm.shape[3]
    return pl.pallas_call(
        functools.partial(_mla_kernel, tk=tk),
        out_shape=jax.ShapeDtypeStruct((B, T, H * MLA_V), BF16),
        grid=(B, H, T // tq),
        in_specs=[pl.BlockSpec((1, 1, tq, MLA_QK_PAD), lambda b, h, i: (b, h, i, 0)),
                  pl.BlockSpec((1, 1, T, MLA_QK_PAD), lambda b, h, i: (b, h, 0, 0)),
                  pl.BlockSpec((1, 1, T, VW), lambda b, h, i: (b, h, 0, 0)),
                  pl.BlockSpec((1, 1, C, MLA_QK_PAD), lambda b, h, i: (b, h, 0, 0)),
                  pl.BlockSpec((1, 1, C, VW), lambda b, h, i: (b, h, 0, 0))],
        out_specs=pl.BlockSpec((1, tq, MLA_V), lambda b, h, i: (b, i, h)),
        scratch_shapes=[pltpu.VMEM((tq, 1), F32), pltpu.VMEM((tq, VW), F32),
                        pltpu.VMEM((tq, tk), F32), pltpu.VMEM((tq, tk), F32), pltpu.VMEM((tq, C), F32)],
        compiler_params=_cparams(("arbitrary", "arbitrary", "arbitrary")),
        name="mla_attn",
    )(qm, km, vm, kmc, vmc)


NA_QROWS = 2
NA_TILE = NA_QROWS * GRID_W
NA_BAND = 5
NA_CFGS = 5


def _na_band_start(i, nblk):
    return jnp.clip(i - 2, 0, nblk - NA_BAND)


def _na_kernel(q_ref, k0, k1, k2, k3, k4, v0, v1, v2, v3, v4, kc_ref, vc_ref, bias_ref, o_ref):
    k_refs = (k0, k1, k2, k3, k4)
    v_refs = (v0, v1, v2, v3, v4)
    vc = vc_ref[...]
    vcat = jnp.concatenate([r[...] for r in v_refs], axis=0)
    for h in range(NA_HEADS):
        q = q_ref[0, h]
        kcat = jnp.concatenate([r[0, h] for r in k_refs], axis=0)
        dn = (((1,), (1,)), ((), ()))
        s_loc = lax.dot_general(q, kcat, dn, preferred_element_type=F32) + bias_ref[0, h]
        s_ctx = lax.dot_general(q, kc_ref[0, h], dn, preferred_element_type=F32)
        m = jnp.maximum(jnp.max(s_loc, axis=-1, keepdims=True), jnp.max(s_ctx, axis=-1, keepdims=True))
        p_loc = jnp.exp(s_loc - m)
        p_ctx = jnp.exp(s_ctx - m)
        l = jnp.sum(p_loc, axis=-1, keepdims=True) + jnp.sum(p_ctx, axis=-1, keepdims=True)
        sl = slice(h * LANES, (h + 1) * LANES)
        o = (jnp.dot(p_loc.astype(BF16), vcat[:, sl], preferred_element_type=F32)
             + jnp.dot(p_ctx.astype(BF16), vc[:, sl], preferred_element_type=F32))
        o_ref[0, :, sl] = (o / l).astype(o_ref.dtype)


def _na_call(qn, kn, p_lat, knc, p_ctx, bias):
    B, H, T, _ = qn.shape
    C = knc.shape[2]
    nblk = T // NA_TILE
    tiles_per_b = T // NA_TILE

    def cfg(i):
        return jnp.minimum(i, 2) + jnp.maximum(i - (nblk - 3), 0)

    k_specs = [pl.BlockSpec((1, H, NA_TILE, LANES),
                            lambda b, i, u=u: (b, 0, _na_band_start(i, nblk) + u, 0)) for u in range(NA_BAND)]
    v_specs = [pl.BlockSpec((NA_TILE, NA_W),
                            lambda b, i, u=u: (b * tiles_per_b + _na_band_start(i, nblk) + u, 2))
               for u in range(NA_BAND)]
    return pl.pallas_call(
        _na_kernel,
        out_shape=jax.ShapeDtypeStruct((B, T, NA_W), BF16),
        grid=(B, nblk),
        in_specs=[pl.BlockSpec((1, H, NA_TILE, LANES), lambda b, i: (b, 0, i, 0))] + k_specs + v_specs + [
            pl.BlockSpec((1, H, C, LANES), lambda b, i: (b, 0, 0, 0)),
            pl.BlockSpec((C, NA_W), lambda b, i: (b, 2)),
            pl.BlockSpec((1, H, NA_TILE, NA_BAND * NA_TILE), lambda b, i: (cfg(i), 0, 0, 0))],
        out_specs=pl.BlockSpec((1, NA_TILE, NA_W), lambda b, i: (b, i, 0)),
        compiler_params=_cparams(("arbitrary", "arbitrary")),
        name="na_attn",
    )(qn, *([kn] * NA_BAND), *([p_lat] * NA_BAND), knc, p_ctx, bias)


def _na_bias_table(rpb, rows):
    nblk = rows // NA_QROWS
    reps = [0, 1, 2, nblk - 2, nblk - 1]
    q_r = np.arange(NA_TILE) // GRID_W
    q_c = np.arange(NA_TILE) % GRID_W
    k_r = np.arange(NA_BAND * NA_TILE) // GRID_W
    k_c = np.arange(NA_BAND * NA_TILE) % GRID_W
    padw = GRID_W - NA_KW
    rp = jnp.pad(rpb, ((0, 0), (0, 0), (padw, padw)))
    toep = jnp.stack([rp[:, :, GRID_W - 1 - qc:2 * GRID_W - 1 - qc] for qc in range(GRID_W)], axis=2)
    tables, ok = [], []
    for i in reps:
        j0 = int(np.clip(i - 2, 0, nblk - NA_BAND))
        r = NA_QROWS * i + q_r
        rs = np.clip(r - NA_KH // 2, 0, rows - NA_KH)
        cs = np.clip(q_c - NA_KW // 2, 0, GRID_W - NA_KW)
        kr = NA_QROWS * j0 + k_r
        ok.append((kr[None, :] >= rs[:, None]) & (kr[None, :] < rs[:, None] + NA_KH)
                  & (k_c[None, :] >= cs[:, None]) & (k_c[None, :] < cs[:, None] + NA_KW))
        per_qr = []
        for qr in range(NA_QROWS):
            pieces = []
            for kb in range(NA_BAND * NA_QROWS):
                ro = int(np.clip(NA_QROWS * (j0 - i) + kb - qr + NA_KH - 1, 0, 2 * NA_KH - 2))
                pieces.append(toep[:, ro])
            per_qr.append(jnp.concatenate(pieces, axis=-1))
        tables.append(jnp.concatenate(per_qr, axis=1))
    return jnp.where(np.stack(ok)[:, None], jnp.stack(tables), NEG).astype(F32)


def _merge_kernel(om_ref, on_ref, x_ref, mod_ref, gom_ref, gon_ref, wout_ref, g2_ref, wrh_ref, wrl_ref,
                  x1_ref, h2_ref, aff_ref):
    a = _rms(om_ref[...].astype(F32), gom_ref[...]).astype(BF16)
    b = _rms(on_ref[...].astype(F32), gon_ref[...]).astype(BF16)
    y = jnp.dot(jnp.concatenate([a, b], axis=-1), wout_ref[...], preferred_element_type=F32)
    x1 = x_ref[...] + mod_ref[0, 2:3, :] * y
    x1_ref[...] = x1
    h2 = _rms(x1, g2_ref[...]) * (1.0 + mod_ref[0, 4:5, :]) + mod_ref[0, 3:4, :]
    h2_ref[...] = h2
    hi = h2.astype(BF16)
    lo = (h2 - hi.astype(F32)).astype(BF16)
    logits = (jnp.dot(hi, wrh_ref[...], preferred_element_type=F32)
              + jnp.dot(lo, wrh_ref[...], preferred_element_type=F32)
              + jnp.dot(hi, wrl_ref[...], preferred_element_type=F32))
    lane = lax.broadcasted_iota(jnp.int32, logits.shape, 1)
    logits = jnp.where(lane < N_EXPERTS, logits, NEG)
    e = jnp.exp(logits - jnp.max(logits, axis=-1, keepdims=True))
    aff_ref[...] = e / jnp.sum(e, axis=-1, keepdims=True)


def _merge_call(om, on, x2, mod, gom, gon, wout, g2, wrh, wrl, rows_per_mod, tm):
    M, D = x2.shape
    per = rows_per_mod // tm
    row = lambda i: (i, 0)
    full = lambda i: (0, 0)
    return pl.pallas_call(
        _merge_kernel,
        out_shape=[jax.ShapeDtypeStruct((M, D), F32), jax.ShapeDtypeStruct((M, D), F32),
                   jax.ShapeDtypeStruct((M, LANES), F32)],
        grid=(M // tm,),
        in_specs=[pl.BlockSpec((tm, om.shape[1]), row), pl.BlockSpec((tm, on.shape[1]), row),
                  pl.BlockSpec((tm, D), row),
                  pl.BlockSpec((1, N_MOD, D), lambda i: (i // per, 0, 0)),
                  pl.BlockSpec((1, om.shape[1]), full), pl.BlockSpec((1, on.shape[1]), full),
                  pl.BlockSpec(wout.shape, full), pl.BlockSpec((1, D), full),
                  pl.BlockSpec(wrh.shape, full), pl.BlockSpec(wrl.shape, full)],
        out_specs=[pl.BlockSpec((tm, D), row), pl.BlockSpec((tm, D), row), pl.BlockSpec((tm, LANES), row)],
        compiler_params=_cparams(("arbitrary",)),
        name="merge_router",
    )(om, on, x2, mod, gom, gon, wout, g2, wrh, wrl)


RT_CHUNK = 256


def _route_kernel(aff_ref, slot_ref, idx_ref, gate_ref, bounds_ref, afft_ref, *, cap):
    T = aff_ref.shape[1]
    nch = T // RT_CHUNK
    capf = float(cap)

    def as_float(bits):
        return lax.bitcast_convert_type(bits, F32)

    def bs_body(i, prefix):
        cand = prefix | jnp.left_shift(jnp.int32(1), 30 - i)
        cnt = jnp.sum((aff_ref[0] >= as_float(cand)).astype(F32), axis=0, keepdims=True)
        return jnp.where(cnt >= capf, cand, prefix)

    thr_bits = lax.fori_loop(0, 31, bs_body, jnp.zeros((1, LANES), jnp.int32))
    thr = as_float(thr_bits)
    above = as_float(jnp.maximum(thr_bits + 1, jnp.int32(0x00800000)))
    n_gt = jnp.sum((aff_ref[0] >= above).astype(F32), axis=0, keepdims=True)
    need = capf - n_gt

    ri = lax.broadcasted_iota(jnp.int32, (RT_CHUNK, RT_CHUNK), 0)
    ci = lax.broadcasted_iota(jnp.int32, (RT_CHUNK, RT_CHUNK), 1)
    tri = (ci <= ri).astype(BF16)

    def ch_body(c, carry):
        run_eq, run_sel = carry
        off = _mult(c * RT_CHUNK, RT_CHUNK)
        a = aff_ref[0, pl.ds(off, RT_CHUNK), :]
        gt = a >= above
        eq = (a >= thr) & (a < above)
        eqf = eq.astype(F32)
        incl = jnp.dot(tri, eqf.astype(BF16), preferred_element_type=F32)
        sel = gt | (eq & ((incl - eqf + run_eq) < need))
        self_ = sel.astype(F32)
        incl2 = jnp.dot(tri, self_.astype(BF16), preferred_element_type=F32)
        slot = jnp.where(sel, incl2 - self_ + run_sel, -1.0)
        bounds_ref[0, pl.ds(c, 1), :] = run_sel.astype(jnp.int32)
        slot_ref[0, :, pl.ds(off, RT_CHUNK)] = slot.T[:N_EXPERTS]
        afft_ref[:, pl.ds(off, RT_CHUNK)] = a.T[:N_EXPERTS]
        return run_eq + incl[RT_CHUNK - 1:RT_CHUNK], run_sel + incl2[RT_CHUNK - 1:RT_CHUNK]

    zero = jnp.zeros((1, LANES), F32)
    bounds_ref[...] = jnp.zeros(bounds_ref.shape, jnp.int32)
    _, total = lax.fori_loop(0, nch, ch_body, (zero, zero))
    bounds_ref[0, pl.ds(nch, 1), :] = total.astype(jnp.int32)

    nsc = cap // LANES
    sub = lax.broadcasted_iota(jnp.int32, (LANES, LANES), 0).astype(F32)
    lane_f = lax.broadcasted_iota(jnp.int32, (1, LANES), 1).astype(F32)

    def cp_body(sc, carry, e):
        s0 = sc * LANES
        target = sub + _f32(s0)

        def tok_body(c, acc):
            acc_i, acc_g = acc
            off = _mult(c * LANES, LANES)
            srow = slot_ref[0, e:e + 1, pl.ds(off, LANES)]
            arow = afft_ref[e:e + 1, pl.ds(off, LANES)]
            hit = srow == target
            trow = lane_f + _f32(off)
            return acc_i + jnp.where(hit, trow, 0.0), acc_g + jnp.where(hit, arow, 0.0)

        z = jnp.zeros((LANES, LANES), F32)
        acc_i, acc_g = lax.fori_loop(0, T // LANES, tok_body, (z, z))
        s0m = _mult(s0, LANES)
        idx_ref[0, e, pl.ds(s0m, LANES), :] = jnp.sum(acc_i, axis=-1, keepdims=True).astype(jnp.int32)
        gate_ref[0, e, pl.ds(s0m, LANES), :] = jnp.sum(acc_g, axis=-1, keepdims=True)
        return carry

    for e in range(N_EXPERTS):
        lax.fori_loop(0, nsc, functools.partial(cp_body, e=e), 0)


def _route_call(aff, cap):
    B, T, _ = aff.shape
    nch = T // RT_CHUNK
    nb_rows = ((nch + 1 + 7) // 8) * 8
    E = N_EXPERTS
    return pl.pallas_call(
        functools.partial(_route_kernel, cap=cap),
        out_shape=[jax.ShapeDtypeStruct((B, E, T), F32),
                   jax.ShapeDtypeStruct((B, E, cap, 1), jnp.int32),
                   jax.ShapeDtypeStruct((B, E, cap, 1), F32),
                   jax.ShapeDtypeStruct((B, nb_rows, LANES), jnp.int32)],
        grid=(B,),
        in_specs=[pl.BlockSpec((1, T, LANES), lambda b: (b, 0, 0))],
        out_specs=[pl.BlockSpec((1, E, T), lambda b: (b, 0, 0)),
                   pl.BlockSpec((1, E, cap, 1), lambda b: (b, 0, 0, 0)),
                   pl.BlockSpec((1, E, cap, 1), lambda b: (b, 0, 0, 0)),
                   pl.BlockSpec((1, nb_rows, LANES), lambda b: (b, 0, 0))],
        scratch_shapes=[pltpu.VMEM((E, T), F32)],
        compiler_params=_cparams(("arbitrary",)),
        name="route",
    )(aff)


def _ffn_kernel(idx_ref, h2_hbm, gate_ref, wg_ref, wu_ref, wd_ref, y_ref, xs_ref, xb_ref, acc_ref, sem,
                *, cap, tokens, gchunk):
    e = pl.program_id(0)
    b = pl.program_id(1)
    f = pl.program_id(2)
    nf = pl.num_programs(2)

    @pl.when(f == 0)
    def _():
        base = (b * N_EXPERTS + e) * cap
        for g in range(cap // gchunk):
            def issue(s, carry):
                t = idx_ref[base + g * gchunk + s]
                pltpu.make_async_copy(h2_hbm.at[pl.ds(b * tokens + t, 1)], xs_ref.at[pl.ds(s, 1)], sem).start()
                return carry

            lax.fori_loop(0, gchunk, issue, 0)
            pltpu.make_async_copy(h2_hbm.at[pl.ds(0, gchunk)], xs_ref, sem).wait()
            xb_ref[g * gchunk:(g + 1) * gchunk, :] = xs_ref[...].astype(BF16)

    xb = xb_ref[...]
    a = jnp.dot(xb, wg_ref[0], preferred_element_type=F32)
    u = jnp.dot(xb, wu_ref[0], preferred_element_type=F32)
    hmid = (a * (1.0 / (1.0 + jnp.exp(-a))) * u).astype(BF16)
    part = jnp.dot(hmid, wd_ref[0], preferred_element_type=F32)

    @pl.when(f == 0)
    def _():
        acc_ref[...] = part

    @pl.when(f > 0)
    def _():
        acc_ref[...] += part

    @pl.when(f == nf - 1)
    def _():
        y_ref[0, 0] = (acc_ref[...] * gate_ref[0, 0]).astype(y_ref.dtype)


def _ffn_call(idx_flat, h2, gate, wg, wu, wd, B, T, cap):
    E, D, F = wg.shape
    tf = 512
    gchunk = min(512, cap)
    grid_spec = pltpu.PrefetchScalarGridSpec(
        num_scalar_prefetch=1,
        grid=(E, B, F // tf),
        in_specs=[pl.BlockSpec(memory_space=pl.ANY),
                  pl.BlockSpec((1, 1, cap, 1), lambda e, b, f, idx: (b, e, 0, 0)),
                  pl.BlockSpec((1, D, tf), lambda e, b, f, idx: (e, 0, f)),
                  pl.BlockSpec((1, D, tf), lambda e, b, f, idx: (e, 0, f)),
                  pl.BlockSpec((1, tf, D), lambda e, b, f, idx: (e, f, 0))],
        out_specs=pl.BlockSpec((1, 1, cap, D), lambda e, b, f, idx: (b, e, 0, 0)),
        scratch_shapes=[pltpu.VMEM((gchunk, D), F32), pltpu.VMEM((cap, D), BF16), pltpu.VMEM((cap, D), F32),
                        pltpu.SemaphoreType.DMA(())],
    )
    return pl.pallas_call(
        functools.partial(_ffn_kernel, cap=cap, tokens=T, gchunk=gchunk),
        out_shape=jax.ShapeDtypeStruct((B, E, cap, D), BF16),
        grid_spec=grid_spec,
        compiler_params=_cparams(("arbitrary", "arbitrary", "arbitrary")),
        name="expert_ffn",
    )(idx_flat, h2, gate, wg, wu, wd)


CB_WIN = 32
CB_ALIGN = 16


def _combine_kernel(bnd_ref, x1_ref, mod_ref, slot_ref, y_hbm, o_ref, buf_ref, acc_ref, sem, *, cap, nch):
    b = pl.program_id(0)
    j = pl.program_id(1)
    E = N_EXPERTS
    row0 = (b * (nch + 1) + j) * E
    row1 = row0 + E
    lo0 = []
    n_pass = jnp.int32(0)
    for e in range(E):
        base = bnd_ref[row0 + e]
        cnt = bnd_ref[row1 + e] - base
        al = (base // CB_ALIGN) * CB_ALIGN
        lo0.append(al)
        n_pass = jnp.maximum(n_pass, (base - al + cnt + CB_WIN - 1) // CB_WIN)

    acc_ref[...] = jnp.zeros(acc_ref.shape, F32)
    sub = lax.broadcasted_iota(jnp.int32, (CB_WIN, RT_CHUNK), 0).astype(F32)

    def pass_body(c, carry):
        starts = []
        for e in range(E):
            lo = lo0[e] + c * CB_WIN
            st = _mult(jnp.minimum(lo, cap - CB_WIN), CB_ALIGN)
            starts.append((lo, st))
            pltpu.make_async_copy(y_hbm.at[b, e, pl.ds(st, CB_WIN)], buf_ref.at[pl.ds(e * CB_WIN, CB_WIN)],
                                  sem).start()
        pieces = []
        for e in range(E):
            lo, st = starts[e]
            srow = slot_ref[0, e:e + 1, :]
            hit = (srow == sub + _f32(st)) & (srow >= _f32(lo))
            pieces.append(hit.astype(BF16))
        sel_t = jnp.concatenate(pieces, axis=0)
        pltpu.make_async_copy(y_hbm.at[0, 0, pl.ds(0, E * CB_WIN)], buf_ref, sem).wait()
        acc_ref[...] += lax.dot_general(sel_t, buf_ref[...], (((0,), (0,)), ((), ())),
                                        preferred_element_type=F32)
        return carry

    lax.fori_loop(0, n_pass, pass_body, 0)
    o_ref[...] = x1_ref[...] + mod_ref[0, 5:6, :] * acc_ref[...]


def _combine_call(bounds_flat, x1, mod, slot_t, y, B, T, cap):
    D = x1.shape[1]
    nch = T // RT_CHUNK
    E = N_EXPERTS
    grid_spec = pltpu.PrefetchScalarGridSpec(
        num_scalar_prefetch=1,
        grid=(B, nch),
        in_specs=[pl.BlockSpec((RT_CHUNK, D), lambda b, j, bnd: (b * nch + j, 0)),
                  pl.BlockSpec((1, N_MOD, D), lambda b, j, bnd: (b, 0, 0)),
                  pl.BlockSpec((1, E, RT_CHUNK), lambda b, j, bnd: (b, 0, j)),
                  pl.BlockSpec(memory_space=pl.ANY)],
        out_specs=pl.BlockSpec((RT_CHUNK, D), lambda b, j, bnd: (b * nch + j, 0)),
        scratch_shapes=[pltpu.VMEM((E * CB_WIN, D), BF16), pltpu.VMEM((RT_CHUNK, D), F32),
                        pltpu.SemaphoreType.DMA(())],
    )
    return pl.pallas_call(
        functools.partial(_combine_kernel, cap=cap, nch=nch),
        out_shape=jax.ShapeDtypeStruct(x1.shape, F32),
        grid_spec=grid_spec,
        compiler_params=_cparams(("arbitrary", "arbitrary")),
        name="combine",
    )(bounds_flat, x1, mod, slot_t, y)


def _rope_tables(n_tokens):
    t = jnp.arange(n_tokens)
    row = (t // GRID_W).astype(F32)
    col = (t % GRID_W).astype(F32)
    n_freq = MLA_ROPE // 4
    inv = ROPE_THETA ** (-jnp.arange(n_freq, dtype=F32) / n_freq)
    ang = jnp.concatenate([row[:, None] * inv, col[:, None] * inv], axis=-1)
    cos, sin = jnp.cos(ang), jnp.sin(ang)
    z = jnp.zeros((n_tokens, LANES - MLA_ROPE), F32)
    return jnp.concatenate([cos, cos, z], axis=-1), jnp.concatenate([-sin, sin, z], axis=-1)


def _pad_lanes(v):
    return jnp.concatenate([v, jnp.zeros((LANES - v.shape[0],), v.dtype)])[None, :]


def kernel(x, c, ctx, c_ctx, w_mod, b_mod, g_norm1, w_in, g_qa, w_qb, g_kva, w_kvb, g_q_mla, g_k_mla,
           g_q_na, g_k_na, rpb_na, g_out_mla, g_out_na, w_out, g_norm2, w_router, w_gate, w_up, w_down):
    B, T, D = x.shape
    C = ctx.shape[1]
    assert w_mod.shape[0] == 1, "single-layer problem"
    assert T % GRID_W == 0 and (T // GRID_W) % NA_QROWS == 0 and T // GRID_W >= 2 * NA_BAND
    assert w_in.shape[2] == 3 * NA_W + MLA_Q_RANK + MLA_KV_RANK + MLA_ROPE
    rows = T // GRID_W
    cap = EC_CAPACITY_FACTOR * T // N_EXPERTS
    half = MLA_ROPE // 2

    wi = w_in[0]
    o_q, o_kv, o_pe, o_na = 0, MLA_Q_RANK, MLA_Q_RANK + MLA_KV_RANK, MLA_Q_RANK + MLA_KV_RANK + MLA_ROPE
    w_in_p = jnp.concatenate(
        [wi[:, o_na:], wi[:, o_q:o_pe], wi[:, o_pe:o_na], wi[:, o_pe + half:o_na], wi[:, o_pe:o_pe + half],
         jnp.zeros((D, P_WIDTH - wi.shape[1] - MLA_ROPE), F32)], axis=1).astype(BF16)
    wq3 = w_qb[0].reshape(MLA_Q_RANK, MLA_HEADS, MLA_QK_DIM)
    wq_p = jnp.concatenate([wq3, wq3[..., MLA_NOPE + half:], wq3[..., MLA_NOPE:MLA_NOPE + half]], axis=-1)
    wq_p = wq_p.reshape(MLA_Q_RANK, MLA_HEADS * MLA_QK_PAD).astype(BF16)
    wkv3 = w_kvb[0].reshape(MLA_KV_RANK, MLA_HEADS, MLA_NOPE + MLA_V)
    wk_p = wkv3[..., :MLA_NOPE].reshape(MLA_KV_RANK, MLA_HEADS * MLA_NOPE).astype(BF16)
    wv_p = wkv3[..., MLA_NOPE:].reshape(MLA_KV_RANK, MLA_HEADS * MLA_V).astype(BF16)

    def rope_gains(g):
        gr = g[MLA_NOPE:]
        return g[None, :MLA_NOPE], _pad_lanes(gr), _pad_lanes(jnp.concatenate([gr[half:], gr[:half]]))

    gq0, gqa, gqb = rope_gains(g_q_mla[0])
    gk0, gka, gkb = rope_gains(g_k_mla[0])
    ct, st = _rope_tables(T)
    consts = dict(ct=ct, st=st, g_qa=g_qa, wq=wq_p, g_kva=g_kva, wk=wk_p, wv=wv_p,
                  gq0=gq0, gqa=gqa, gqb=gqb, gk0=gk0, gka=gka, gkb=gkb, g_q_na=g_q_na, g_k_na=g_k_na)
    wr = jnp.concatenate([w_router[0], jnp.zeros((D, LANES - N_EXPERTS), F32)], axis=1)
    wr_hi = wr.astype(BF16)
    wr_lo = (wr - wr_hi.astype(F32)).astype(BF16)
    bias = _na_bias_table(rpb_na[0], rows)

    cvec = jnp.concatenate([c, c_ctx[None, :], jnp.zeros((8 - B - 1, D), F32)], axis=0)
    mod = _mod_call(cvec.T, w_mod[0], b_mod, B + 1)
    mod_lat = mod[:B].reshape(B, N_MOD, D)
    mod_ctx = mod[B:].reshape(1, N_MOD, D)

    x2 = x.reshape(B * T, D)
    p_lat = _inproj_call(x2, mod_lat, g_norm1, w_in_p, T, min(1024, T))
    p_ctx = _inproj_call(ctx.reshape(B * C, D), mod_ctx, g_norm1, w_in_p, B * C, C)
    qm, km, vm, qn, kn = _prep_call(p_lat, B, T, min(512, T), consts, False)
    kmc, vmc, knc = _prep_call(p_ctx, B, C, C, consts, True)
    o_m = _mla_call(qm, km, vm, kmc, vmc, min(512, T), min(512, T))
    o_n = _na_call(qn, kn, p_lat, knc, p_ctx, bias)
    x1, h2, aff = _merge_call(o_m.reshape(B * T, -1), o_n.reshape(B * T, -1), x2, mod_lat,
                              g_out_mla, g_out_na, w_out[0].astype(BF16), g_norm2, wr_hi, wr_lo,
                              T, min(512, T))

    slot_t, idx, gate, bounds = _route_call(aff.reshape(B, T, LANES), cap)
    nch = T // RT_CHUNK
    y = _ffn_call(idx.reshape(-1), h2, gate, w_gate[0].astype(BF16), w_up[0].astype(BF16),
                  w_down[0].astype(BF16), B, T, cap)
    out = _combine_call(bounds[:, :nch + 1, :N_EXPERTS].reshape(-1), x1, mod_lat, slot_t, y, B, T, cap)
    return out.reshape(B, T, D)
```

```python
import functools

import numpy as np
import jax
import jax.numpy as jnp
from jax import lax
from jax.experimental import pallas as pl
from jax.experimental.pallas import tpu as pltpu

F32 = jnp.float32
BF16 = jnp.bfloat16

GRID_W = 64
MLA_HEADS = 8
MLA_NOPE = 128
MLA_ROPE = 64
MLA_QK_DIM = MLA_NOPE + MLA_ROPE
MLA_V = 128
MLA_Q_RANK = 512
MLA_KV_RANK = 256
NA_HEADS = 8
NA_HEAD_DIM = 128
NA_KH = 8
NA_KW = 16
N_EXPERTS = 16
EC_CAPACITY_FACTOR = 2
ROPE_THETA = 10000.0
EPS = 1e-6
N_MOD = 6
MLA_SCALE = MLA_QK_DIM ** -0.5
NA_SCALE = NA_HEAD_DIM ** -0.5
LOG2E = 1.4426950408889634

LANES = 128
MLA_QK_PAD = 2 * LANES
NA_W = NA_HEADS * NA_HEAD_DIM
P_WIDTH = 3 * NA_W + 1024
VMEM_LIMIT = 56 * 1024 * 1024
NEG = -1e30


def _f32(v):
    return jnp.asarray(v, dtype=F32)


def _mult(v, m):
    return v if isinstance(v, int) else pl.multiple_of(v, m)


def _cparams(sem, vmem=VMEM_LIMIT):
    return pltpu.CompilerParams(dimension_semantics=sem, vmem_limit_bytes=vmem)


def _mod_kernel(ct_ref, w_ref, b_ref, o_ref):
    ct = ct_ref[...]
    s = ct * (1.0 / (1.0 + jnp.exp(-ct)))
    w = w_ref[...]
    rows = []
    for m in range(o_ref.shape[0]):
        rows.append(jnp.sum(w * s[:, m:m + 1], axis=0, keepdims=True))
    o_ref[...] = jnp.concatenate(rows, axis=0) + b_ref[...]


def _mod_call(ct, w_mod, b_mod, n_rows):
    D, N = w_mod.shape
    tn = 512
    return pl.pallas_call(
        _mod_kernel,
        out_shape=jax.ShapeDtypeStruct((n_rows, N), F32),
        grid=(N // tn,),
        in_specs=[pl.BlockSpec((D, 8), lambda j: (0, 0)),
                  pl.BlockSpec((D, tn), lambda j: (0, j)),
                  pl.BlockSpec((1, tn), lambda j: (0, j))],
        out_specs=pl.BlockSpec((n_rows, tn), lambda j: (0, j)),
        compiler_params=_cparams(("arbitrary",)),
        name="mod",
    )(ct, w_mod, b_mod)


def _inproj_kernel(x_ref, mod_ref, g_ref, w_ref, o_ref, hn_ref):
    @pl.when(pl.program_id(1) == 0)
    def _():
        x = x_ref[...]
        y = x * lax.rsqrt(jnp.mean(x * x, axis=-1, keepdims=True) + EPS) * g_ref[...]
        h = y * (1.0 + mod_ref[0, 1:2, :]) + mod_ref[0, 0:1, :]
        hn_ref[...] = h.astype(BF16)

    o_ref[...] = jnp.dot(hn_ref[...], w_ref[...], preferred_element_type=F32).astype(BF16)


def _inproj_call(x2, mod, g, w, rows_per_mod, tm):
    M, D = x2.shape
    N = w.shape[1]
    tn = 1024
    per = rows_per_mod // tm
    return pl.pallas_call(
        _inproj_kernel,
        out_shape=jax.ShapeDtypeStruct((M, N), BF16),
        grid=(M // tm, N // tn),
        in_specs=[pl.BlockSpec((tm, D), lambda i, j: (i, 0)),
                  pl.BlockSpec((1, N_MOD, D), lambda i, j: (i // per, 0, 0)),
                  pl.BlockSpec((1, D), lambda i, j: (0, 0)),
                  pl.BlockSpec((D, tn), lambda i, j: (0, j))],
        out_specs=pl.BlockSpec((tm, tn), lambda i, j: (i, j)),
        scratch_shapes=[pltpu.VMEM((tm, D), BF16)],
        compiler_params=_cparams(("arbitrary", "arbitrary")),
        name="inproj",
    )(x2, mod, g, w)


def _rms(x, g):
    return x * lax.rsqrt(jnp.mean(x * x, axis=-1, keepdims=True) + EPS) * g


def _prep_kernel(*refs, is_ctx):
    if is_ctx:
        (pk_ref, pm_ref, gkva_ref, wk_ref, wv_ref, gk0_ref, gka_ref, gkn_ref,
         km_ref, vm_ref, kn_ref) = refs
    else:
        (pq_ref, pk_ref, pm_ref, ct_ref, st_ref, gqa_ref, wq_ref, gkva_ref, wk_ref, wv_ref,
         gq0_ref, gqa2_ref, gqb2_ref, gk0_ref, gka_ref, gkb_ref, gqn_ref, gkn_ref,
         qm_ref, km_ref, vm_ref, qn_ref, kn_ref) = refs
    tm = pm_ref.shape[0]
    lane = lax.broadcasted_iota(jnp.int32, (1, LANES), 1)
    rope_mask = (lane < MLA_ROPE).astype(F32)
    pm = pm_ref[...].astype(F32)
    inv_qk = 1.0 / MLA_QK_DIM

    ckvn = _rms(pm[:, MLA_Q_RANK:MLA_Q_RANK + MLA_KV_RANK], gkva_ref[...]).astype(BF16)
    kn = jnp.dot(ckvn, wk_ref[...], preferred_element_type=F32)
    vv = jnp.dot(ckvn, wv_ref[...], preferred_element_type=F32)
    kpe = pm[:, MLA_Q_RANK + MLA_KV_RANK:MLA_Q_RANK + MLA_KV_RANK + LANES]
    ss_pe = jnp.sum(kpe * kpe * rope_mask, axis=-1, keepdims=True)
    if is_ctx:
        k_rope = kpe * gka_ref[...]
    else:
        ct = ct_ref[...]
        st = st_ref[...]
        k_rope = kpe * (ct * gka_ref[...]) + pltpu.roll(kpe, MLA_ROPE, 1) * (st * gkb_ref[...])
    for h in range(MLA_HEADS):
        k0 = kn[:, h * LANES:(h + 1) * LANES]
        r = lax.rsqrt((jnp.sum(k0 * k0, axis=-1, keepdims=True) + ss_pe) * inv_qk + EPS)
        km_ref[0, h] = jnp.concatenate([k0 * r * gk0_ref[...], k_rope * r], axis=-1).astype(BF16)
        vm_ref[0, h] = jnp.concatenate([vv[:, h * LANES:(h + 1) * LANES], jnp.ones((tm, LANES), F32)],
                                       axis=-1).astype(BF16)

    pk = pk_ref[...].astype(F32)
    for h in range(NA_HEADS):
        kn_ref[0, h] = _rms(pk[:, h * LANES:(h + 1) * LANES], gkn_ref[...]).astype(BF16)

    if is_ctx:
        return

    cqn = _rms(pm[:, :MLA_Q_RANK], gqa_ref[...]).astype(BF16)
    q = jnp.dot(cqn, wq_ref[...], preferred_element_type=F32)
    qa = ct * gqa2_ref[...]
    qb = st * gqb2_ref[...]
    for h in range(MLA_HEADS):
        t0 = q[:, h * MLA_QK_PAD:h * MLA_QK_PAD + LANES]
        t1 = q[:, h * MLA_QK_PAD + LANES:(h + 1) * MLA_QK_PAD]
        ss = jnp.sum(t0 * t0, axis=-1, keepdims=True) + jnp.sum(t1 * t1 * rope_mask, axis=-1, keepdims=True)
        r = lax.rsqrt(ss * inv_qk + EPS) * (MLA_SCALE * LOG2E)
        o1 = t1 * qa + pltpu.roll(t1, MLA_ROPE, 1) * qb
        qm_ref[0, h] = jnp.concatenate([t0 * r * gq0_ref[...], o1 * r], axis=-1).astype(BF16)

    pq = pq_ref[...].astype(F32)
    for h in range(NA_HEADS):
        qn_ref[0, h] = (_rms(pq[:, h * LANES:(h + 1) * LANES], gqn_ref[...]) * NA_SCALE).astype(BF16)


def _prep_call(p, nb, tok, tm, consts, is_ctx):
    per = tok // tm
    H = MLA_HEADS
    row = lambda i: (i, 0)
    full = lambda i: (0, 0)
    hm = lambda i: (i // per, 0, i % per, 0)
    p_spec = lambda c: pl.BlockSpec((tm, 1024), lambda i, c=c: (i, c))
    vec = lambda n: pl.BlockSpec((1, n), full)
    km_s = jax.ShapeDtypeStruct((nb, H, tok, MLA_QK_PAD), BF16)
    h128_s = jax.ShapeDtypeStruct((nb, H, tok, LANES), BF16)
    km_o = pl.BlockSpec((1, H, tm, MLA_QK_PAD), hm)
    h128_o = pl.BlockSpec((1, H, tm, LANES), hm)
    c = consts
    if is_ctx:
        ins = [p, p, c["g_kva"], c["wk"], c["wv"], c["gk0"], c["gka"], c["g_k_na"]]
        in_specs = [p_spec(1), p_spec(3), vec(MLA_KV_RANK),
                    pl.BlockSpec(c["wk"].shape, full), pl.BlockSpec(c["wv"].shape, full),
                    vec(LANES), vec(LANES), vec(LANES)]
        out_shape = [km_s, km_s, h128_s]
        out_specs = [km_o, km_o, h128_o]
    else:
        ins = [p, p, p, c["ct"], c["st"], c["g_qa"], c["wq"], c["g_kva"], c["wk"], c["wv"],
               c["gq0"], c["gqa"], c["gqb"], c["gk0"], c["gka"], c["gkb"], c["g_q_na"], c["g_k_na"]]
        tab = pl.BlockSpec((tm, LANES), lambda i: (i % per, 0))
        in_specs = [p_spec(0), p_spec(1), p_spec(3), tab, tab, vec(MLA_Q_RANK),
                    pl.BlockSpec(c["wq"].shape, full), vec(MLA_KV_RANK),
                    pl.BlockSpec(c["wk"].shape, full), pl.BlockSpec(c["wv"].shape, full),
                    vec(LANES), vec(LANES), vec(LANES), vec(LANES), vec(LANES), vec(LANES),
                    vec(LANES), vec(LANES)]
        out_shape = [km_s, km_s, km_s, h128_s, h128_s]
        out_specs = [km_o, km_o, km_o, h128_o, h128_o]
    return pl.pallas_call(
        functools.partial(_prep_kernel, is_ctx=is_ctx),
        out_shape=out_shape,
        grid=(nb * per,),
        in_specs=in_specs,
        out_specs=out_specs,
        compiler_params=_cparams(("arbitrary",)),
        name="prep_ctx" if is_ctx else "prep_lat",
    )(*ins)


MLA_SUB = 2


def _mla_kernel(q_ref, k_ref, v_ref, kc_ref, vc_ref, o_ref, m_ref, acc_ref, sa_ref, sb_ref, sc_ref, *, tk):
    ts = q_ref.shape[2] // MLA_SUB
    n = k_ref.shape[2] // tk
    m_ref[...] = jnp.full(m_ref.shape, NEG, F32)
    acc_ref[...] = jnp.zeros(acc_ref.shape, F32)
    dn = (((1,), (1,)), ((), ()))

    def scores(c):
        return lax.dot_general(q_ref[0, 0], k_ref[0, 0, pl.ds(_mult(c * tk, tk), tk), :], dn,
                               preferred_element_type=F32)

    def values(c):
        return v_ref[0, 0, pl.ds(_mult(c * tk, tk), tk), :]

    def softmax_pv(s_ref, v):
        for u in range(MLA_SUB):
            rows = slice(u * ts, (u + 1) * ts)
            m_old = m_ref[rows, :]
            m_new = jnp.maximum(m_old, jnp.max(s_ref[rows, :], axis=-1, keepdims=True))
            p = jnp.exp2(s_ref[rows, :] - m_new).astype(BF16)
            acc_ref[rows, :] = (jnp.exp2(m_old - m_new) * acc_ref[rows, :]
                                + jnp.dot(p, v, preferred_element_type=F32))
            m_ref[rows, :] = m_new

    sa_ref[...] = scores(0)

    def body(i, carry):
        c = 2 * i
        sb_ref[...] = scores(c + 1)
        softmax_pv(sa_ref, values(c))
        sa_ref[...] = scores(c + 2)
        softmax_pv(sb_ref, values(c + 1))
        return carry

    lax.fori_loop(0, n // 2 - 1, body, 0)
    sb_ref[...] = scores(n - 1)
    softmax_pv(sa_ref, values(n - 2))
    sc_ref[...] = lax.dot_general(q_ref[0, 0], kc_ref[0, 0], dn, preferred_element_type=F32)
    softmax_pv(sb_ref, values(n - 1))
    softmax_pv(sc_ref, vc_ref[0, 0])
    o_ref[0] = (acc_ref[:, :MLA_V] / acc_ref[:, MLA_V:]).astype(o_ref.dtype)


def _mla_call(qm, km, vm, kmc, vmc, tq, tk):
    B, H, T, _ = qm.shape
    C = kmc.shape[2]
    VW = vm.shape[3]
    return pl.pallas_call(
        functools.partial(_mla_kernel, tk=tk),
        out_shape=jax.ShapeDtypeStruct((B, T, H * MLA_V), BF16),
        grid=(B, H, T // tq),
        in_specs=[pl.BlockSpec((1, 1, tq, MLA_QK_PAD), lambda b, h, i: (b, h, i, 0)),
                  pl.BlockSpec((1, 1, T, MLA_QK_PAD), lambda b, h, i: (b, h, 0, 0)),
                  pl.BlockSpec((1, 1, T, VW), lambda b, h, i: (b, h, 0, 0)),
                  pl.BlockSpec((1, 1, C, MLA_QK_PAD), lambda b, h, i: (b, h, 0, 0)),
                  pl.BlockSpec((1, 1, C, VW), lambda b, h, i: (b, h, 0, 0))],
        out_specs=pl.BlockSpec((1, tq, MLA_V), lambda b, h, i: (b, i, h)),
        scratch_shapes=[pltpu.VMEM((tq, 1), F32), pltpu.VMEM((tq, VW), F32),
                        pltpu.VMEM((tq, tk), F32), pltpu.VMEM((tq, tk), F32), pltpu.VMEM((tq, C), F32)],
        compiler_params=_cparams(("arbitrary", "arbitrary", "arbitrary")),
        name="mla_attn",
    )(qm, km, vm, kmc, vmc)


NA_QROWS = 2
NA_TILE = NA_QROWS * GRID_W
NA_BAND = 5
NA_CFGS = 5


def _na_band_start(i, nblk):
    return jnp.clip(i - 2, 0, nblk - NA_BAND)


def _na_kernel(q_ref, k0, k1, k2, k3, k4, v0, v1, v2, v3, v4, kc_ref, vc_ref, bias_ref, o_ref):
    k_refs = (k0, k1, k2, k3, k4)
    v_refs = (v0, v1, v2, v3, v4)
    vc = vc_ref[...]
    vcat = jnp.concatenate([r[...] for r in v_refs], axis=0)
    for h in range(NA_HEADS):
        q = q_ref[0, h]
        kcat = jnp.concatenate([r[0, h] for r in k_refs], axis=0)
        dn = (((1,), (1,)), ((), ()))
        s_loc = lax.dot_general(q, kcat, dn, preferred_element_type=F32) + bias_ref[0, h]
        s_ctx = lax.dot_general(q, kc_ref[0, h], dn, preferred_element_type=F32)
        m = jnp.maximum(jnp.max(s_loc, axis=-1, keepdims=True), jnp.max(s_ctx, axis=-1, keepdims=True))
        p_loc = jnp.exp(s_loc - m)
        p_ctx = jnp.exp(s_ctx - m)
        l = jnp.sum(p_loc, axis=-1, keepdims=True) + jnp.sum(p_ctx, axis=-1, keepdims=True)
        sl = slice(h * LANES, (h + 1) * LANES)
        o = (jnp.dot(p_loc.astype(BF16), vcat[:, sl], preferred_element_type=F32)
             + jnp.dot(p_ctx.astype(BF16), vc[:, sl], preferred_element_type=F32))
        o_ref[0, :, sl] = (o / l).astype(o_ref.dtype)


def _na_call(qn, kn, p_lat, knc, p_ctx, bias):
    B, H, T, _ = qn.shape
    C = knc.shape[2]
    nblk = T // NA_TILE
    tiles_per_b = T // NA_TILE

    def cfg(i):
        return jnp.minimum(i, 2) + jnp.maximum(i - (nblk - 3), 0)

    k_specs = [pl.BlockSpec((1, H, NA_TILE, LANES),
                            lambda b, i, u=u: (b, 0, _na_band_start(i, nblk) + u, 0)) for u in range(NA_BAND)]
    v_specs = [pl.BlockSpec((NA_TILE, NA_W),
                            lambda b, i, u=u: (b * tiles_per_b + _na_band_start(i, nblk) + u, 2))
               for u in range(NA_BAND)]
    return pl.pallas_call(
        _na_kernel,
        out_shape=jax.ShapeDtypeStruct((B, T, NA_W), BF16),
        grid=(B, nblk),
        in_specs=[pl.BlockSpec((1, H, NA_TILE, LANES), lambda b, i: (b, 0, i, 0))] + k_specs + v_specs + [
            pl.BlockSpec((1, H, C, LANES), lambda b, i: (b, 0, 0, 0)),
            pl.BlockSpec((C, NA_W), lambda b, i: (b, 2)),
            pl.BlockSpec((1, H, NA_TILE, NA_BAND * NA_TILE), lambda b, i: (cfg(i), 0, 0, 0))],
        out_specs=pl.BlockSpec((1, NA_TILE, NA_W), lambda b, i: (b, i, 0)),
        compiler_params=_cparams(("arbitrary", "arbitrary")),
        name="na_attn",
    )(qn, *([kn] * NA_BAND), *([p_lat] * NA_BAND), knc, p_ctx, bias)


def _na_bias_table(rpb, rows):
    nblk = rows // NA_QROWS
    reps = [0, 1, 2, nblk - 2, nblk - 1]
    q_r = np.arange(NA_TILE) // GRID_W
    q_c = np.arange(NA_TILE) % GRID_W
    k_r = np.arange(NA_BAND * NA_TILE) // GRID_W
    k_c = np.arange(NA_BAND * NA_TILE) % GRID_W
    padw = GRID_W - NA_KW
    rp = jnp.pad(rpb, ((0, 0), (0, 0), (padw, padw)))
    toep = jnp.stack([rp[:, :, GRID_W - 1 - qc:2 * GRID_W - 1 - qc] for qc in range(GRID_W)], axis=2)
    tables, ok = [], []
    for i in reps:
        j0 = int(np.clip(i - 2, 0, nblk - NA_BAND))
        r = NA_QROWS * i + q_r
        rs = np.clip(r - NA_KH // 2, 0, rows - NA_KH)
        cs = np.clip(q_c - NA_KW // 2, 0, GRID_W - NA_KW)
        kr = NA_QROWS * j0 + k_r
        ok.append((kr[None, :] >= rs[:, None]) & (kr[None, :] < rs[:, None] + NA_KH)
                  & (k_c[None, :] >= cs[:, None]) & (k_c[None, :] < cs[:, None] + NA_KW))
        per_qr = []
        for qr in range(NA_QROWS):
            pieces = []
            for kb in range(NA_BAND * NA_QROWS):
                ro = int(np.clip(NA_QROWS * (j0 - i) + kb - qr + NA_KH - 1, 0, 2 * NA_KH - 2))
                pieces.append(toep[:, ro])
            per_qr.append(jnp.concatenate(pieces, axis=-1))
        tables.append(jnp.concatenate(per_qr, axis=1))
    return jnp.where(np.stack(ok)[:, None], jnp.stack(tables), NEG).astype(F32)


def _merge_kernel(om_ref, on_ref, x_ref, mod_ref, gom_ref, gon_ref, wout_ref, g2_ref, wrh_ref, wrl_ref,
                  x1_ref, h2_ref, aff_ref):
    a = _rms(om_ref[...].astype(F32), gom_ref[...]).astype(BF16)
    b = _rms(on_ref[...].astype(F32), gon_ref[...]).astype(BF16)
    y = jnp.dot(jnp.concatenate([a, b], axis=-1), wout_ref[...], preferred_element_type=F32)
    x1 = x_ref[...] + mod_ref[0, 2:3, :] * y
    x1_ref[...] = x1
    h2 = _rms(x1, g2_ref[...]) * (1.0 + mod_ref[0, 4:5, :]) + mod_ref[0, 3:4, :]
    h2_ref[...] = h2
    hi = h2.astype(BF16)
    lo = (h2 - hi.astype(F32)).astype(BF16)
    logits = (jnp.dot(hi, wrh_ref[...], preferred_element_type=F32)
              + jnp.dot(lo, wrh_ref[...], preferred_element_type=F32)
              + jnp.dot(hi, wrl_ref[...], preferred_element_type=F32))
    lane = lax.broadcasted_iota(jnp.int32, logits.shape, 1)
    logits = jnp.where(lane < N_EXPERTS, logits, NEG)
    e = jnp.exp(logits - jnp.max(logits, axis=-1, keepdims=True))
    aff_ref[...] = e / jnp.sum(e, axis=-1, keepdims=True)


def _merge_call(om, on, x2, mod, gom, gon, wout, g2, wrh, wrl, rows_per_mod, tm):
    M, D = x2.shape
    per = rows_per_mod // tm
    row = lambda i: (i, 0)
    full = lambda i: (0, 0)
    return pl.pallas_call(
        _merge_kernel,
        out_shape=[jax.ShapeDtypeStruct((M, D), F32), jax.ShapeDtypeStruct((M, D), F32),
                   jax.ShapeDtypeStruct((M, LANES), F32)],
        grid=(M // tm,),
        in_specs=[pl.BlockSpec((tm, om.shape[1]), row), pl.BlockSpec((tm, on.shape[1]), row),
                  pl.BlockSpec((tm, D), row),
                  pl.BlockSpec((1, N_MOD, D), lambda i: (i // per, 0, 0)),
                  pl.BlockSpec((1, om.shape[1]), full), pl.BlockSpec((1, on.shape[1]), full),
                  pl.BlockSpec(wout.shape, full), pl.BlockSpec((1, D), full),
                  pl.BlockSpec(wrh.shape, full), pl.BlockSpec(wrl.shape, full)],
        out_specs=[pl.BlockSpec((tm, D), row), pl.BlockSpec((tm, D), row), pl.BlockSpec((tm, LANES), row)],
        compiler_params=_cparams(("arbitrary",)),
        name="merge_router",
    )(om, on, x2, mod, gom, gon, wout, g2, wrh, wrl)


RT_CHUNK = 256


def _select_kernel(aff_ref, slot_ref, afft_ref, bounds_ref, *, cap):
    T = aff_ref.shape[1]
    E = N_EXPERTS
    n_tiles = T // LANES
    capf = float(cap)

    def tr_body(c, carry):
        off = _mult(c * RT_CHUNK, RT_CHUNK)
        afft_ref[0, :, pl.ds(off, RT_CHUNK)] = aff_ref[0, pl.ds(off, RT_CHUNK), :].T[:E]
        return carry

    lax.fori_loop(0, T // RT_CHUNK, tr_body, 0)

    def as_float(bits):
        return lax.bitcast_convert_type(bits, F32)

    def count_ge(v):
        return jnp.sum((afft_ref[0] >= v).astype(F32), axis=1, keepdims=True)

    def bs_body(i, prefix):
        cand = prefix | jnp.left_shift(jnp.int32(1), 30 - i)
        return jnp.where(count_ge(as_float(cand)) >= capf, cand, prefix)

    thr_bits = lax.fori_loop(0, 31, bs_body, jnp.zeros((E, 1), jnp.int32))
    thr = as_float(thr_bits)
    above = as_float(jnp.maximum(thr_bits + 1, jnp.int32(0x00800000)))
    need = capf - count_ge(above)

    ri = lax.broadcasted_iota(jnp.int32, (LANES, LANES), 0)
    ci = lax.broadcasted_iota(jnp.int32, (LANES, LANES), 1)
    upper = (ri <= ci).astype(BF16)
    lane = lax.broadcasted_iota(jnp.int32, (E, LANES), 1)
    run_eq = jnp.zeros((E, 1), F32)
    run_sel = jnp.zeros((E, 1), F32)
    bounds = jnp.zeros((E, LANES), F32)
    for k in range(n_tiles):
        a = afft_ref[0, :, k * LANES:(k + 1) * LANES]
        gt = a >= above
        eq = (a >= thr) & (a < above)
        eqf = eq.astype(F32)
        incl_eq = jnp.dot(eqf.astype(BF16), upper, preferred_element_type=F32)
        sel = gt | (eq & ((incl_eq - eqf + run_eq) < need))
        self_ = sel.astype(F32)
        incl_sel = jnp.dot(self_.astype(BF16), upper, preferred_element_type=F32)
        slot_ref[0, :, k * LANES:(k + 1) * LANES] = jnp.where(sel, incl_sel - self_ + run_sel, -1.0)
        bounds = jnp.where(lane == k, run_sel, bounds)
        run_eq = run_eq + incl_eq[:, LANES - 1:LANES]
        run_sel = run_sel + incl_sel[:, LANES - 1:LANES]
    bounds_ref[0] = jnp.where(lane >= n_tiles, run_sel, bounds).astype(jnp.int32)


def _compact_kernel(bnd_ref, slot_ref, afft_ref, idx_ref, gate_ref, acc_i_ref, acc_g_ref, *, cap):
    b = pl.program_id(0)
    T = slot_ref.shape[2]
    n_tiles = T // LANES
    sub = lax.broadcasted_iota(jnp.int32, (LANES, LANES), 0).astype(F32)
    lane_f = lax.broadcasted_iota(jnp.int32, (1, LANES), 1).astype(F32)

    for e in range(N_EXPERTS):
        acc_i_ref[...] = jnp.zeros(acc_i_ref.shape, F32)
        acc_g_ref[...] = jnp.zeros(acc_g_ref.shape, F32)
        row = (b * N_EXPERTS + e) * LANES

        def tile_body(k, carry, e=e, row=row):
            lo = bnd_ref[row + k]
            hi = bnd_ref[row + k + 1]
            off = _mult(k * LANES, LANES)
            srow = slot_ref[0, e:e + 1, pl.ds(off, LANES)]
            arow = afft_ref[0, e:e + 1, pl.ds(off, LANES)]
            trow = lane_f + _f32(off)

            def block_body(sb, c2):
                s0 = _mult(sb * LANES, LANES)
                hit = srow == (sub + _f32(s0))
                acc_i_ref[pl.ds(s0, LANES), :] += jnp.where(hit, trow, 0.0)
                acc_g_ref[pl.ds(s0, LANES), :] += jnp.where(hit, arow, 0.0)
                return c2

            lax.fori_loop(lo // LANES, (hi + LANES - 1) // LANES, block_body, 0)
            return carry

        lax.fori_loop(0, n_tiles, tile_body, 0)
        idx_ref[0, e] = jnp.sum(acc_i_ref[...], axis=-1, keepdims=True).astype(jnp.int32)
        gate_ref[0, e] = jnp.sum(acc_g_ref[...], axis=-1, keepdims=True)


def _route_call(aff, cap):
    B, T, _ = aff.shape
    E = N_EXPERTS
    assert T // LANES < LANES
    bet = lambda b: (b, 0, 0)
    slot_t, aff_t, bounds = pl.pallas_call(
        functools.partial(_select_kernel, cap=cap),
        out_shape=[jax.ShapeDtypeStruct((B, E, T), F32), jax.ShapeDtypeStruct((B, E, T), F32),
                   jax.ShapeDtypeStruct((B, E, LANES), jnp.int32)],
        grid=(B,),
        in_specs=[pl.BlockSpec((1, T, LANES), bet)],
        out_specs=[pl.BlockSpec((1, E, T), bet), pl.BlockSpec((1, E, T), bet), pl.BlockSpec((1, E, LANES), bet)],
        compiler_params=_cparams(("arbitrary",)),
        name="route_select",
    )(aff)
    bounds_flat = bounds.reshape(-1)
    grid_spec = pltpu.PrefetchScalarGridSpec(
        num_scalar_prefetch=1,
        grid=(B,),
        in_specs=[pl.BlockSpec((1, E, T), lambda b, bnd: (b, 0, 0)),
                  pl.BlockSpec((1, E, T), lambda b, bnd: (b, 0, 0))],
        out_specs=[pl.BlockSpec((1, E, cap, 1), lambda b, bnd: (b, 0, 0, 0)),
                   pl.BlockSpec((1, E, cap, 1), lambda b, bnd: (b, 0, 0, 0))],
        scratch_shapes=[pltpu.VMEM((cap, LANES), F32), pltpu.VMEM((cap, LANES), F32)],
    )
    idx, gate = pl.pallas_call(
        functools.partial(_compact_kernel, cap=cap),
        out_shape=[jax.ShapeDtypeStruct((B, E, cap, 1), jnp.int32), jax.ShapeDtypeStruct((B, E, cap, 1), F32)],
        grid_spec=grid_spec,
        compiler_params=_cparams(("arbitrary",)),
        name="route_compact",
    )(bounds_flat, slot_t, aff_t)
    return slot_t, idx, gate, bounds_flat


def _ffn_kernel(idx_ref, h2_hbm, gate_ref, wg_ref, wu_ref, wd_ref, y_ref, xs_ref, xb_ref, acc_ref, sem,
                *, cap, tokens, n_batch, n_steps):
    e = pl.program_id(0)
    b = pl.program_id(1)
    f = pl.program_id(2)
    n_exp = pl.num_programs(0)
    nf = pl.num_programs(2)
    rows_per_step = cap // n_steps

    def row_copy(bb, ee, s):
        t = idx_ref[(bb * n_exp + ee) * cap + s]
        return pltpu.make_async_copy(h2_hbm.at[pl.ds(bb * tokens + t, 1)], xs_ref.at[pl.ds(s, 1)], sem)

    def wait_rows():
        pltpu.make_async_copy(h2_hbm.at[pl.ds(0, cap)], xs_ref, sem).wait()

    @pl.when((e == 0) & (b == 0) & (f == 0))
    def _():
        def issue(s, carry):
            row_copy(b, e, s).start()
            return carry

        lax.fori_loop(0, cap, issue, 0, unroll=8)

    @pl.when(f == 0)
    def _():
        wait_rows()
        xb_ref[...] = xs_ref[...].astype(BF16)

    wrap_b = b + 1 == n_batch
    nb = jnp.where(wrap_b, 0, b + 1)
    ne = jnp.where(wrap_b, jnp.where(e + 1 == n_exp, 0, e + 1), e)
    for k in range(rows_per_step):
        row_copy(nb, ne, f * rows_per_step + k).start()

    xb = xb_ref[...]
    a = jnp.dot(xb, wg_ref[0].astype(BF16), preferred_element_type=F32)
    u = jnp.dot(xb, wu_ref[0].astype(BF16), preferred_element_type=F32)
    hmid = (a * (1.0 / (1.0 + jnp.exp(-a))) * u).astype(BF16)
    part = jnp.dot(hmid, wd_ref[0].astype(BF16), preferred_element_type=F32)

    @pl.when(f == 0)
    def _():
        acc_ref[...] = part

    @pl.when(f > 0)
    def _():
        acc_ref[...] += part

    @pl.when(f == nf - 1)
    def _():
        y_ref[0, 0] = (acc_ref[...] * gate_ref[0, 0]).astype(y_ref.dtype)

    @pl.when((e == n_exp - 1) & (b == n_batch - 1) & (f == nf - 1))
    def _():
        wait_rows()


FFN_TF = 256


def _ffn_call(idx_flat, h2, gate, wg, wu, wd, B, T, cap):
    E, D, F = wg.shape
    tf = FFN_TF
    grid_spec = pltpu.PrefetchScalarGridSpec(
        num_scalar_prefetch=1,
        grid=(E, B, F // tf),
        in_specs=[pl.BlockSpec(memory_space=pl.ANY),
                  pl.BlockSpec((1, 1, cap, 1), lambda e, b, f, idx: (b, e, 0, 0)),
                  pl.BlockSpec((1, D, tf), lambda e, b, f, idx: (e, 0, f)),
                  pl.BlockSpec((1, D, tf), lambda e, b, f, idx: (e, 0, f)),
                  pl.BlockSpec((1, tf, D), lambda e, b, f, idx: (e, f, 0))],
        out_specs=pl.BlockSpec((1, 1, cap, D), lambda e, b, f, idx: (b, e, 0, 0)),
        scratch_shapes=[pltpu.VMEM((cap, D), F32), pltpu.VMEM((cap, D), BF16), pltpu.VMEM((cap, D), F32),
                        pltpu.SemaphoreType.DMA(())],
    )
    return pl.pallas_call(
        functools.partial(_ffn_kernel, cap=cap, tokens=T, n_batch=B, n_steps=F // tf),
        out_shape=jax.ShapeDtypeStruct((B, E, cap, D), BF16),
        grid_spec=grid_spec,
        compiler_params=_cparams(("arbitrary", "arbitrary", "arbitrary")),
        name="expert_ffn",
    )(idx_flat, h2, gate, wg, wu, wd)


CB_WIN = 64
CB_ALIGN = 16


def _combine_kernel(bnd_ref, x1_ref, mod_ref, slot_ref, y_hbm, o_ref, buf_ref, acc_ref, sem,
                    *, cap, nch, n_batch):
    b = pl.program_id(0)
    j = pl.program_id(1)
    E = N_EXPERTS
    step = b * nch + j
    cur = step % 2
    sub = lax.broadcasted_iota(jnp.int32, (CB_WIN, RT_CHUNK), 0).astype(F32)

    def tile_windows(bb, jj):
        per_tile = RT_CHUNK // LANES
        aligned, n_pass = [], jnp.int32(0)
        for e in range(E):
            at = (bb * E + e) * LANES + jj * per_tile
            base = bnd_ref[at]
            cnt = bnd_ref[at + per_tile] - base
            al = (base // CB_ALIGN) * CB_ALIGN
            aligned.append(al)
            n_pass = jnp.maximum(n_pass, (base - al + cnt + CB_WIN - 1) // CB_WIN)
        return aligned, n_pass

    def window(al, c):
        lo = al + c * CB_WIN
        return lo, _mult(jnp.minimum(lo, cap - CB_WIN), CB_ALIGN)

    def start_fetch(bb, aligned, c, slot):
        for e in range(E):
            _, st = window(aligned[e], c)
            pltpu.make_async_copy(y_hbm.at[bb, e, pl.ds(st, CB_WIN)],
                                  buf_ref.at[slot, pl.ds(e * CB_WIN, CB_WIN)], sem.at[slot]).start()

    def wait_fetch(slot):
        pltpu.make_async_copy(buf_ref.at[slot], buf_ref.at[slot], sem.at[slot]).wait()

    def scatter_add(aligned, c, slot):
        pieces = []
        for e in range(E):
            lo, st = window(aligned[e], c)
            srow = slot_ref[0, e:e + 1, :]
            pieces.append(((srow == sub + _f32(st)) & (srow >= _f32(lo))).astype(BF16))
        sel_t = jnp.concatenate(pieces, axis=0)
        return lax.dot_general(sel_t, buf_ref[slot], (((0,), (0,)), ((), ())), preferred_element_type=F32)

    aligned, n_pass = tile_windows(b, j)

    @pl.when(step == 0)
    def _():
        start_fetch(b, aligned, 0, cur)

    @pl.when(step + 1 < n_batch * nch)
    def _():
        wrap = j + 1 == nch
        nb = jnp.where(wrap, b + 1, b)
        nj = jnp.where(wrap, 0, j + 1)
        start_fetch(nb, tile_windows(nb, nj)[0], 0, 1 - cur)

    wait_fetch(cur)
    acc_ref[...] = scatter_add(aligned, 0, cur)

    def extra_pass(c, carry):
        start_fetch(b, aligned, c, cur)
        wait_fetch(cur)
        acc_ref[...] += scatter_add(aligned, c, cur)
        return carry

    lax.fori_loop(1, n_pass, extra_pass, 0)
    o_ref[...] = x1_ref[...] + mod_ref[0, 5:6, :] * acc_ref[...]


def _combine_call(bounds_flat, x1, mod, slot_t, y, B, T, cap):
    D = x1.shape[1]
    nch = T // RT_CHUNK
    E = N_EXPERTS
    grid_spec = pltpu.PrefetchScalarGridSpec(
        num_scalar_prefetch=1,
        grid=(B, nch),
        in_specs=[pl.BlockSpec((RT_CHUNK, D), lambda b, j, bnd: (b * nch + j, 0)),
                  pl.BlockSpec((1, N_MOD, D), lambda b, j, bnd: (b, 0, 0)),
                  pl.BlockSpec((1, E, RT_CHUNK), lambda b, j, bnd: (b, 0, j)),
                  pl.BlockSpec(memory_space=pl.ANY)],
        out_specs=pl.BlockSpec((RT_CHUNK, D), lambda b, j, bnd: (b * nch + j, 0)),
        scratch_shapes=[pltpu.VMEM((2, E * CB_WIN, D), BF16), pltpu.VMEM((RT_CHUNK, D), F32),
                        pltpu.SemaphoreType.DMA((2,))],
    )
    return pl.pallas_call(
        functools.partial(_combine_kernel, cap=cap, nch=nch, n_batch=B),
        out_shape=jax.ShapeDtypeStruct(x1.shape, F32),
        grid_spec=grid_spec,
        compiler_params=_cparams(("arbitrary", "arbitrary")),
        name="combine",
    )(bounds_flat, x1, mod, slot_t, y)


def _rope_tables(n_tokens):
    t = jnp.arange(n_tokens)
    row = (t // GRID_W).astype(F32)
    col = (t % GRID_W).astype(F32)
    n_freq = MLA_ROPE // 4
    inv = ROPE_THETA ** (-jnp.arange(n_freq, dtype=F32) / n_freq)
    ang = jnp.concatenate([row[:, None] * inv, col[:, None] * inv], axis=-1)
    cos, sin = jnp.cos(ang), jnp.sin(ang)
    z = jnp.zeros((n_tokens, LANES - MLA_ROPE), F32)
    return jnp.concatenate([cos, cos, z], axis=-1), jnp.concatenate([-sin, sin, z], axis=-1)


def _pad_lanes(v):
    return jnp.concatenate([v, jnp.zeros((LANES - v.shape[0],), v.dtype)])[None, :]


def kernel(x, c, ctx, c_ctx, w_mod, b_mod, g_norm1, w_in, g_qa, w_qb, g_kva, w_kvb, g_q_mla, g_k_mla,
           g_q_na, g_k_na, rpb_na, g_out_mla, g_out_na, w_out, g_norm2, w_router, w_gate, w_up, w_down):
    B, T, D = x.shape
    C = ctx.shape[1]
    assert w_mod.shape[0] == 1, "single-layer problem"
    assert T % GRID_W == 0 and (T // GRID_W) % NA_QROWS == 0 and T // GRID_W >= 2 * NA_BAND
    assert w_in.shape[2] == 3 * NA_W + MLA_Q_RANK + MLA_KV_RANK + MLA_ROPE
    rows = T // GRID_W
    cap = EC_CAPACITY_FACTOR * T // N_EXPERTS
    half = MLA_ROPE // 2

    wi = w_in[0]
    o_q, o_kv, o_pe, o_na = 0, MLA_Q_RANK, MLA_Q_RANK + MLA_KV_RANK, MLA_Q_RANK + MLA_KV_RANK + MLA_ROPE
    w_in_p = jnp.concatenate(
        [wi[:, o_na:], wi[:, o_q:o_pe], wi[:, o_pe:o_na], wi[:, o_pe + half:o_na], wi[:, o_pe:o_pe + half],
         jnp.zeros((D, P_WIDTH - wi.shape[1] - MLA_ROPE), F32)], axis=1).astype(BF16)
    wq3 = w_qb[0].reshape(MLA_Q_RANK, MLA_HEADS, MLA_QK_DIM)
    wq_p = jnp.concatenate([wq3, wq3[..., MLA_NOPE + half:], wq3[..., MLA_NOPE:MLA_NOPE + half]], axis=-1)
    wq_p = wq_p.reshape(MLA_Q_RANK, MLA_HEADS * MLA_QK_PAD).astype(BF16)
    wkv3 = w_kvb[0].reshape(MLA_KV_RANK, MLA_HEADS, MLA_NOPE + MLA_V)
    wk_p = wkv3[..., :MLA_NOPE].reshape(MLA_KV_RANK, MLA_HEADS * MLA_NOPE).astype(BF16)
    wv_p = wkv3[..., MLA_NOPE:].reshape(MLA_KV_RANK, MLA_HEADS * MLA_V).astype(BF16)

    def rope_gains(g):
        gr = g[MLA_NOPE:]
        return g[None, :MLA_NOPE], _pad_lanes(gr), _pad_lanes(jnp.concatenate([gr[half:], gr[:half]]))

    gq0, gqa, gqb = rope_gains(g_q_mla[0])
    gk0, gka, gkb = rope_gains(g_k_mla[0])
    ct, st = _rope_tables(T)
    consts = dict(ct=ct, st=st, g_qa=g_qa, wq=wq_p, g_kva=g_kva, wk=wk_p, wv=wv_p,
                  gq0=gq0, gqa=gqa, gqb=gqb, gk0=gk0, gka=gka, gkb=gkb, g_q_na=g_q_na, g_k_na=g_k_na)
    wr = jnp.concatenate([w_router[0], jnp.zeros((D, LANES - N_EXPERTS), F32)], axis=1)
    wr_hi = wr.astype(BF16)
    wr_lo = (wr - wr_hi.astype(F32)).astype(BF16)
    bias = _na_bias_table(rpb_na[0], rows)

    cvec = jnp.concatenate([c, c_ctx[None, :], jnp.zeros((8 - B - 1, D), F32)], axis=0)
    mod = _mod_call(cvec.T, w_mod[0], b_mod, B + 1)
    mod_lat = mod[:B].reshape(B, N_MOD, D)
    mod_ctx = mod[B:].reshape(1, N_MOD, D)

    x2 = x.reshape(B * T, D)
    p_lat = _inproj_call(x2, mod_lat, g_norm1, w_in_p, T, min(1024, T))
    p_ctx = _inproj_call(ctx.reshape(B * C, D), mod_ctx, g_norm1, w_in_p, B * C, C)
    qm, km, vm, qn, kn = _prep_call(p_lat, B, T, min(512, T), consts, False)
    kmc, vmc, knc = _prep_call(p_ctx, B, C, C, consts, True)
    o_m = _mla_call(qm, km, vm, kmc, vmc, min(1024, T), min(512, T))
    o_n = _na_call(qn, kn, p_lat, knc, p_ctx, bias)
    x1, h2, aff = _merge_call(o_m.reshape(B * T, -1), o_n.reshape(B * T, -1), x2, mod_lat,
                              g_out_mla, g_out_na, w_out[0].astype(BF16), g_norm2, wr_hi, wr_lo,
                              T, min(512, T))

    slot_t, idx, gate, bounds = _route_call(aff.reshape(B, T, LANES), cap)
    y = _ffn_call(idx.reshape(-1), h2, gate, w_gate[0], w_up[0], w_down[0], B, T, cap)
    out = _combine_call(bounds, x1, mod_lat, slot_t, y, B, T, cap)
    return out.reshape(B, T, D)
```

```python
import functools

import numpy as np
import jax
import jax.numpy as jnp
from jax import lax
from jax.experimental import pallas as pl
from jax.experimental.pallas import tpu as pltpu

F32 = jnp.float32
BF16 = jnp.bfloat16

GRID_W = 64
MLA_HEADS = 8
MLA_NOPE = 128
MLA_ROPE = 64
MLA_QK_DIM = MLA_NOPE + MLA_ROPE
MLA_V = 128
MLA_Q_RANK = 512
MLA_KV_RANK = 256
NA_HEADS = 8
NA_HEAD_DIM = 128
NA_KH = 8
NA_KW = 16
N_EXPERTS = 16
EC_CAPACITY_FACTOR = 2
ROPE_THETA = 10000.0
EPS = 1e-6
N_MOD = 6
MLA_SCALE = MLA_QK_DIM ** -0.5
NA_SCALE = NA_HEAD_DIM ** -0.5
LOG2E = 1.4426950408889634

LANES = 128
MLA_QK_PAD = 2 * LANES
NA_W = NA_HEADS * NA_HEAD_DIM
P_WIDTH = 3 * NA_W + 1024
VMEM_LIMIT = 56 * 1024 * 1024
NEG = -1e30


def _f32(v):
    return jnp.asarray(v, dtype=F32)


def _mult(v, m):
    return v if isinstance(v, int) else pl.multiple_of(v, m)


def _cparams(sem, vmem=VMEM_LIMIT):
    return pltpu.CompilerParams(dimension_semantics=sem, vmem_limit_bytes=vmem)


def _mod_kernel(ct_ref, w_ref, b_ref, o_ref):
    ct = ct_ref[...]
    s = ct * (1.0 / (1.0 + jnp.exp(-ct)))
    w = w_ref[...]
    rows = []
    for m in range(o_ref.shape[0]):
        rows.append(jnp.sum(w * s[:, m:m + 1], axis=0, keepdims=True))
    o_ref[...] = jnp.concatenate(rows, axis=0) + b_ref[...]


def _mod_call(ct, w_mod, b_mod, n_rows):
    D, N = w_mod.shape
    tn = 512
    return pl.pallas_call(
        _mod_kernel,
        out_shape=jax.ShapeDtypeStruct((n_rows, N), F32),
        grid=(N // tn,),
        in_specs=[pl.BlockSpec((D, 8), lambda j: (0, 0)),
                  pl.BlockSpec((D, tn), lambda j: (0, j)),
                  pl.BlockSpec((1, tn), lambda j: (0, j))],
        out_specs=pl.BlockSpec((n_rows, tn), lambda j: (0, j)),
        compiler_params=_cparams(("arbitrary",)),
        name="mod",
    )(ct, w_mod, b_mod)


def _inproj_kernel(x_ref, mod_ref, g_ref, w_ref, o_ref, hn_ref):
    @pl.when(pl.program_id(1) == 0)
    def _():
        x = x_ref[...]
        y = x * lax.rsqrt(jnp.mean(x * x, axis=-1, keepdims=True) + EPS) * g_ref[...]
        h = y * (1.0 + mod_ref[0, 1:2, :]) + mod_ref[0, 0:1, :]
        hn_ref[...] = h.astype(BF16)

    o_ref[...] = jnp.dot(hn_ref[...], w_ref[...], preferred_element_type=F32).astype(BF16)


def _inproj_call(x2, mod, g, w, rows_per_mod, tm):
    M, D = x2.shape
    N = w.shape[1]
    tn = 1024
    per = rows_per_mod // tm
    return pl.pallas_call(
        _inproj_kernel,
        out_shape=jax.ShapeDtypeStruct((M, N), BF16),
        grid=(M // tm, N // tn),
        in_specs=[pl.BlockSpec((tm, D), lambda i, j: (i, 0)),
                  pl.BlockSpec((1, N_MOD, D), lambda i, j: (i // per, 0, 0)),
                  pl.BlockSpec((1, D), lambda i, j: (0, 0)),
                  pl.BlockSpec((D, tn), lambda i, j: (0, j))],
        out_specs=pl.BlockSpec((tm, tn), lambda i, j: (i, j)),
        scratch_shapes=[pltpu.VMEM((tm, D), BF16)],
        compiler_params=_cparams(("arbitrary", "arbitrary")),
        name="inproj",
    )(x2, mod, g, w)


def _rms(x, g):
    return x * lax.rsqrt(jnp.mean(x * x, axis=-1, keepdims=True) + EPS) * g


def _prep_kernel(*refs, is_ctx):
    if is_ctx:
        (pk_ref, pm_ref, gkva_ref, wk_ref, wv_ref, gk0_ref, gka_ref, gkn_ref,
         km_ref, vm_ref, kn_ref) = refs
    else:
        (pq_ref, pk_ref, pm_ref, ct_ref, st_ref, gqa_ref, wq_ref, gkva_ref, wk_ref, wv_ref,
         gq0_ref, gqa2_ref, gqb2_ref, gk0_ref, gka_ref, gkb_ref, gqn_ref, gkn_ref,
         qm_ref, km_ref, vm_ref, qn_ref, kn_ref) = refs
    tm = pm_ref.shape[0]
    lane = lax.broadcasted_iota(jnp.int32, (1, LANES), 1)
    rope_mask = (lane < MLA_ROPE).astype(F32)
    pm = pm_ref[...].astype(F32)
    inv_qk = 1.0 / MLA_QK_DIM

    ckvn = _rms(pm[:, MLA_Q_RANK:MLA_Q_RANK + MLA_KV_RANK], gkva_ref[...]).astype(BF16)
    kn = jnp.dot(ckvn, wk_ref[...], preferred_element_type=F32)
    vv = jnp.dot(ckvn, wv_ref[...], preferred_element_type=F32)
    kpe = pm[:, MLA_Q_RANK + MLA_KV_RANK:MLA_Q_RANK + MLA_KV_RANK + LANES]
    ss_pe = jnp.sum(kpe * kpe * rope_mask, axis=-1, keepdims=True)
    if is_ctx:
        k_rope = kpe * gka_ref[...]
    else:
        ct = ct_ref[...]
        st = st_ref[...]
        k_rope = kpe * (ct * gka_ref[...]) + pltpu.roll(kpe, MLA_ROPE, 1) * (st * gkb_ref[...])
    for h in range(MLA_HEADS):
        k0 = kn[:, h * LANES:(h + 1) * LANES]
        r = lax.rsqrt((jnp.sum(k0 * k0, axis=-1, keepdims=True) + ss_pe) * inv_qk + EPS)
        km_ref[0, h] = jnp.concatenate([k0 * r * gk0_ref[...], k_rope * r], axis=-1).astype(BF16)
        vm_ref[0, h] = jnp.concatenate([vv[:, h * LANES:(h + 1) * LANES], jnp.ones((tm, LANES), F32)],
                                       axis=-1).astype(BF16)

    pk = pk_ref[...].astype(F32)
    for h in range(NA_HEADS):
        kn_ref[0, h] = _rms(pk[:, h * LANES:(h + 1) * LANES], gkn_ref[...]).astype(BF16)

    if is_ctx:
        return

    cqn = _rms(pm[:, :MLA_Q_RANK], gqa_ref[...]).astype(BF16)
    q = jnp.dot(cqn, wq_ref[...], preferred_element_type=F32)
    qa = ct * gqa2_ref[...]
    qb = st * gqb2_ref[...]
    for h in range(MLA_HEADS):
        t0 = q[:, h * MLA_QK_PAD:h * MLA_QK_PAD + LANES]
        t1 = q[:, h * MLA_QK_PAD + LANES:(h + 1) * MLA_QK_PAD]
        ss = jnp.sum(t0 * t0, axis=-1, keepdims=True) + jnp.sum(t1 * t1 * rope_mask, axis=-1, keepdims=True)
        r = lax.rsqrt(ss * inv_qk + EPS) * (MLA_SCALE * LOG2E)
        o1 = t1 * qa + pltpu.roll(t1, MLA_ROPE, 1) * qb
        qm_ref[0, h] = jnp.concatenate([t0 * r * gq0_ref[...], o1 * r], axis=-1).astype(BF16)

    pq = pq_ref[...].astype(F32)
    for h in range(NA_HEADS):
        qn_ref[0, h] = (_rms(pq[:, h * LANES:(h + 1) * LANES], gqn_ref[...]) * (NA_SCALE * LOG2E)).astype(BF16)


def _prep_call(p, nb, tok, tm, consts, is_ctx):
    per = tok // tm
    H = MLA_HEADS
    row = lambda i: (i, 0)
    full = lambda i: (0, 0)
    hm = lambda i: (i // per, 0, i % per, 0)
    p_spec = lambda c: pl.BlockSpec((tm, 1024), lambda i, c=c: (i, c))
    vec = lambda n: pl.BlockSpec((1, n), full)
    km_s = jax.ShapeDtypeStruct((nb, H, tok, MLA_QK_PAD), BF16)
    h128_s = jax.ShapeDtypeStruct((nb, H, tok, LANES), BF16)
    km_o = pl.BlockSpec((1, H, tm, MLA_QK_PAD), hm)
    h128_o = pl.BlockSpec((1, H, tm, LANES), hm)
    c = consts
    if is_ctx:
        ins = [p, p, c["g_kva"], c["wk"], c["wv"], c["gk0"], c["gka"], c["g_k_na"]]
        in_specs = [p_spec(1), p_spec(3), vec(MLA_KV_RANK),
                    pl.BlockSpec(c["wk"].shape, full), pl.BlockSpec(c["wv"].shape, full),
                    vec(LANES), vec(LANES), vec(LANES)]
        out_shape = [km_s, km_s, h128_s]
        out_specs = [km_o, km_o, h128_o]
    else:
        ins = [p, p, p, c["ct"], c["st"], c["g_qa"], c["wq"], c["g_kva"], c["wk"], c["wv"],
               c["gq0"], c["gqa"], c["gqb"], c["gk0"], c["gka"], c["gkb"], c["g_q_na"], c["g_k_na"]]
        tab = pl.BlockSpec((tm, LANES), lambda i: (i % per, 0))
        in_specs = [p_spec(0), p_spec(1), p_spec(3), tab, tab, vec(MLA_Q_RANK),
                    pl.BlockSpec(c["wq"].shape, full), vec(MLA_KV_RANK),
                    pl.BlockSpec(c["wk"].shape, full), pl.BlockSpec(c["wv"].shape, full),
                    vec(LANES), vec(LANES), vec(LANES), vec(LANES), vec(LANES), vec(LANES),
                    vec(LANES), vec(LANES)]
        out_shape = [km_s, km_s, km_s, h128_s, h128_s]
        out_specs = [km_o, km_o, km_o, h128_o, h128_o]
    return pl.pallas_call(
        functools.partial(_prep_kernel, is_ctx=is_ctx),
        out_shape=out_shape,
        grid=(nb * per,),
        in_specs=in_specs,
        out_specs=out_specs,
        compiler_params=_cparams(("arbitrary",)),
        name="prep_ctx" if is_ctx else "prep_lat",
    )(*ins)


MLA_SUB = 2


def _mla_kernel(q_ref, k_ref, v_ref, kc_ref, vc_ref, o_ref, m_ref, acc_ref, sa_ref, sb_ref, sc_ref, *, tk):
    ts = q_ref.shape[2] // MLA_SUB
    n = k_ref.shape[2] // tk
    m_ref[...] = jnp.full(m_ref.shape, NEG, F32)
    acc_ref[...] = jnp.zeros(acc_ref.shape, F32)
    dn = (((1,), (1,)), ((), ()))

    def scores(c):
        return lax.dot_general(q_ref[0, 0], k_ref[0, 0, pl.ds(_mult(c * tk, tk), tk), :], dn,
                               preferred_element_type=F32)

    def values(c):
        return v_ref[0, 0, pl.ds(_mult(c * tk, tk), tk), :]

    def softmax_pv(s_ref, v):
        for u in range(MLA_SUB):
            rows = slice(u * ts, (u + 1) * ts)
            m_old = m_ref[rows, :]
            m_new = jnp.maximum(m_old, jnp.max(s_ref[rows, :], axis=-1, keepdims=True))
            p = jnp.exp2(s_ref[rows, :] - m_new).astype(BF16)
            acc_ref[rows, :] = (jnp.exp2(m_old - m_new) * acc_ref[rows, :]
                                + jnp.dot(p, v, preferred_element_type=F32))
            m_ref[rows, :] = m_new

    sa_ref[...] = scores(0)

    def body(i, carry):
        c = 2 * i
        sb_ref[...] = scores(c + 1)
        softmax_pv(sa_ref, values(c))
        sa_ref[...] = scores(c + 2)
        softmax_pv(sb_ref, values(c + 1))
        return carry

    lax.fori_loop(0, n // 2 - 1, body, 0)
    sb_ref[...] = scores(n - 1)
    softmax_pv(sa_ref, values(n - 2))
    sc_ref[...] = lax.dot_general(q_ref[0, 0], kc_ref[0, 0], dn, preferred_element_type=F32)
    softmax_pv(sb_ref, values(n - 1))
    softmax_pv(sc_ref, vc_ref[0, 0])
    o_ref[0] = (acc_ref[:, :MLA_V] / acc_ref[:, MLA_V:]).astype(o_ref.dtype)


def _mla_call(qm, km, vm, kmc, vmc, tq, tk):
    B, H, T, _ = qm.shape
    C = kmc.shape[2]
    VW = vm.shape[3]
    return pl.pallas_call(
        functools.partial(_mla_kernel, tk=tk),
        out_shape=jax.ShapeDtypeStruct((B, T, H * MLA_V), BF16),
        grid=(B, H, T // tq),
        in_specs=[pl.BlockSpec((1, 1, tq, MLA_QK_PAD), lambda b, h, i: (b, h, i, 0)),
                  pl.BlockSpec((1, 1, T, MLA_QK_PAD), lambda b, h, i: (b, h, 0, 0)),
                  pl.BlockSpec((1, 1, T, VW), lambda b, h, i: (b, h, 0, 0)),
                  pl.BlockSpec((1, 1, C, MLA_QK_PAD), lambda b, h, i: (b, h, 0, 0)),
                  pl.BlockSpec((1, 1, C, VW), lambda b, h, i: (b, h, 0, 0))],
        out_specs=pl.BlockSpec((1, tq, MLA_V), lambda b, h, i: (b, i, h)),
        scratch_shapes=[pltpu.VMEM((tq, 1), F32), pltpu.VMEM((tq, VW), F32),
                        pltpu.VMEM((tq, tk), F32), pltpu.VMEM((tq, tk), F32), pltpu.VMEM((tq, C), F32)],
        compiler_params=_cparams(("arbitrary", "arbitrary", "arbitrary")),
        name="mla_attn",
    )(qm, km, vm, kmc, vmc)


NA_QROWS = 2
NA_TILE = NA_QROWS * GRID_W
NA_BAND = 5
NA_CFGS = 5


def _na_band_start(i, nblk):
    return jnp.clip(i - 2, 0, nblk - NA_BAND)


def _na_kernel(q_ref, k0, k1, k2, k3, k4, v0, v1, v2, v3, v4, kc_ref, vc_ref, bias_ref, o_ref,
               sl_ref, sx_ref):
    k_refs = (k0, k1, k2, k3, k4)
    v_refs = (v0, v1, v2, v3, v4)
    dn = (((1,), (1,)), ((), ()))

    def scores(h, slot):
        q = q_ref[0, h]
        kcat = jnp.concatenate([r[0, h] for r in k_refs], axis=0)
        sl_ref[slot] = lax.dot_general(q, kcat, dn, preferred_element_type=F32) + bias_ref[0, h]
        sx_ref[slot] = lax.dot_general(q, kc_ref[0, h], dn, preferred_element_type=F32)

    def softmax_pv(h, slot):
        s_loc = sl_ref[slot]
        s_ctx = sx_ref[slot]
        m = jnp.maximum(jnp.max(s_loc, axis=-1, keepdims=True), jnp.max(s_ctx, axis=-1, keepdims=True))
        p_loc = jnp.exp2(s_loc - m)
        p_ctx = jnp.exp2(s_ctx - m)
        l = jnp.sum(p_loc, axis=-1, keepdims=True) + jnp.sum(p_ctx, axis=-1, keepdims=True)
        cols = slice(h * LANES, (h + 1) * LANES)
        vcat = jnp.concatenate([r[:, cols] for r in v_refs], axis=0)
        o = (jnp.dot(p_loc.astype(BF16), vcat, preferred_element_type=F32)
             + jnp.dot(p_ctx.astype(BF16), vc_ref[:, cols], preferred_element_type=F32))
        o_ref[0, :, cols] = (o / l).astype(o_ref.dtype)

    scores(0, 0)
    for h in range(NA_HEADS):
        if h + 1 < NA_HEADS:
            scores(h + 1, (h + 1) % 2)
        softmax_pv(h, h % 2)


def _na_call(qn, kn, p_lat, knc, p_ctx, bias):
    B, H, T, _ = qn.shape
    C = knc.shape[2]
    nblk = T // NA_TILE
    tiles_per_b = T // NA_TILE

    def cfg(i):
        return jnp.minimum(i, 2) + jnp.maximum(i - (nblk - 3), 0)

    k_specs = [pl.BlockSpec((1, H, NA_TILE, LANES),
                            lambda b, i, u=u: (b, 0, _na_band_start(i, nblk) + u, 0)) for u in range(NA_BAND)]
    v_specs = [pl.BlockSpec((NA_TILE, NA_W),
                            lambda b, i, u=u: (b * tiles_per_b + _na_band_start(i, nblk) + u, 2))
               for u in range(NA_BAND)]
    return pl.pallas_call(
        _na_kernel,
        out_shape=jax.ShapeDtypeStruct((B, T, NA_W), BF16),
        grid=(B, nblk),
        in_specs=[pl.BlockSpec((1, H, NA_TILE, LANES), lambda b, i: (b, 0, i, 0))] + k_specs + v_specs + [
            pl.BlockSpec((1, H, C, LANES), lambda b, i: (b, 0, 0, 0)),
            pl.BlockSpec((C, NA_W), lambda b, i: (b, 2)),
            pl.BlockSpec((1, H, NA_TILE, NA_BAND * NA_TILE), lambda b, i: (cfg(i), 0, 0, 0))],
        out_specs=pl.BlockSpec((1, NA_TILE, NA_W), lambda b, i: (b, i, 0)),
        scratch_shapes=[pltpu.VMEM((2, NA_TILE, NA_BAND * NA_TILE), F32), pltpu.VMEM((2, NA_TILE, C), F32)],
        compiler_params=_cparams(("arbitrary", "arbitrary")),
        name="na_attn",
    )(qn, *([kn] * NA_BAND), *([p_lat] * NA_BAND), knc, p_ctx, bias)


def _na_bias_table(rpb, rows):
    nblk = rows // NA_QROWS
    reps = [0, 1, 2, nblk - 2, nblk - 1]
    q_r = np.arange(NA_TILE) // GRID_W
    q_c = np.arange(NA_TILE) % GRID_W
    k_r = np.arange(NA_BAND * NA_TILE) // GRID_W
    k_c = np.arange(NA_BAND * NA_TILE) % GRID_W
    padw = GRID_W - NA_KW
    rp = jnp.pad(rpb, ((0, 0), (0, 0), (padw, padw)))
    toep = jnp.stack([rp[:, :, GRID_W - 1 - qc:2 * GRID_W - 1 - qc] for qc in range(GRID_W)], axis=2)
    tables, ok = [], []
    for i in reps:
        j0 = int(np.clip(i - 2, 0, nblk - NA_BAND))
        r = NA_QROWS * i + q_r
        rs = np.clip(r - NA_KH // 2, 0, rows - NA_KH)
        cs = np.clip(q_c - NA_KW // 2, 0, GRID_W - NA_KW)
        kr = NA_QROWS * j0 + k_r
        ok.append((kr[None, :] >= rs[:, None]) & (kr[None, :] < rs[:, None] + NA_KH)
                  & (k_c[None, :] >= cs[:, None]) & (k_c[None, :] < cs[:, None] + NA_KW))
        per_qr = []
        for qr in range(NA_QROWS):
            pieces = []
            for kb in range(NA_BAND * NA_QROWS):
                ro = int(np.clip(NA_QROWS * (j0 - i) + kb - qr + NA_KH - 1, 0, 2 * NA_KH - 2))
                pieces.append(toep[:, ro])
            per_qr.append(jnp.concatenate(pieces, axis=-1))
        tables.append(jnp.concatenate(per_qr, axis=1))
    return jnp.where(np.stack(ok)[:, None], jnp.stack(tables) * LOG2E, NEG).astype(F32)


def _merge_kernel(om_ref, on_ref, x_ref, mod_ref, gom_ref, gon_ref, wout_ref, g2_ref, wrh_ref, wrl_ref,
                  x1_ref, h2_ref, aff_ref):
    ts = x_ref.shape[0] // MERGE_SUB
    for s in range(MERGE_SUB):
        rows = slice(s * ts, (s + 1) * ts)
        a = _rms(om_ref[rows, :].astype(F32), gom_ref[...]).astype(BF16)
        b = _rms(on_ref[rows, :].astype(F32), gon_ref[...]).astype(BF16)
        y = jnp.dot(jnp.concatenate([a, b], axis=-1), wout_ref[...], preferred_element_type=F32)
        x1 = x_ref[rows, :] + mod_ref[0, 2:3, :] * y
        x1_ref[rows, :] = x1
        h2 = _rms(x1, g2_ref[...]) * (1.0 + mod_ref[0, 4:5, :]) + mod_ref[0, 3:4, :]
        h2_ref[rows, :] = h2
        hi = h2.astype(BF16)
        lo = (h2 - hi.astype(F32)).astype(BF16)
        logits = (jnp.dot(hi, wrh_ref[...], preferred_element_type=F32)
                  + jnp.dot(lo, wrh_ref[...], preferred_element_type=F32)
                  + jnp.dot(hi, wrl_ref[...], preferred_element_type=F32))
        lane = lax.broadcasted_iota(jnp.int32, logits.shape, 1)
        logits = jnp.where(lane < N_EXPERTS, logits, NEG)
        e = jnp.exp(logits - jnp.max(logits, axis=-1, keepdims=True))
        aff_ref[rows, :] = e / jnp.sum(e, axis=-1, keepdims=True)


MERGE_SUB = 2


def _merge_call(om, on, x2, mod, gom, gon, wout, g2, wrh, wrl, rows_per_mod, tm):
    M, D = x2.shape
    per = rows_per_mod // tm
    row = lambda i: (i, 0)
    full = lambda i: (0, 0)
    return pl.pallas_call(
        _merge_kernel,
        out_shape=[jax.ShapeDtypeStruct((M, D), F32), jax.ShapeDtypeStruct((M, D), F32),
                   jax.ShapeDtypeStruct((M, LANES), F32)],
        grid=(M // tm,),
        in_specs=[pl.BlockSpec((tm, om.shape[1]), row), pl.BlockSpec((tm, on.shape[1]), row),
                  pl.BlockSpec((tm, D), row),
                  pl.BlockSpec((1, N_MOD, D), lambda i: (i // per, 0, 0)),
                  pl.BlockSpec((1, om.shape[1]), full), pl.BlockSpec((1, on.shape[1]), full),
                  pl.BlockSpec(wout.shape, full), pl.BlockSpec((1, D), full),
                  pl.BlockSpec(wrh.shape, full), pl.BlockSpec(wrl.shape, full)],
        out_specs=[pl.BlockSpec((tm, D), row), pl.BlockSpec((tm, D), row), pl.BlockSpec((tm, LANES), row)],
        compiler_params=_cparams(("arbitrary",)),
        name="merge_router",
    )(om, on, x2, mod, gom, gon, wout, g2, wrh, wrl)


RT_CHUNK = 256


def _select_kernel(aff_ref, slot_ref, afft_ref, bounds_ref, *, cap):
    T = aff_ref.shape[1]
    E = N_EXPERTS
    n_tiles = T // LANES
    capf = float(cap)

    def tr_body(c, carry):
        off = _mult(c * RT_CHUNK, RT_CHUNK)
        afft_ref[0, :, pl.ds(off, RT_CHUNK)] = aff_ref[0, pl.ds(off, RT_CHUNK), :].T[:E]
        return carry

    lax.fori_loop(0, T // RT_CHUNK, tr_body, 0)

    def as_float(bits):
        return lax.bitcast_convert_type(bits, F32)

    def count_ge(v):
        return jnp.sum((afft_ref[0] >= v).astype(F32), axis=1, keepdims=True)

    def bs_body(i, prefix):
        cand = prefix | jnp.left_shift(jnp.int32(1), 30 - i)
        return jnp.where(count_ge(as_float(cand)) >= capf, cand, prefix)

    thr_bits = lax.fori_loop(0, 31, bs_body, jnp.zeros((E, 1), jnp.int32))
    thr = as_float(thr_bits)
    above = as_float(jnp.maximum(thr_bits + 1, jnp.int32(0x00800000)))
    need = capf - count_ge(above)

    ri = lax.broadcasted_iota(jnp.int32, (LANES, LANES), 0)
    ci = lax.broadcasted_iota(jnp.int32, (LANES, LANES), 1)
    upper = (ri <= ci).astype(BF16)
    lane = lax.broadcasted_iota(jnp.int32, (E, LANES), 1)
    run_eq = jnp.zeros((E, 1), F32)
    run_sel = jnp.zeros((E, 1), F32)
    bounds = jnp.zeros((E, LANES), F32)
    for k in range(n_tiles):
        a = afft_ref[0, :, k * LANES:(k + 1) * LANES]
        gt = a >= above
        eq = (a >= thr) & (a < above)
        eqf = eq.astype(F32)
        incl_eq = jnp.dot(eqf.astype(BF16), upper, preferred_element_type=F32)
        sel = gt | (eq & ((incl_eq - eqf + run_eq) < need))
        self_ = sel.astype(F32)
        incl_sel = jnp.dot(self_.astype(BF16), upper, preferred_element_type=F32)
        slot_ref[0, :, k * LANES:(k + 1) * LANES] = jnp.where(sel, incl_sel - self_ + run_sel, -1.0)
        bounds = jnp.where(lane == k, run_sel, bounds)
        run_eq = run_eq + incl_eq[:, LANES - 1:LANES]
        run_sel = run_sel + incl_sel[:, LANES - 1:LANES]
    bounds_ref[0] = jnp.where(lane >= n_tiles, run_sel, bounds).astype(jnp.int32)


def _compact_kernel(bnd_ref, slot_ref, afft_ref, idx_ref, gate_ref, acc_i_ref, acc_g_ref, *, cap):
    b = pl.program_id(0)
    T = slot_ref.shape[2]
    n_tiles = T // LANES
    sub = lax.broadcasted_iota(jnp.int32, (LANES, LANES), 0).astype(F32)
    lane_f = lax.broadcasted_iota(jnp.int32, (1, LANES), 1).astype(F32)

    for e in range(N_EXPERTS):
        acc_i_ref[...] = jnp.zeros(acc_i_ref.shape, F32)
        acc_g_ref[...] = jnp.zeros(acc_g_ref.shape, F32)
        row = (b * N_EXPERTS + e) * LANES

        def tile_body(k, carry, e=e, row=row):
            lo = bnd_ref[row + k]
            hi = bnd_ref[row + k + 1]
            off = _mult(k * LANES, LANES)
            srow = slot_ref[0, e:e + 1, pl.ds(off, LANES)]
            arow = afft_ref[0, e:e + 1, pl.ds(off, LANES)]
            trow = lane_f + _f32(off)

            def block_body(sb, c2):
                s0 = _mult(sb * LANES, LANES)
                hit = srow == (sub + _f32(s0))
                acc_i_ref[pl.ds(s0, LANES), :] += jnp.where(hit, trow, 0.0)
                acc_g_ref[pl.ds(s0, LANES), :] += jnp.where(hit, arow, 0.0)
                return c2

            lax.fori_loop(lo // LANES, (hi + LANES - 1) // LANES, block_body, 0)
            return carry

        lax.fori_loop(0, n_tiles, tile_body, 0)
        idx_ref[0, e] = jnp.sum(acc_i_ref[...], axis=-1, keepdims=True).astype(jnp.int32)
        gate_ref[0, e] = jnp.sum(acc_g_ref[...], axis=-1, keepdims=True)


def _route_call(aff, cap):
    B, T, _ = aff.shape
    E = N_EXPERTS
    assert T // LANES < LANES
    bet = lambda b: (b, 0, 0)
    slot_t, aff_t, bounds = pl.pallas_call(
        functools.partial(_select_kernel, cap=cap),
        out_shape=[jax.ShapeDtypeStruct((B, E, T), F32), jax.ShapeDtypeStruct((B, E, T), F32),
                   jax.ShapeDtypeStruct((B, E, LANES), jnp.int32)],
        grid=(B,),
        in_specs=[pl.BlockSpec((1, T, LANES), bet)],
        out_specs=[pl.BlockSpec((1, E, T), bet), pl.BlockSpec((1, E, T), bet), pl.BlockSpec((1, E, LANES), bet)],
        compiler_params=_cparams(("arbitrary",)),
        name="route_select",
    )(aff)
    bounds_flat = bounds.reshape(-1)
    grid_spec = pltpu.PrefetchScalarGridSpec(
        num_scalar_prefetch=1,
        grid=(B,),
        in_specs=[pl.BlockSpec((1, E, T), lambda b, bnd: (b, 0, 0)),
                  pl.BlockSpec((1, E, T), lambda b, bnd: (b, 0, 0))],
        out_specs=[pl.BlockSpec((1, E, cap, 1), lambda b, bnd: (b, 0, 0, 0)),
                   pl.BlockSpec((1, E, cap, 1), lambda b, bnd: (b, 0, 0, 0))],
        scratch_shapes=[pltpu.VMEM((cap, LANES), F32), pltpu.VMEM((cap, LANES), F32)],
    )
    idx, gate = pl.pallas_call(
        functools.partial(_compact_kernel, cap=cap),
        out_shape=[jax.ShapeDtypeStruct((B, E, cap, 1), jnp.int32), jax.ShapeDtypeStruct((B, E, cap, 1), F32)],
        grid_spec=grid_spec,
        compiler_params=_cparams(("arbitrary",)),
        name="route_compact",
    )(bounds_flat, slot_t, aff_t)
    return slot_t, idx, gate, bounds_flat


def _ffn_kernel(idx_ref, h2_hbm, gate_ref, wg_ref, wu_ref, wd_ref, y_ref, xs_ref, xb_ref, hmid_ref, sem,
                *, cap, tokens, n_batch, n_steps):
    e = pl.program_id(0)
    b = pl.program_id(1)
    f = pl.program_id(2)
    n_exp = pl.num_programs(0)
    nf = pl.num_programs(2)
    rows_per_step = cap // n_steps

    def row_copy(bb, ee, s):
        t = idx_ref[(bb * n_exp + ee) * cap + s]
        return pltpu.make_async_copy(h2_hbm.at[pl.ds(bb * tokens + t, 1)], xs_ref.at[pl.ds(s, 1)], sem)

    def wait_rows():
        pltpu.make_async_copy(h2_hbm.at[pl.ds(0, cap)], xs_ref, sem).wait()

    @pl.when((e == 0) & (b == 0) & (f == 0))
    def _():
        def issue(s, carry):
            row_copy(b, e, s).start()
            return carry

        lax.fori_loop(0, cap, issue, 0, unroll=8)

    @pl.when(f == 0)
    def _():
        wait_rows()
        xb_ref[...] = xs_ref[...].astype(BF16)

    wrap_b = b + 1 == n_batch
    nb = jnp.where(wrap_b, 0, b + 1)
    ne = jnp.where(wrap_b, jnp.where(e + 1 == n_exp, 0, e + 1), e)
    for k in range(rows_per_step):
        row_copy(nb, ne, f * rows_per_step + k).start()

    xb = xb_ref[...]
    a = jnp.dot(xb, wg_ref[0].astype(BF16), preferred_element_type=F32)
    u = jnp.dot(xb, wu_ref[0].astype(BF16), preferred_element_type=F32)
    tf = wg_ref.shape[2]
    hmid_ref[:, pl.ds(_mult(f * tf, tf), tf)] = (a * (1.0 / (1.0 + jnp.exp(-a))) * u).astype(BF16)

    @pl.when(f == nf - 1)
    def _():
        hmid = hmid_ref[...]
        gate = gate_ref[0, 0]
        for n in range(y_ref.shape[3] // FFN_TN):
            cols = slice(n * FFN_TN, (n + 1) * FFN_TN)
            part = jnp.dot(hmid, wd_ref[0, :, cols].astype(BF16), preferred_element_type=F32)
            y_ref[0, 0, :, cols] = (part * gate).astype(y_ref.dtype)

    @pl.when((e == n_exp - 1) & (b == n_batch - 1) & (f == nf - 1))
    def _():
        wait_rows()


FFN_TF = 256
FFN_TN = 512


def _ffn_call(idx_flat, h2, gate, wg, wu, wd, B, T, cap):
    E, D, F = wg.shape
    tf = FFN_TF
    grid_spec = pltpu.PrefetchScalarGridSpec(
        num_scalar_prefetch=1,
        grid=(E, B, F // tf),
        in_specs=[pl.BlockSpec(memory_space=pl.ANY),
                  pl.BlockSpec((1, 1, cap, 1), lambda e, b, f, idx: (b, e, 0, 0)),
                  pl.BlockSpec((1, D, tf), lambda e, b, f, idx: (e, 0, f)),
                  pl.BlockSpec((1, D, tf), lambda e, b, f, idx: (e, 0, f)),
                  pl.BlockSpec((1, F, D), lambda e, b, f, idx: (e, 0, 0))],
        out_specs=pl.BlockSpec((1, 1, cap, D), lambda e, b, f, idx: (b, e, 0, 0)),
        scratch_shapes=[pltpu.VMEM((cap, D), F32), pltpu.VMEM((cap, D), BF16), pltpu.VMEM((cap, F), BF16),
                        pltpu.SemaphoreType.DMA(())],
    )
    return pl.pallas_call(
        functools.partial(_ffn_kernel, cap=cap, tokens=T, n_batch=B, n_steps=F // tf),
        out_shape=jax.ShapeDtypeStruct((B, E, cap, D), BF16),
        grid_spec=grid_spec,
        compiler_params=_cparams(("arbitrary", "arbitrary", "arbitrary")),
        name="expert_ffn",
    )(idx_flat, h2, gate, wg, wu, wd)


CB_WIN = 64
CB_ALIGN = 16


def _combine_kernel(bnd_ref, x1_ref, mod_ref, slot_ref, y_hbm, o_ref, buf_ref, acc_ref, sem,
                    *, cap, nch, n_batch):
    b = pl.program_id(0)
    j = pl.program_id(1)
    E = N_EXPERTS
    step = b * nch + j
    cur = step % 2
    sub = lax.broadcasted_iota(jnp.int32, (CB_WIN, RT_CHUNK), 0).astype(F32)

    def tile_windows(bb, jj):
        per_tile = RT_CHUNK // LANES
        aligned, n_pass = [], jnp.int32(0)
        for e in range(E):
            at = (bb * E + e) * LANES + jj * per_tile
            base = bnd_ref[at]
            cnt = bnd_ref[at + per_tile] - base
            al = (base // CB_ALIGN) * CB_ALIGN
            aligned.append(al)
            n_pass = jnp.maximum(n_pass, (base - al + cnt + CB_WIN - 1) // CB_WIN)
        return aligned, n_pass

    def window(al, c):
        lo = al + c * CB_WIN
        return lo, _mult(jnp.minimum(lo, cap - CB_WIN), CB_ALIGN)

    def start_fetch(bb, aligned, c, slot):
        for e in range(E):
            _, st = window(aligned[e], c)
            pltpu.make_async_copy(y_hbm.at[bb, e, pl.ds(st, CB_WIN)],
                                  buf_ref.at[slot, pl.ds(e * CB_WIN, CB_WIN)], sem.at[slot]).start()

    def wait_fetch(slot):
        pltpu.make_async_copy(buf_ref.at[slot], buf_ref.at[slot], sem.at[slot]).wait()

    def scatter_add(aligned, c, slot):
        pieces = []
        for e in range(E):
            lo, st = window(aligned[e], c)
            srow = slot_ref[0, e:e + 1, :]
            pieces.append(((srow == sub + _f32(st)) & (srow >= _f32(lo))).astype(BF16))
        sel_t = jnp.concatenate(pieces, axis=0)
        return lax.dot_general(sel_t, buf_ref[slot], (((0,), (0,)), ((), ())), preferred_element_type=F32)

    aligned, n_pass = tile_windows(b, j)

    @pl.when(step == 0)
    def _():
        start_fetch(b, aligned, 0, cur)

    @pl.when(step + 1 < n_batch * nch)
    def _():
        wrap = j + 1 == nch
        nb = jnp.where(wrap, b + 1, b)
        nj = jnp.where(wrap, 0, j + 1)
        start_fetch(nb, tile_windows(nb, nj)[0], 0, 1 - cur)

    wait_fetch(cur)
    acc_ref[...] = scatter_add(aligned, 0, cur)

    def extra_pass(c, carry):
        start_fetch(b, aligned, c, cur)
        wait_fetch(cur)
        acc_ref[...] += scatter_add(aligned, c, cur)
        return carry

    lax.fori_loop(1, n_pass, extra_pass, 0)
    o_ref[...] = x1_ref[...] + mod_ref[0, 5:6, :] * acc_ref[...]


def _combine_call(bounds_flat, x1, mod, slot_t, y, B, T, cap):
    D = x1.shape[1]
    nch = T // RT_CHUNK
    E = N_EXPERTS
    grid_spec = pltpu.PrefetchScalarGridSpec(
        num_scalar_prefetch=1,
        grid=(B, nch),
        in_specs=[pl.BlockSpec((RT_CHUNK, D), lambda b, j, bnd: (b * nch + j, 0)),
                  pl.BlockSpec((1, N_MOD, D), lambda b, j, bnd: (b, 0, 0)),
                  pl.BlockSpec((1, E, RT_CHUNK), lambda b, j, bnd: (b, 0, j)),
                  pl.BlockSpec(memory_space=pl.ANY)],
        out_specs=pl.BlockSpec((RT_CHUNK, D), lambda b, j, bnd: (b * nch + j, 0)),
        scratch_shapes=[pltpu.VMEM((2, E * CB_WIN, D), BF16), pltpu.VMEM((RT_CHUNK, D), F32),
                        pltpu.SemaphoreType.DMA((2,))],
    )
    return pl.pallas_call(
        functools.partial(_combine_kernel, cap=cap, nch=nch, n_batch=B),
        out_shape=jax.ShapeDtypeStruct(x1.shape, F32),
        grid_spec=grid_spec,
        compiler_params=_cparams(("arbitrary", "arbitrary")),
        name="combine",
    )(bounds_flat, x1, mod, slot_t, y)


def _rope_tables(n_tokens):
    t = jnp.arange(n_tokens)
    row = (t // GRID_W).astype(F32)
    col = (t % GRID_W).astype(F32)
    n_freq = MLA_ROPE // 4
    inv = ROPE_THETA ** (-jnp.arange(n_freq, dtype=F32) / n_freq)
    ang = jnp.concatenate([row[:, None] * inv, col[:, None] * inv], axis=-1)
    cos, sin = jnp.cos(ang), jnp.sin(ang)
    z = jnp.zeros((n_tokens, LANES - MLA_ROPE), F32)
    return jnp.concatenate([cos, cos, z], axis=-1), jnp.concatenate([-sin, sin, z], axis=-1)


def _pad_lanes(v):
    return jnp.concatenate([v, jnp.zeros((LANES - v.shape[0],), v.dtype)])[None, :]


def kernel(x, c, ctx, c_ctx, w_mod, b_mod, g_norm1, w_in, g_qa, w_qb, g_kva, w_kvb, g_q_mla, g_k_mla,
           g_q_na, g_k_na, rpb_na, g_out_mla, g_out_na, w_out, g_norm2, w_router, w_gate, w_up, w_down):
    B, T, D = x.shape
    C = ctx.shape[1]
    assert w_mod.shape[0] == 1, "single-layer problem"
    assert T % GRID_W == 0 and (T // GRID_W) % NA_QROWS == 0 and T // GRID_W >= 2 * NA_BAND
    assert w_in.shape[2] == 3 * NA_W + MLA_Q_RANK + MLA_KV_RANK + MLA_ROPE
    rows = T // GRID_W
    cap = EC_CAPACITY_FACTOR * T // N_EXPERTS
    half = MLA_ROPE // 2

    wi = w_in[0]
    o_q, o_kv, o_pe, o_na = 0, MLA_Q_RANK, MLA_Q_RANK + MLA_KV_RANK, MLA_Q_RANK + MLA_KV_RANK + MLA_ROPE
    w_in_p = jnp.concatenate(
        [wi[:, o_na:], wi[:, o_q:o_pe], wi[:, o_pe:o_na], wi[:, o_pe + half:o_na], wi[:, o_pe:o_pe + half],
         jnp.zeros((D, P_WIDTH - wi.shape[1] - MLA_ROPE), F32)], axis=1).astype(BF16)
    wq3 = w_qb[0].reshape(MLA_Q_RANK, MLA_HEADS, MLA_QK_DIM)
    wq_p = jnp.concatenate([wq3, wq3[..., MLA_NOPE + half:], wq3[..., MLA_NOPE:MLA_NOPE + half]], axis=-1)
    wq_p = wq_p.reshape(MLA_Q_RANK, MLA_HEADS * MLA_QK_PAD).astype(BF16)
    wkv3 = w_kvb[0].reshape(MLA_KV_RANK, MLA_HEADS, MLA_NOPE + MLA_V)
    wk_p = wkv3[..., :MLA_NOPE].reshape(MLA_KV_RANK, MLA_HEADS * MLA_NOPE).astype(BF16)
    wv_p = wkv3[..., MLA_NOPE:].reshape(MLA_KV_RANK, MLA_HEADS * MLA_V).astype(BF16)

    def rope_gains(g):
        gr = g[MLA_NOPE:]
        return g[None, :MLA_NOPE], _pad_lanes(gr), _pad_lanes(jnp.concatenate([gr[half:], gr[:half]]))

    gq0, gqa, gqb = rope_gains(g_q_mla[0])
    gk0, gka, gkb = rope_gains(g_k_mla[0])
    ct, st = _rope_tables(T)
    consts = dict(ct=ct, st=st, g_qa=g_qa, wq=wq_p, g_kva=g_kva, wk=wk_p, wv=wv_p,
                  gq0=gq0, gqa=gqa, gqb=gqb, gk0=gk0, gka=gka, gkb=gkb, g_q_na=g_q_na, g_k_na=g_k_na)
    wr = jnp.concatenate([w_router[0], jnp.zeros((D, LANES - N_EXPERTS), F32)], axis=1)
    wr_hi = wr.astype(BF16)
    wr_lo = (wr - wr_hi.astype(F32)).astype(BF16)
    bias = _na_bias_table(rpb_na[0], rows)

    cvec = jnp.concatenate([c, c_ctx[None, :], jnp.zeros((8 - B - 1, D), F32)], axis=0)
    mod = _mod_call(cvec.T, w_mod[0], b_mod, B + 1)
    mod_lat = mod[:B].reshape(B, N_MOD, D)
    mod_ctx = mod[B:].reshape(1, N_MOD, D)

    x2 = x.reshape(B * T, D)
    p_lat = _inproj_call(x2, mod_lat, g_norm1, w_in_p, T, min(1024, T))
    p_ctx = _inproj_call(ctx.reshape(B * C, D), mod_ctx, g_norm1, w_in_p, B * C, C)
    qm, km, vm, qn, kn = _prep_call(p_lat, B, T, min(512, T), consts, False)
    kmc, vmc, knc = _prep_call(p_ctx, B, C, C, consts, True)
    o_m = _mla_call(qm, km, vm, kmc, vmc, min(1024, T), min(512, T))
    o_n = _na_call(qn, kn, p_lat, knc, p_ctx, bias)
    x1, h2, aff = _merge_call(o_m.reshape(B * T, -1), o_n.reshape(B * T, -1), x2, mod_lat,
                              g_out_mla, g_out_na, w_out[0].astype(BF16), g_norm2, wr_hi, wr_lo,
                              T, min(512, T))

    slot_t, idx, gate, bounds = _route_call(aff.reshape(B, T, LANES), cap)
    y = _ffn_call(idx.reshape(-1), h2, gate, w_gate[0], w_up[0], w_down[0], B, T, cap)
    out = _combine_call(bounds, x1, mod_lat, slot_t, y, B, T, cap)
    return out.reshape(B, T, D)
```

```python
import functools

import numpy as np
import jax
import jax.numpy as jnp
from jax import lax
from jax.experimental import pallas as pl
from jax.experimental.pallas import tpu as pltpu

F32 = jnp.float32
BF16 = jnp.bfloat16

GRID_W = 64
MLA_HEADS = 8
MLA_NOPE = 128
MLA_ROPE = 64
MLA_QK_DIM = MLA_NOPE + MLA_ROPE
MLA_V = 128
MLA_Q_RANK = 512
MLA_KV_RANK = 256
NA_HEADS = 8
NA_HEAD_DIM = 128
NA_KH = 8
NA_KW = 16
N_EXPERTS = 16
EC_CAPACITY_FACTOR = 2
ROPE_THETA = 10000.0
EPS = 1e-6
N_MOD = 6
MLA_SCALE = MLA_QK_DIM ** -0.5
NA_SCALE = NA_HEAD_DIM ** -0.5
LOG2E = 1.4426950408889634

LANES = 128
MLA_QK_PAD = 2 * LANES
NA_W = NA_HEADS * NA_HEAD_DIM
P_WIDTH = 3 * NA_W + 1024
VMEM_LIMIT = 56 * 1024 * 1024
NEG = -1e30


def _f32(v):
    return jnp.asarray(v, dtype=F32)


def _mult(v, m):
    return v if isinstance(v, int) else pl.multiple_of(v, m)


def _cparams(sem, vmem=VMEM_LIMIT):
    return pltpu.CompilerParams(dimension_semantics=sem, vmem_limit_bytes=vmem)


def _mod_kernel(ct_ref, w_ref, b_ref, o_ref):
    ct = ct_ref[...]
    s = ct * (1.0 / (1.0 + jnp.exp(-ct)))
    w = w_ref[...]
    rows = []
    for m in range(o_ref.shape[0]):
        rows.append(jnp.sum(w * s[:, m:m + 1], axis=0, keepdims=True))
    o_ref[...] = jnp.concatenate(rows, axis=0) + b_ref[...]


def _mod_call(ct, w_mod, b_mod, n_rows):
    D, N = w_mod.shape
    tn = 512
    return pl.pallas_call(
        _mod_kernel,
        out_shape=jax.ShapeDtypeStruct((n_rows, N), F32),
        grid=(N // tn,),
        in_specs=[pl.BlockSpec((D, 8), lambda j: (0, 0)),
                  pl.BlockSpec((D, tn), lambda j: (0, j)),
                  pl.BlockSpec((1, tn), lambda j: (0, j))],
        out_specs=pl.BlockSpec((n_rows, tn), lambda j: (0, j)),
        compiler_params=_cparams(("arbitrary",)),
        name="mod",
    )(ct, w_mod, b_mod)


def _inproj_kernel(x_ref, mod_ref, g_ref, w_ref, o_ref, hn_ref):
    @pl.when(pl.program_id(1) == 0)
    def _():
        x = x_ref[...]
        y = x * lax.rsqrt(jnp.mean(x * x, axis=-1, keepdims=True) + EPS) * g_ref[...]
        h = y * (1.0 + mod_ref[0, 1:2, :]) + mod_ref[0, 0:1, :]
        hn_ref[...] = h.astype(BF16)

    o_ref[...] = jnp.dot(hn_ref[...], w_ref[...], preferred_element_type=F32).astype(BF16)


def _inproj_call(x2, mod, g, w, rows_per_mod, tm):
    M, D = x2.shape
    N = w.shape[1]
    tn = 1024
    per = rows_per_mod // tm
    return pl.pallas_call(
        _inproj_kernel,
        out_shape=jax.ShapeDtypeStruct((M, N), BF16),
        grid=(M // tm, N // tn),
        in_specs=[pl.BlockSpec((tm, D), lambda i, j: (i, 0)),
                  pl.BlockSpec((1, N_MOD, D), lambda i, j: (i // per, 0, 0)),
                  pl.BlockSpec((1, D), lambda i, j: (0, 0)),
                  pl.BlockSpec((D, tn), lambda i, j: (0, j))],
        out_specs=pl.BlockSpec((tm, tn), lambda i, j: (i, j)),
        scratch_shapes=[pltpu.VMEM((tm, D), BF16)],
        compiler_params=_cparams(("arbitrary", "arbitrary")),
        name="inproj",
    )(x2, mod, g, w)


def _rms(x, g):
    return x * lax.rsqrt(jnp.mean(x * x, axis=-1, keepdims=True) + EPS) * g


def _prep_kernel(*refs, is_ctx):
    if is_ctx:
        (pk_ref, pm_ref, gkva_ref, wk_ref, wv_ref, gk0_ref, gka_ref, gkn_ref,
         km_ref, vm_ref, kn_ref) = refs
    else:
        (pq_ref, pk_ref, pm_ref, ct_ref, st_ref, gqa_ref, wq_ref, gkva_ref, wk_ref, wv_ref,
         gq0_ref, gqa2_ref, gqb2_ref, gk0_ref, gka_ref, gkb_ref, gqn_ref, gkn_ref,
         qm_ref, km_ref, vm_ref, qn_ref, kn_ref) = refs
    tm = pm_ref.shape[0]
    lane = lax.broadcasted_iota(jnp.int32, (1, LANES), 1)
    rope_mask = (lane < MLA_ROPE).astype(F32)
    pm = pm_ref[...].astype(F32)
    inv_qk = 1.0 / MLA_QK_DIM

    ckvn = _rms(pm[:, MLA_Q_RANK:MLA_Q_RANK + MLA_KV_RANK], gkva_ref[...]).astype(BF16)
    kn = jnp.dot(ckvn, wk_ref[...], preferred_element_type=F32)
    vv = jnp.dot(ckvn, wv_ref[...], preferred_element_type=F32)
    kpe = pm[:, MLA_Q_RANK + MLA_KV_RANK:MLA_Q_RANK + MLA_KV_RANK + LANES]
    ss_pe = jnp.sum(kpe * kpe * rope_mask, axis=-1, keepdims=True)
    if is_ctx:
        k_rope = kpe * gka_ref[...]
    else:
        ct = ct_ref[...]
        st = st_ref[...]
        k_rope = kpe * (ct * gka_ref[...]) + pltpu.roll(kpe, MLA_ROPE, 1) * (st * gkb_ref[...])
    for h in range(MLA_HEADS):
        k0 = kn[:, h * LANES:(h + 1) * LANES]
        r = lax.rsqrt((jnp.sum(k0 * k0, axis=-1, keepdims=True) + ss_pe) * inv_qk + EPS)
        km_ref[0, h] = jnp.concatenate([k0 * r * gk0_ref[...], k_rope * r], axis=-1).astype(BF16)
        vm_ref[0, h] = jnp.concatenate([vv[:, h * LANES:(h + 1) * LANES], jnp.ones((tm, LANES), F32)],
                                       axis=-1).astype(BF16)

    pk = pk_ref[...].astype(F32)
    for h in range(NA_HEADS):
        kn_ref[0, h] = _rms(pk[:, h * LANES:(h + 1) * LANES], gkn_ref[...]).astype(BF16)

    if is_ctx:
        return

    cqn = _rms(pm[:, :MLA_Q_RANK], gqa_ref[...]).astype(BF16)
    q = jnp.dot(cqn, wq_ref[...], preferred_element_type=F32)
    qa = ct * gqa2_ref[...]
    qb = st * gqb2_ref[...]
    for h in range(MLA_HEADS):
        t0 = q[:, h * MLA_QK_PAD:h * MLA_QK_PAD + LANES]
        t1 = q[:, h * MLA_QK_PAD + LANES:(h + 1) * MLA_QK_PAD]
        ss = jnp.sum(t0 * t0, axis=-1, keepdims=True) + jnp.sum(t1 * t1 * rope_mask, axis=-1, keepdims=True)
        r = lax.rsqrt(ss * inv_qk + EPS) * (MLA_SCALE * LOG2E)
        o1 = t1 * qa + pltpu.roll(t1, MLA_ROPE, 1) * qb
        qm_ref[0, h] = jnp.concatenate([t0 * r * gq0_ref[...], o1 * r], axis=-1).astype(BF16)

    pq = pq_ref[...].astype(F32)
    for h in range(NA_HEADS):
        qn_ref[0, h] = (_rms(pq[:, h * LANES:(h + 1) * LANES], gqn_ref[...]) * (NA_SCALE * LOG2E)).astype(BF16)


def _prep_call(p, nb, tok, tm, consts, is_ctx):
    per = tok // tm
    H = MLA_HEADS
    row = lambda i: (i, 0)
    full = lambda i: (0, 0)
    hm = lambda i: (i // per, 0, i % per, 0)
    p_spec = lambda c: pl.BlockSpec((tm, 1024), lambda i, c=c: (i, c))
    vec = lambda n: pl.BlockSpec((1, n), full)
    km_s = jax.ShapeDtypeStruct((nb, H, tok, MLA_QK_PAD), BF16)
    h128_s = jax.ShapeDtypeStruct((nb, H, tok, LANES), BF16)
    km_o = pl.BlockSpec((1, H, tm, MLA_QK_PAD), hm)
    h128_o = pl.BlockSpec((1, H, tm, LANES), hm)
    c = consts
    if is_ctx:
        ins = [p, p, c["g_kva"], c["wk"], c["wv"], c["gk0"], c["gka"], c["g_k_na"]]
        in_specs = [p_spec(1), p_spec(3), vec(MLA_KV_RANK),
                    pl.BlockSpec(c["wk"].shape, full), pl.BlockSpec(c["wv"].shape, full),
                    vec(LANES), vec(LANES), vec(LANES)]
        out_shape = [km_s, km_s, h128_s]
        out_specs = [km_o, km_o, h128_o]
    else:
        ins = [p, p, p, c["ct"], c["st"], c["g_qa"], c["wq"], c["g_kva"], c["wk"], c["wv"],
               c["gq0"], c["gqa"], c["gqb"], c["gk0"], c["gka"], c["gkb"], c["g_q_na"], c["g_k_na"]]
        tab = pl.BlockSpec((tm, LANES), lambda i: (i % per, 0))
        in_specs = [p_spec(0), p_spec(1), p_spec(3), tab, tab, vec(MLA_Q_RANK),
                    pl.BlockSpec(c["wq"].shape, full), vec(MLA_KV_RANK),
                    pl.BlockSpec(c["wk"].shape, full), pl.BlockSpec(c["wv"].shape, full),
                    vec(LANES), vec(LANES), vec(LANES), vec(LANES), vec(LANES), vec(LANES),
                    vec(LANES), vec(LANES)]
        out_shape = [km_s, km_s, km_s, h128_s, h128_s]
        out_specs = [km_o, km_o, km_o, h128_o, h128_o]
    return pl.pallas_call(
        functools.partial(_prep_kernel, is_ctx=is_ctx),
        out_shape=out_shape,
        grid=(nb * per,),
        in_specs=in_specs,
        out_specs=out_specs,
        compiler_params=_cparams(("arbitrary",)),
        name="prep_ctx" if is_ctx else "prep_lat",
    )(*ins)


MLA_SUB = 2


def _mla_kernel(q_ref, k_ref, v_ref, kc_ref, vc_ref, o_ref, m_ref, acc_ref,
                sa_ref, sb_ref, sc_ref, mxa_ref, mxb_ref, mxc_ref, *, tq, tk):
    ts = tq // MLA_SUB
    T = k_ref.shape[2]
    n = T // tk
    nq = T // tq
    dn = (((1,), (1,)), ((), ()))

    def scores_into(t, keys, s_ref, mx_ref):
        q = q_ref[0, 0, pl.ds(_mult(t * tq, tq), tq), :]
        s = lax.dot_general(q, keys, dn, preferred_element_type=F32)
        s_ref[...] = s
        mx_ref[...] = jnp.broadcast_to(jnp.max(s, axis=-1, keepdims=True), mx_ref.shape)

    def keys(c):
        return k_ref[0, 0, pl.ds(_mult(c * tk, tk), tk), :]

    def values(c):
        return v_ref[0, 0, pl.ds(_mult(c * tk, tk), tk), :]

    def softmax_pv(s_ref, mx_ref, v):
        for u in range(MLA_SUB):
            rows = slice(u * ts, (u + 1) * ts)
            m_old = m_ref[rows, :]
            m_new = jnp.maximum(m_old, mx_ref[rows, :])
            alpha = jnp.exp2(m_old - m_new)
            p = jnp.concatenate([jnp.exp2(s_ref[rows, j * LANES:(j + 1) * LANES] - m_new)
                                 for j in range(s_ref.shape[1] // LANES)], axis=-1).astype(BF16)
            acc_ref[rows, :] = (jnp.concatenate([alpha] * (acc_ref.shape[1] // LANES), axis=-1) * acc_ref[rows, :]
                                + jnp.dot(p, v, preferred_element_type=F32))
            m_ref[rows, :] = m_new

    scores_into(0, keys(0), sa_ref, mxa_ref)

    def tile_body(t, carry):
        m_ref[...] = jnp.full(m_ref.shape, NEG, F32)
        acc_ref[...] = jnp.zeros(acc_ref.shape, F32)

        def pair_body(i, c2):
            c = 2 * i
            scores_into(t, keys(c + 1), sb_ref, mxb_ref)
            softmax_pv(sa_ref, mxa_ref, values(c))
            scores_into(t, keys(c + 2), sa_ref, mxa_ref)
            softmax_pv(sb_ref, mxb_ref, values(c + 1))
            return c2

        lax.fori_loop(0, n // 2 - 1, pair_body, 0)
        scores_into(t, keys(n - 1), sb_ref, mxb_ref)
        softmax_pv(sa_ref, mxa_ref, values(n - 2))
        scores_into(t, kc_ref[0, 0], sc_ref, mxc_ref)
        softmax_pv(sb_ref, mxb_ref, values(n - 1))
        scores_into(jnp.minimum(t + 1, nq - 1), keys(0), sa_ref, mxa_ref)
        softmax_pv(sc_ref, mxc_ref, vc_ref[0, 0])
        o_ref[0, pl.ds(_mult(t * tq, tq), tq), :] = (acc_ref[:, :MLA_V] / acc_ref[:, MLA_V:]).astype(o_ref.dtype)
        return carry

    lax.fori_loop(0, nq, tile_body, 0)


def _mla_call(qm, km, vm, kmc, vmc, tq, tk):
    B, H, T, _ = qm.shape
    C = kmc.shape[2]
    VW = vm.shape[3]
    assert (T // tk) % 2 == 0 and T % tq == 0
    bh = lambda b, h: (b, h, 0, 0)
    return pl.pallas_call(
        functools.partial(_mla_kernel, tq=tq, tk=tk),
        out_shape=jax.ShapeDtypeStruct((B, T, H * MLA_V), BF16),
        grid=(B, H),
        in_specs=[pl.BlockSpec((1, 1, T, MLA_QK_PAD), bh),
                  pl.BlockSpec((1, 1, T, MLA_QK_PAD), bh),
                  pl.BlockSpec((1, 1, T, VW), bh),
                  pl.BlockSpec((1, 1, C, MLA_QK_PAD), bh),
                  pl.BlockSpec((1, 1, C, VW), bh)],
        out_specs=pl.BlockSpec((1, T, MLA_V), lambda b, h: (b, 0, h)),
        scratch_shapes=[pltpu.VMEM((tq, LANES), F32), pltpu.VMEM((tq, VW), F32),
                        pltpu.VMEM((tq, tk), F32), pltpu.VMEM((tq, tk), F32), pltpu.VMEM((tq, C), F32),
                        pltpu.VMEM((tq, LANES), F32), pltpu.VMEM((tq, LANES), F32), pltpu.VMEM((tq, LANES), F32)],
        compiler_params=_cparams(("arbitrary", "arbitrary")),
        name="mla_attn",
    )(qm, km, vm, kmc, vmc)


NA_QROWS = 2
NA_TILE = NA_QROWS * GRID_W
NA_BAND = 5
NA_CFGS = 5


def _na_band_start(i, nblk):
    return jnp.clip(i - 2, 0, nblk - NA_BAND)


def _na_kernel(q_ref, k0, k1, k2, k3, k4, v0, v1, v2, v3, v4, kc_ref, vc_ref, bias_ref, o_ref,
               sl_ref, sx_ref):
    k_refs = (k0, k1, k2, k3, k4)
    v_refs = (v0, v1, v2, v3, v4)
    dn = (((1,), (1,)), ((), ()))

    def scores(h, slot):
        q = q_ref[0, h]
        kcat = jnp.concatenate([r[0, h] for r in k_refs], axis=0)
        sl_ref[slot] = lax.dot_general(q, kcat, dn, preferred_element_type=F32) + bias_ref[0, h]
        sx_ref[slot] = lax.dot_general(q, kc_ref[0, h], dn, preferred_element_type=F32)

    def softmax_pv(h, slot):
        s_loc = sl_ref[slot]
        s_ctx = sx_ref[slot]
        m = jnp.maximum(jnp.max(s_loc, axis=-1, keepdims=True), jnp.max(s_ctx, axis=-1, keepdims=True))
        p_loc = jnp.exp2(s_loc - m)
        p_ctx = jnp.exp2(s_ctx - m)
        l = jnp.sum(p_loc, axis=-1, keepdims=True) + jnp.sum(p_ctx, axis=-1, keepdims=True)
        cols = slice(h * LANES, (h + 1) * LANES)
        vcat = jnp.concatenate([r[:, cols] for r in v_refs], axis=0)
        o = (jnp.dot(p_loc.astype(BF16), vcat, preferred_element_type=F32)
             + jnp.dot(p_ctx.astype(BF16), vc_ref[:, cols], preferred_element_type=F32))
        o_ref[0, :, cols] = (o / l).astype(o_ref.dtype)

    scores(0, 0)
    for h in range(NA_HEADS):
        if h + 1 < NA_HEADS:
            scores(h + 1, (h + 1) % 2)
        softmax_pv(h, h % 2)


def _na_call(qn, kn, p_lat, knc, p_ctx, bias):
    B, H, T, _ = qn.shape
    C = knc.shape[2]
    nblk = T // NA_TILE
    tiles_per_b = T // NA_TILE

    def cfg(i):
        return jnp.minimum(i, 2) + jnp.maximum(i - (nblk - 3), 0)

    k_specs = [pl.BlockSpec((1, H, NA_TILE, LANES),
                            lambda b, i, u=u: (b, 0, _na_band_start(i, nblk) + u, 0)) for u in range(NA_BAND)]
    v_specs = [pl.BlockSpec((NA_TILE, NA_W),
                            lambda b, i, u=u: (b * tiles_per_b + _na_band_start(i, nblk) + u, 2))
               for u in range(NA_BAND)]
    return pl.pallas_call(
        _na_kernel,
        out_shape=jax.ShapeDtypeStruct((B, T, NA_W), BF16),
        grid=(B, nblk),
        in_specs=[pl.BlockSpec((1, H, NA_TILE, LANES), lambda b, i: (b, 0, i, 0))] + k_specs + v_specs + [
            pl.BlockSpec((1, H, C, LANES), lambda b, i: (b, 0, 0, 0)),
            pl.BlockSpec((C, NA_W), lambda b, i: (b, 2)),
            pl.BlockSpec((1, H, NA_TILE, NA_BAND * NA_TILE), lambda b, i: (cfg(i), 0, 0, 0))],
        out_specs=pl.BlockSpec((1, NA_TILE, NA_W), lambda b, i: (b, i, 0)),
        scratch_shapes=[pltpu.VMEM((2, NA_TILE, NA_BAND * NA_TILE), F32), pltpu.VMEM((2, NA_TILE, C), F32)],
        compiler_params=_cparams(("arbitrary", "arbitrary")),
        name="na_attn",
    )(qn, *([kn] * NA_BAND), *([p_lat] * NA_BAND), knc, p_ctx, bias)


def _na_bias_table(rpb, rows):
    nblk = rows // NA_QROWS
    reps = [0, 1, 2, nblk - 2, nblk - 1]
    q_r = np.arange(NA_TILE) // GRID_W
    q_c = np.arange(NA_TILE) % GRID_W
    k_r = np.arange(NA_BAND * NA_TILE) // GRID_W
    k_c = np.arange(NA_BAND * NA_TILE) % GRID_W
    n_ro, n_co, n_kb = 2 * NA_KH - 1, 2 * NA_KW - 1, NA_BAND * NA_QROWS
    col_sel = (np.arange(n_co)[:, None, None]
               == np.arange(GRID_W)[None, None, :] - np.arange(GRID_W)[None, :, None] + NA_KW - 1)
    row_sel = np.zeros((NA_CFGS, NA_QROWS, n_kb, n_ro), np.float32)
    ok = []
    for c, i in enumerate(reps):
        j0 = int(np.clip(i - 2, 0, nblk - NA_BAND))
        r = NA_QROWS * i + q_r
        rs = np.clip(r - NA_KH // 2, 0, rows - NA_KH)
        cs = np.clip(q_c - NA_KW // 2, 0, GRID_W - NA_KW)
        kr = NA_QROWS * j0 + k_r
        ok.append((kr[None, :] >= rs[:, None]) & (kr[None, :] < rs[:, None] + NA_KH)
                  & (k_c[None, :] >= cs[:, None]) & (k_c[None, :] < cs[:, None] + NA_KW))
        for qr in range(NA_QROWS):
            for kb in range(n_kb):
                row_sel[c, qr, kb, int(np.clip(NA_QROWS * (j0 - i) + kb - qr + NA_KH - 1, 0, n_ro - 1))] = 1.0
    table = jnp.einsum('cabr,hrd,dqk->chaqbk', row_sel, rpb, col_sel.astype(np.float32),
                       precision=lax.Precision.HIGHEST)
    table = table.reshape(NA_CFGS, rpb.shape[0], NA_TILE, NA_BAND * NA_TILE)
    return jnp.where(np.stack(ok)[:, None], table * LOG2E, NEG).astype(F32)


def _merge_kernel(om_ref, on_ref, x_ref, mod_ref, gom_ref, gon_ref, wout_ref, g2_ref, wrh_ref, wrl_ref,
                  x1_ref, h2_ref, aff_ref):
    ts = x_ref.shape[0] // MERGE_SUB
    for s in range(MERGE_SUB):
        rows = slice(s * ts, (s + 1) * ts)
        a = _rms(om_ref[rows, :].astype(F32), gom_ref[...]).astype(BF16)
        b = _rms(on_ref[rows, :].astype(F32), gon_ref[...]).astype(BF16)
        y = jnp.dot(jnp.concatenate([a, b], axis=-1), wout_ref[...], preferred_element_type=F32)
        x1 = x_ref[rows, :] + mod_ref[0, 2:3, :] * y
        x1_ref[rows, :] = x1
        h2 = _rms(x1, g2_ref[...]) * (1.0 + mod_ref[0, 4:5, :]) + mod_ref[0, 3:4, :]
        h2_ref[rows, :] = h2
        hi = h2.astype(BF16)
        lo = (h2 - hi.astype(F32)).astype(BF16)
        logits = (jnp.dot(hi, wrh_ref[...], preferred_element_type=F32)
                  + jnp.dot(lo, wrh_ref[...], preferred_element_type=F32)
                  + jnp.dot(hi, wrl_ref[...], preferred_element_type=F32))
        lane = lax.broadcasted_iota(jnp.int32, logits.shape, 1)
        logits = jnp.where(lane < N_EXPERTS, logits, NEG)
        e = jnp.exp(logits - jnp.max(logits, axis=-1, keepdims=True))
        aff_ref[rows, :] = e / jnp.sum(e, axis=-1, keepdims=True)


MERGE_SUB = 2


def _merge_call(om, on, x2, mod, gom, gon, wout, g2, wrh, wrl, rows_per_mod, tm):
    M, D = x2.shape
    per = rows_per_mod // tm
    row = lambda i: (i, 0)
    full = lambda i: (0, 0)
    return pl.pallas_call(
        _merge_kernel,
        out_shape=[jax.ShapeDtypeStruct((M, D), F32), jax.ShapeDtypeStruct((M, D), F32),
                   jax.ShapeDtypeStruct((M, LANES), F32)],
        grid=(M // tm,),
        in_specs=[pl.BlockSpec((tm, om.shape[1]), row), pl.BlockSpec((tm, on.shape[1]), row),
                  pl.BlockSpec((tm, D), row),
                  pl.BlockSpec((1, N_MOD, D), lambda i: (i // per, 0, 0)),
                  pl.BlockSpec((1, om.shape[1]), full), pl.BlockSpec((1, on.shape[1]), full),
                  pl.BlockSpec(wout.shape, full), pl.BlockSpec((1, D), full),
                  pl.BlockSpec(wrh.shape, full), pl.BlockSpec(wrl.shape, full)],
        out_specs=[pl.BlockSpec((tm, D), row), pl.BlockSpec((tm, D), row), pl.BlockSpec((tm, LANES), row)],
        compiler_params=_cparams(("arbitrary",)),
        name="merge_router",
    )(om, on, x2, mod, gom, gon, wout, g2, wrh, wrl)


RT_CHUNK = 256


def _select_kernel(aff_ref, slot_ref, afft_ref, bounds_ref, *, cap):
    T = aff_ref.shape[1]
    E = N_EXPERTS
    n_tiles = T // LANES
    capf = float(cap)

    def tr_body(c, carry):
        off = _mult(c * RT_CHUNK, RT_CHUNK)
        afft_ref[0, :, pl.ds(off, RT_CHUNK)] = aff_ref[0, pl.ds(off, RT_CHUNK), :].T[:E]
        return carry

    lax.fori_loop(0, T // RT_CHUNK, tr_body, 0)

    def as_float(bits):
        return lax.bitcast_convert_type(bits, F32)

    def count_ge(v):
        return jnp.sum((afft_ref[0] >= v).astype(F32), axis=1, keepdims=True)

    def bs_body(i, prefix):
        cand = prefix | jnp.left_shift(jnp.int32(1), 30 - i)
        return jnp.where(count_ge(as_float(cand)) >= capf, cand, prefix)

    thr_bits = lax.fori_loop(0, 31, bs_body, jnp.zeros((E, 1), jnp.int32))
    thr = as_float(thr_bits)
    above = as_float(jnp.maximum(thr_bits + 1, jnp.int32(0x00800000)))
    need = capf - count_ge(above)

    ri = lax.broadcasted_iota(jnp.int32, (LANES, LANES), 0)
    ci = lax.broadcasted_iota(jnp.int32, (LANES, LANES), 1)
    upper = (ri <= ci).astype(BF16)
    lane = lax.broadcasted_iota(jnp.int32, (E, LANES), 1)
    run_eq = jnp.zeros((E, 1), F32)
    run_sel = jnp.zeros((E, 1), F32)
    bounds = jnp.zeros((E, LANES), F32)
    for k in range(n_tiles):
        a = afft_ref[0, :, k * LANES:(k + 1) * LANES]
        gt = a >= above
        eq = (a >= thr) & (a < above)
        eqf = eq.astype(F32)
        incl_eq = jnp.dot(eqf.astype(BF16), upper, preferred_element_type=F32)
        sel = gt | (eq & ((incl_eq - eqf + run_eq) < need))
        self_ = sel.astype(F32)
        incl_sel = jnp.dot(self_.astype(BF16), upper, preferred_element_type=F32)
        slot_ref[0, :, k * LANES:(k + 1) * LANES] = jnp.where(sel, incl_sel - self_ + run_sel, -1.0)
        bounds = jnp.where(lane == k, run_sel, bounds)
        run_eq = run_eq + incl_eq[:, LANES - 1:LANES]
        run_sel = run_sel + incl_sel[:, LANES - 1:LANES]
    bounds_ref[0] = jnp.where(lane >= n_tiles, run_sel, bounds).astype(jnp.int32)


def _compact_kernel(bnd_ref, slot_ref, afft_ref, idx_ref, gate_ref, acc_i_ref, acc_g_ref, *, cap):
    b = pl.program_id(0)
    T = slot_ref.shape[2]
    n_tiles = T // LANES
    sub = lax.broadcasted_iota(jnp.int32, (LANES, LANES), 0).astype(F32)
    lane_f = lax.broadcasted_iota(jnp.int32, (1, LANES), 1).astype(F32)

    for e in range(N_EXPERTS):
        acc_i_ref[...] = jnp.zeros(acc_i_ref.shape, F32)
        acc_g_ref[...] = jnp.zeros(acc_g_ref.shape, F32)
        row = (b * N_EXPERTS + e) * LANES

        def tile_body(k, carry, e=e, row=row):
            lo = bnd_ref[row + k]
            hi = bnd_ref[row + k + 1]
            off = _mult(k * LANES, LANES)
            srow = slot_ref[0, e:e + 1, pl.ds(off, LANES)]
            arow = afft_ref[0, e:e + 1, pl.ds(off, LANES)]
            trow = lane_f + _f32(off)

            def block_body(sb, c2):
                s0 = _mult(sb * LANES, LANES)
                hit = srow == (sub + _f32(s0))
                acc_i_ref[pl.ds(s0, LANES), :] += jnp.where(hit, trow, 0.0)
                acc_g_ref[pl.ds(s0, LANES), :] += jnp.where(hit, arow, 0.0)
                return c2

            lax.fori_loop(lo // LANES, (hi + LANES - 1) // LANES, block_body, 0)
            return carry

        lax.fori_loop(0, n_tiles, tile_body, 0)
        idx_ref[0, e] = jnp.sum(acc_i_ref[...].T, axis=0, keepdims=True).astype(jnp.int32)
        gate_ref[0, e] = jnp.sum(acc_g_ref[...].T, axis=0, keepdims=True)


def _route_call(aff, cap):
    B, T, _ = aff.shape
    E = N_EXPERTS
    assert T // LANES < LANES
    bet = lambda b: (b, 0, 0)
    slot_t, aff_t, bounds = pl.pallas_call(
        functools.partial(_select_kernel, cap=cap),
        out_shape=[jax.ShapeDtypeStruct((B, E, T), F32), jax.ShapeDtypeStruct((B, E, T), F32),
                   jax.ShapeDtypeStruct((B, E, LANES), jnp.int32)],
        grid=(B,),
        in_specs=[pl.BlockSpec((1, T, LANES), bet)],
        out_specs=[pl.BlockSpec((1, E, T), bet), pl.BlockSpec((1, E, T), bet), pl.BlockSpec((1, E, LANES), bet)],
        compiler_params=_cparams(("arbitrary",)),
        name="route_select",
    )(aff)
    bounds_flat = bounds.reshape(-1)
    grid_spec = pltpu.PrefetchScalarGridSpec(
        num_scalar_prefetch=1,
        grid=(B,),
        in_specs=[pl.BlockSpec((1, E, T), lambda b, bnd: (b, 0, 0)),
                  pl.BlockSpec((1, E, T), lambda b, bnd: (b, 0, 0))],
        out_specs=[pl.BlockSpec((1, E, 1, cap), lambda b, bnd: (b, 0, 0, 0)),
                   pl.BlockSpec((1, E, 1, cap), lambda b, bnd: (b, 0, 0, 0))],
        scratch_shapes=[pltpu.VMEM((cap, LANES), F32), pltpu.VMEM((cap, LANES), F32)],
    )
    idx, gate = pl.pallas_call(
        functools.partial(_compact_kernel, cap=cap),
        out_shape=[jax.ShapeDtypeStruct((B, E, 1, cap), jnp.int32), jax.ShapeDtypeStruct((B, E, 1, cap), F32)],
        grid_spec=grid_spec,
        compiler_params=_cparams(("arbitrary",)),
        name="route_compact",
    )(bounds_flat, slot_t, aff_t)
    return slot_t, idx, gate, bounds_flat


def _ffn_kernel(idx_ref, h2_hbm, gate_ref, wg_ref, wu_ref, wd_ref, y_ref, xs_ref, xb_ref, hmid_ref, sem,
                *, cap, tokens, n_batch, n_steps):
    e = pl.program_id(0)
    b = pl.program_id(1)
    f = pl.program_id(2)
    n_exp = pl.num_programs(0)
    nf = pl.num_programs(2)
    rows_per_step = cap // n_steps

    def row_copy(bb, ee, s):
        t = idx_ref[(bb * n_exp + ee) * cap + s]
        return pltpu.make_async_copy(h2_hbm.at[pl.ds(bb * tokens + t, 1)], xs_ref.at[pl.ds(s, 1)], sem)

    def wait_rows():
        pltpu.make_async_copy(h2_hbm.at[pl.ds(0, cap)], xs_ref, sem).wait()

    @pl.when((e == 0) & (b == 0) & (f == 0))
    def _():
        def issue(s, carry):
            row_copy(b, e, s).start()
            return carry

        lax.fori_loop(0, cap, issue, 0, unroll=8)

    @pl.when(f == 0)
    def _():
        wait_rows()
        xb_ref[...] = xs_ref[...].astype(BF16)

    wrap_b = b + 1 == n_batch
    nb = jnp.where(wrap_b, 0, b + 1)
    ne = jnp.where(wrap_b, jnp.where(e + 1 == n_exp, 0, e + 1), e)
    for k in range(rows_per_step):
        row_copy(nb, ne, f * rows_per_step + k).start()

    xb = xb_ref[...]
    a = jnp.dot(xb, wg_ref[0].astype(BF16), preferred_element_type=F32)
    u = jnp.dot(xb, wu_ref[0].astype(BF16), preferred_element_type=F32)
    tf = wg_ref.shape[2]
    hmid_ref[:, pl.ds(_mult(f * tf, tf), tf)] = (a * (1.0 / (1.0 + jnp.exp(-a))) * u).astype(BF16)

    @pl.when(f == nf - 1)
    def _():
        hmid = hmid_ref[...]
        gate = jnp.broadcast_to(gate_ref[0, 0], (LANES, cap)).T[:, :1]
        for n in range(y_ref.shape[3] // FFN_TN):
            cols = slice(n * FFN_TN, (n + 1) * FFN_TN)
            part = jnp.dot(hmid, wd_ref[0, :, cols].astype(BF16), preferred_element_type=F32)
            y_ref[0, 0, :, cols] = (part * gate).astype(y_ref.dtype)

    @pl.when((e == n_exp - 1) & (b == n_batch - 1) & (f == nf - 1))
    def _():
        wait_rows()


FFN_TF = 256
FFN_TN = 512


def _ffn_call(idx_flat, h2, gate, wg, wu, wd, B, T, cap):
    E, D, F = wg.shape
    tf = FFN_TF
    grid_spec = pltpu.PrefetchScalarGridSpec(
        num_scalar_prefetch=1,
        grid=(E, B, F // tf),
        in_specs=[pl.BlockSpec(memory_space=pl.ANY),
                  pl.BlockSpec((1, 1, 1, cap), lambda e, b, f, idx: (b, e, 0, 0)),
                  pl.BlockSpec((1, D, tf), lambda e, b, f, idx: (e, 0, f)),
                  pl.BlockSpec((1, D, tf), lambda e, b, f, idx: (e, 0, f)),
                  pl.BlockSpec((1, F, D), lambda e, b, f, idx: (e, 0, 0))],
        out_specs=pl.BlockSpec((1, 1, cap, D), lambda e, b, f, idx: (b, e, 0, 0)),
        scratch_shapes=[pltpu.VMEM((cap, D), F32), pltpu.VMEM((cap, D), BF16), pltpu.VMEM((cap, F), BF16),
                        pltpu.SemaphoreType.DMA(())],
    )
    return pl.pallas_call(
        functools.partial(_ffn_kernel, cap=cap, tokens=T, n_batch=B, n_steps=F // tf),
        out_shape=jax.ShapeDtypeStruct((B, E, cap, D), BF16),
        grid_spec=grid_spec,
        compiler_params=_cparams(("arbitrary", "arbitrary", "arbitrary")),
        name="expert_ffn",
    )(idx_flat, h2, gate, wg, wu, wd)


CB_WIN = 64
CB_ALIGN = 16


def _combine_kernel(bnd_ref, x1_ref, mod_ref, slot_ref, y_hbm, o_ref, buf_ref, acc_ref, sem,
                    *, cap, nch, n_batch):
    b = pl.program_id(0)
    j = pl.program_id(1)
    E = N_EXPERTS
    step = b * nch + j
    cur = step % 2
    sub = lax.broadcasted_iota(jnp.int32, (CB_WIN, RT_CHUNK), 0).astype(F32)

    def tile_windows(bb, jj):
        per_tile = RT_CHUNK // LANES
        aligned, n_pass = [], jnp.int32(0)
        for e in range(E):
            at = (bb * E + e) * LANES + jj * per_tile
            base = bnd_ref[at]
            cnt = bnd_ref[at + per_tile] - base
            al = (base // CB_ALIGN) * CB_ALIGN
            aligned.append(al)
            n_pass = jnp.maximum(n_pass, (base - al + cnt + CB_WIN - 1) // CB_WIN)
        return aligned, n_pass

    def window(al, c):
        lo = al + c * CB_WIN
        return lo, _mult(jnp.minimum(lo, cap - CB_WIN), CB_ALIGN)

    def start_fetch(bb, aligned, c, slot):
        for e in range(E):
            _, st = window(aligned[e], c)
            pltpu.make_async_copy(y_hbm.at[bb, e, pl.ds(st, CB_WIN)],
                                  buf_ref.at[slot, pl.ds(e * CB_WIN, CB_WIN)], sem.at[slot]).start()

    def wait_fetch(slot):
        pltpu.make_async_copy(buf_ref.at[slot], buf_ref.at[slot], sem.at[slot]).wait()

    def scatter_add(aligned, c, slot):
        pieces = []
        for e in range(E):
            lo, st = window(aligned[e], c)
            srow = slot_ref[0, e:e + 1, :]
            pieces.append(((srow == sub + _f32(st)) & (srow >= _f32(lo))).astype(BF16))
        sel_t = jnp.concatenate(pieces, axis=0)
        return lax.dot_general(sel_t, buf_ref[slot], (((0,), (0,)), ((), ())), preferred_element_type=F32)

    aligned, n_pass = tile_windows(b, j)

    @pl.when(step == 0)
    def _():
        start_fetch(b, aligned, 0, cur)

    @pl.when(step + 1 < n_batch * nch)
    def _():
        wrap = j + 1 == nch
        nb = jnp.where(wrap, b + 1, b)
        nj = jnp.where(wrap, 0, j + 1)
        start_fetch(nb, tile_windows(nb, nj)[0], 0, 1 - cur)

    wait_fetch(cur)
    acc_ref[...] = scatter_add(aligned, 0, cur)

    def extra_pass(c, carry):
        start_fetch(b, aligned, c, cur)
        wait_fetch(cur)
        acc_ref[...] += scatter_add(aligned, c, cur)
        return carry

    lax.fori_loop(1, n_pass, extra_pass, 0)
    o_ref[...] = x1_ref[...] + mod_ref[0, 5:6, :] * acc_ref[...]


def _combine_call(bounds_flat, x1, mod, slot_t, y, B, T, cap):
    D = x1.shape[1]
    nch = T // RT_CHUNK
    E = N_EXPERTS
    grid_spec = pltpu.PrefetchScalarGridSpec(
        num_scalar_prefetch=1,
        grid=(B, nch),
        in_specs=[pl.BlockSpec((RT_CHUNK, D), lambda b, j, bnd: (b * nch + j, 0)),
                  pl.BlockSpec((1, N_MOD, D), lambda b, j, bnd: (b, 0, 0)),
                  pl.BlockSpec((1, E, RT_CHUNK), lambda b, j, bnd: (b, 0, j)),
                  pl.BlockSpec(memory_space=pl.ANY)],
        out_specs=pl.BlockSpec((RT_CHUNK, D), lambda b, j, bnd: (b * nch + j, 0)),
        scratch_shapes=[pltpu.VMEM((2, E * CB_WIN, D), BF16), pltpu.VMEM((RT_CHUNK, D), F32),
                        pltpu.SemaphoreType.DMA((2,))],
    )
    return pl.pallas_call(
        functools.partial(_combine_kernel, cap=cap, nch=nch, n_batch=B),
        out_shape=jax.ShapeDtypeStruct(x1.shape, F32),
        grid_spec=grid_spec,
        compiler_params=_cparams(("arbitrary", "arbitrary")),
        name="combine",
    )(bounds_flat, x1, mod, slot_t, y)


def _rope_tables(n_tokens):
    t = np.arange(n_tokens)
    row = (t // GRID_W).astype(np.float64)
    col = (t % GRID_W).astype(np.float64)
    n_freq = MLA_ROPE // 4
    inv = ROPE_THETA ** (-np.arange(n_freq, dtype=np.float64) / n_freq)
    ang = np.concatenate([row[:, None] * inv, col[:, None] * inv], axis=-1)
    cos, sin = np.cos(ang), np.sin(ang)
    z = np.zeros((n_tokens, LANES - MLA_ROPE))
    return (np.concatenate([cos, cos, z], axis=-1).astype(np.float32),
            np.concatenate([-sin, sin, z], axis=-1).astype(np.float32))


def _pad_lanes(v):
    return jnp.concatenate([v, jnp.zeros((LANES - v.shape[0],), v.dtype)])[None, :]


def kernel(x, c, ctx, c_ctx, w_mod, b_mod, g_norm1, w_in, g_qa, w_qb, g_kva, w_kvb, g_q_mla, g_k_mla,
           g_q_na, g_k_na, rpb_na, g_out_mla, g_out_na, w_out, g_norm2, w_router, w_gate, w_up, w_down):
    B, T, D = x.shape
    C = ctx.shape[1]
    assert w_mod.shape[0] == 1, "single-layer problem"
    assert T % GRID_W == 0 and (T // GRID_W) % NA_QROWS == 0 and T // GRID_W >= 2 * NA_BAND
    assert w_in.shape[2] == 3 * NA_W + MLA_Q_RANK + MLA_KV_RANK + MLA_ROPE
    rows = T // GRID_W
    cap = EC_CAPACITY_FACTOR * T // N_EXPERTS
    half = MLA_ROPE // 2

    wi = w_in[0]
    o_q, o_kv, o_pe, o_na = 0, MLA_Q_RANK, MLA_Q_RANK + MLA_KV_RANK, MLA_Q_RANK + MLA_KV_RANK + MLA_ROPE
    w_in_p = jnp.concatenate(
        [wi[:, o_na:], wi[:, o_q:o_pe], wi[:, o_pe:o_na], wi[:, o_pe + half:o_na], wi[:, o_pe:o_pe + half],
         jnp.zeros((D, P_WIDTH - wi.shape[1] - MLA_ROPE), F32)], axis=1).astype(BF16)
    wq3 = w_qb[0].reshape(MLA_Q_RANK, MLA_HEADS, MLA_QK_DIM)
    wq_p = jnp.concatenate([wq3, wq3[..., MLA_NOPE + half:], wq3[..., MLA_NOPE:MLA_NOPE + half]], axis=-1)
    wq_p = wq_p.reshape(MLA_Q_RANK, MLA_HEADS * MLA_QK_PAD).astype(BF16)
    wkv3 = w_kvb[0].reshape(MLA_KV_RANK, MLA_HEADS, MLA_NOPE + MLA_V)
    wk_p = wkv3[..., :MLA_NOPE].reshape(MLA_KV_RANK, MLA_HEADS * MLA_NOPE).astype(BF16)
    wv_p = wkv3[..., MLA_NOPE:].reshape(MLA_KV_RANK, MLA_HEADS * MLA_V).astype(BF16)

    def rope_gains(g):
        gr = g[MLA_NOPE:]
        return g[None, :MLA_NOPE], _pad_lanes(gr), _pad_lanes(jnp.concatenate([gr[half:], gr[:half]]))

    gq0, gqa, gqb = rope_gains(g_q_mla[0])
    gk0, gka, gkb = rope_gains(g_k_mla[0])
    ct, st = _rope_tables(T)
    consts = dict(ct=ct, st=st, g_qa=g_qa, wq=wq_p, g_kva=g_kva, wk=wk_p, wv=wv_p,
                  gq0=gq0, gqa=gqa, gqb=gqb, gk0=gk0, gka=gka, gkb=gkb, g_q_na=g_q_na, g_k_na=g_k_na)
    wr = jnp.concatenate([w_router[0], jnp.zeros((D, LANES - N_EXPERTS), F32)], axis=1)
    wr_hi = wr.astype(BF16)
    wr_lo = (wr - wr_hi.astype(F32)).astype(BF16)
    bias = _na_bias_table(rpb_na[0], rows)

    cvec = jnp.concatenate([c, c_ctx[None, :], jnp.zeros((8 - B - 1, D), F32)], axis=0)
    mod = _mod_call(cvec.T, w_mod[0], b_mod, B + 1)
    mod_lat = mod[:B].reshape(B, N_MOD, D)
    mod_ctx = mod[B:].reshape(1, N_MOD, D)

    x2 = x.reshape(B * T, D)
    p_lat = _inproj_call(x2, mod_lat, g_norm1, w_in_p, T, min(1024, T))
    p_ctx = _inproj_call(ctx.reshape(B * C, D), mod_ctx, g_norm1, w_in_p, B * C, C)
    qm, km, vm, qn, kn = _prep_call(p_lat, B, T, min(512, T), consts, False)
    kmc, vmc, knc = _prep_call(p_ctx, B, C, C, consts, True)
    o_m = _mla_call(qm, km, vm, kmc, vmc, min(1024, T), min(512, T))
    o_n = _na_call(qn, kn, p_lat, knc, p_ctx, bias)
    x1, h2, aff = _merge_call(o_m.reshape(B * T, -1), o_n.reshape(B * T, -1), x2, mod_lat,
                              g_out_mla, g_out_na, w_out[0].astype(BF16), g_norm2, wr_hi, wr_lo,
                              T, min(512, T))

    slot_t, idx, gate, bounds = _route_call(aff.reshape(B, T, LANES), cap)
    y = _ffn_call(idx.reshape(-1), h2, gate, w_gate[0], w_up[0], w_down[0], B, T, cap)
    out = _combine_call(bounds, x1, mod_lat, slot_t, y, B, T, cap)
    return out.reshape(B, T, D)
```

```python
import functools

import numpy as np
import jax
import jax.numpy as jnp
from jax import lax
from jax.experimental import pallas as pl
from jax.experimental.pallas import tpu as pltpu

F32 = jnp.float32
BF16 = jnp.bfloat16

GRID_W = 64
MLA_HEADS = 8
MLA_NOPE = 128
MLA_ROPE = 64
MLA_QK_DIM = MLA_NOPE + MLA_ROPE
MLA_V = 128
MLA_Q_RANK = 512
MLA_KV_RANK = 256
NA_HEADS = 8
NA_HEAD_DIM = 128
NA_KH = 8
NA_KW = 16
N_EXPERTS = 16
EC_CAPACITY_FACTOR = 2
ROPE_THETA = 10000.0
EPS = 1e-6
N_MOD = 6
MLA_SCALE = MLA_QK_DIM ** -0.5
NA_SCALE = NA_HEAD_DIM ** -0.5
LOG2E = 1.4426950408889634

LANES = 128
MLA_QK_PAD = 2 * LANES
NA_W = NA_HEADS * NA_HEAD_DIM
P_WIDTH = 3 * NA_W + 1024
VMEM_LIMIT = 56 * 1024 * 1024
NEG = -1e30


def _f32(v):
    return jnp.asarray(v, dtype=F32)


def _mult(v, m):
    return v if isinstance(v, int) else pl.multiple_of(v, m)


def _cparams(sem, vmem=VMEM_LIMIT):
    return pltpu.CompilerParams(dimension_semantics=sem, vmem_limit_bytes=vmem)


def _mod_kernel(ct_ref, w_ref, b_ref, o_ref):
    ct = ct_ref[...]
    s = ct * (1.0 / (1.0 + jnp.exp(-ct)))
    w = w_ref[...]
    rows = []
    for m in range(o_ref.shape[0]):
        rows.append(jnp.sum(w * s[:, m:m + 1], axis=0, keepdims=True))
    o_ref[...] = jnp.concatenate(rows, axis=0) + b_ref[...]


def _mod_call(ct, w_mod, b_mod, n_rows):
    D, N = w_mod.shape
    tn = 512
    return pl.pallas_call(
        _mod_kernel,
        out_shape=jax.ShapeDtypeStruct((n_rows, N), F32),
        grid=(N // tn,),
        in_specs=[pl.BlockSpec((D, 8), lambda j: (0, 0)),
                  pl.BlockSpec((D, tn), lambda j: (0, j)),
                  pl.BlockSpec((1, tn), lambda j: (0, j))],
        out_specs=pl.BlockSpec((n_rows, tn), lambda j: (0, j)),
        compiler_params=_cparams(("arbitrary",)),
        name="mod",
    )(ct, w_mod, b_mod)


def _inproj_kernel(x_ref, mod_ref, g_ref, w_ref, o_ref, hn_ref):
    @pl.when(pl.program_id(1) == 0)
    def _():
        x = x_ref[...]
        y = x * lax.rsqrt(jnp.mean(x * x, axis=-1, keepdims=True) + EPS) * g_ref[...]
        h = y * (1.0 + mod_ref[0, 1:2, :]) + mod_ref[0, 0:1, :]
        hn_ref[...] = h.astype(BF16)

    o_ref[...] = jnp.dot(hn_ref[...], w_ref[...], preferred_element_type=F32).astype(BF16)


def _inproj_call(x2, mod, g, w, rows_per_mod, tm):
    M, D = x2.shape
    N = w.shape[1]
    tn = 1024
    per = rows_per_mod // tm
    return pl.pallas_call(
        _inproj_kernel,
        out_shape=jax.ShapeDtypeStruct((M, N), BF16),
        grid=(M // tm, N // tn),
        in_specs=[pl.BlockSpec((tm, D), lambda i, j: (i, 0)),
                  pl.BlockSpec((1, N_MOD, D), lambda i, j: (i // per, 0, 0)),
                  pl.BlockSpec((1, D), lambda i, j: (0, 0)),
                  pl.BlockSpec((D, tn), lambda i, j: (0, j))],
        out_specs=pl.BlockSpec((tm, tn), lambda i, j: (i, j)),
        scratch_shapes=[pltpu.VMEM((tm, D), BF16)],
        compiler_params=_cparams(("arbitrary", "arbitrary")),
        name="inproj",
    )(x2, mod, g, w)


def _rms(x, g):
    return x * lax.rsqrt(jnp.mean(x * x, axis=-1, keepdims=True) + EPS) * g


def _prep_kernel(*refs, is_ctx):
    if is_ctx:
        (pk_ref, pm_ref, gkva_ref, wk_ref, wv_ref, gk0_ref, gka_ref, gkn_ref,
         km_ref, vm_ref, kn_ref) = refs
    else:
        (pq_ref, pk_ref, pm_ref, ct_ref, st_ref, gqa_ref, wq_ref, gkva_ref, wk_ref, wv_ref,
         gq0_ref, gqa2_ref, gqb2_ref, gk0_ref, gka_ref, gkb_ref, gqn_ref, gkn_ref,
         qm_ref, km_ref, vm_ref, qn_ref, kn_ref) = refs
    tm = pm_ref.shape[0]
    lane = lax.broadcasted_iota(jnp.int32, (1, LANES), 1)
    rope_mask = (lane < MLA_ROPE).astype(F32)
    pm = pm_ref[...].astype(F32)
    inv_qk = 1.0 / MLA_QK_DIM

    ckvn = _rms(pm[:, MLA_Q_RANK:MLA_Q_RANK + MLA_KV_RANK], gkva_ref[...]).astype(BF16)
    kn = jnp.dot(ckvn, wk_ref[...], preferred_element_type=F32)
    vv = jnp.dot(ckvn, wv_ref[...], preferred_element_type=F32)
    kpe = pm[:, MLA_Q_RANK + MLA_KV_RANK:MLA_Q_RANK + MLA_KV_RANK + LANES]
    ss_pe = jnp.sum(kpe * kpe * rope_mask, axis=-1, keepdims=True)
    if is_ctx:
        k_rope = kpe * gka_ref[...]
    else:
        ct = ct_ref[...]
        st = st_ref[...]
        k_rope = kpe * (ct * gka_ref[...]) + pltpu.roll(kpe, MLA_ROPE, 1) * (st * gkb_ref[...])
    for h in range(MLA_HEADS):
        k0 = kn[:, h * LANES:(h + 1) * LANES]
        r = lax.rsqrt((jnp.sum(k0 * k0, axis=-1, keepdims=True) + ss_pe) * inv_qk + EPS)
        km_ref[0, h] = jnp.concatenate([k0 * r * gk0_ref[...], k_rope * r], axis=-1).astype(BF16)
        vm_ref[0, h] = jnp.concatenate([vv[:, h * LANES:(h + 1) * LANES], jnp.ones((tm, LANES), F32)],
                                       axis=-1).astype(BF16)

    pk = pk_ref[...].astype(F32)
    for h in range(NA_HEADS):
        kn_ref[0, h] = _rms(pk[:, h * LANES:(h + 1) * LANES], gkn_ref[...]).astype(BF16)

    if is_ctx:
        return

    cqn = _rms(pm[:, :MLA_Q_RANK], gqa_ref[...]).astype(BF16)
    q = jnp.dot(cqn, wq_ref[...], preferred_element_type=F32)
    qa = ct * gqa2_ref[...]
    qb = st * gqb2_ref[...]
    for h in range(MLA_HEADS):
        t0 = q[:, h * MLA_QK_PAD:h * MLA_QK_PAD + LANES]
        t1 = q[:, h * MLA_QK_PAD + LANES:(h + 1) * MLA_QK_PAD]
        ss = jnp.sum(t0 * t0, axis=-1, keepdims=True) + jnp.sum(t1 * t1 * rope_mask, axis=-1, keepdims=True)
        r = lax.rsqrt(ss * inv_qk + EPS) * (MLA_SCALE * LOG2E)
        o1 = t1 * qa + pltpu.roll(t1, MLA_ROPE, 1) * qb
        qm_ref[0, h] = jnp.concatenate([t0 * r * gq0_ref[...], o1 * r], axis=-1).astype(BF16)

    pq = pq_ref[...].astype(F32)
    for h in range(NA_HEADS):
        qn_ref[0, h] = (_rms(pq[:, h * LANES:(h + 1) * LANES], gqn_ref[...]) * (NA_SCALE * LOG2E)).astype(BF16)


def _prep_call(p, nb, tok, tm, consts, is_ctx):
    per = tok // tm
    H = MLA_HEADS
    row = lambda i: (i, 0)
    full = lambda i: (0, 0)
    hm = lambda i: (i // per, 0, i % per, 0)
    p_spec = lambda c: pl.BlockSpec((tm, 1024), lambda i, c=c: (i, c))
    vec = lambda n: pl.BlockSpec((1, n), full)
    km_s = jax.ShapeDtypeStruct((nb, H, tok, MLA_QK_PAD), BF16)
    h128_s = jax.ShapeDtypeStruct((nb, H, tok, LANES), BF16)
    km_o = pl.BlockSpec((1, H, tm, MLA_QK_PAD), hm)
    h128_o = pl.BlockSpec((1, H, tm, LANES), hm)
    c = consts
    if is_ctx:
        ins = [p, p, c["g_kva"], c["wk"], c["wv"], c["gk0"], c["gka"], c["g_k_na"]]
        in_specs = [p_spec(1), p_spec(3), vec(MLA_KV_RANK),
                    pl.BlockSpec(c["wk"].shape, full), pl.BlockSpec(c["wv"].shape, full),
                    vec(LANES), vec(LANES), vec(LANES)]
        out_shape = [km_s, km_s, h128_s]
        out_specs = [km_o, km_o, h128_o]
    else:
        ins = [p, p, p, c["ct"], c["st"], c["g_qa"], c["wq"], c["g_kva"], c["wk"], c["wv"],
               c["gq0"], c["gqa"], c["gqb"], c["gk0"], c["gka"], c["gkb"], c["g_q_na"], c["g_k_na"]]
        tab = pl.BlockSpec((tm, LANES), lambda i: (i % per, 0))
        in_specs = [p_spec(0), p_spec(1), p_spec(3), tab, tab, vec(MLA_Q_RANK),
                    pl.BlockSpec(c["wq"].shape, full), vec(MLA_KV_RANK),
                    pl.BlockSpec(c["wk"].shape, full), pl.BlockSpec(c["wv"].shape, full),
                    vec(LANES), vec(LANES), vec(LANES), vec(LANES), vec(LANES), vec(LANES),
                    vec(LANES), vec(LANES)]
        out_shape = [km_s, km_s, km_s, h128_s, h128_s]
        out_specs = [km_o, km_o, km_o, h128_o, h128_o]
    return pl.pallas_call(
        functools.partial(_prep_kernel, is_ctx=is_ctx),
        out_shape=out_shape,
        grid=(nb * per,),
        in_specs=in_specs,
        out_specs=out_specs,
        compiler_params=_cparams(("arbitrary",)),
        name="prep_ctx" if is_ctx else "prep_lat",
    )(*ins)


MLA_UNROLL = 4
MLA_SUB = 2


def _mla_kernel(q_ref, k_ref, v_ref, kc_ref, vc_ref, o_ref, m_ref, acc_ref,
                sa_ref, sb_ref, sc_ref, mxa_ref, mxb_ref, mxc_ref, *, tq, tk):
    ts = tq // MLA_SUB
    T = k_ref.shape[2]
    n = T // tk
    nq = T // tq
    dn = (((1,), (1,)), ((), ()))

    def scores_into(t, keys, s_ref, mx_ref):
        q = q_ref[0, 0, pl.ds(_mult(t * tq, tq), tq), :]
        s = lax.dot_general(q, keys, dn, preferred_element_type=F32)
        s_ref[...] = s
        mx_ref[...] = jnp.broadcast_to(jnp.max(s, axis=-1, keepdims=True), mx_ref.shape)

    def keys(c):
        return k_ref[0, 0, pl.ds(_mult(c * tk, tk), tk), :]

    def values(c):
        return v_ref[0, 0, pl.ds(_mult(c * tk, tk), tk), :]

    def softmax_pv(s_ref, mx_ref, v):
        for u in range(MLA_SUB):
            rows = slice(u * ts, (u + 1) * ts)
            m_old = m_ref[rows, :]
            m_new = jnp.maximum(m_old, mx_ref[rows, :])
            alpha = jnp.exp2(m_old - m_new)
            p = jnp.concatenate([jnp.exp2(s_ref[rows, j * LANES:(j + 1) * LANES] - m_new)
                                 for j in range(s_ref.shape[1] // LANES)], axis=-1).astype(BF16)
            acc_ref[rows, :] = (jnp.concatenate([alpha] * (acc_ref.shape[1] // LANES), axis=-1) * acc_ref[rows, :]
                                + jnp.dot(p, v, preferred_element_type=F32))
            m_ref[rows, :] = m_new

    scores_into(0, keys(0), sa_ref, mxa_ref)

    def tile_body(t, carry):
        m_ref[...] = jnp.full(m_ref.shape, NEG, F32)
        acc_ref[...] = jnp.zeros(acc_ref.shape, F32)

        bufs = ((sa_ref, mxa_ref), (sb_ref, mxb_ref))

        def group_body(i, c2):
            c0 = MLA_UNROLL * i
            for u in range(MLA_UNROLL):
                scores_into(t, keys(c0 + u + 1), *bufs[(u + 1) % 2])
                softmax_pv(*bufs[u % 2], values(c0 + u))
            return c2

        n_loop = (n - 2) // MLA_UNROLL * MLA_UNROLL
        lax.fori_loop(0, n_loop // MLA_UNROLL, group_body, 0)
        for c in range(n_loop, n):
            if c + 1 < n:
                scores_into(t, keys(c + 1), *bufs[(c + 1) % 2])
            else:
                scores_into(t, kc_ref[0, 0], sc_ref, mxc_ref)
            softmax_pv(*bufs[c % 2], values(c))
        scores_into(jnp.minimum(t + 1, nq - 1), keys(0), sa_ref, mxa_ref)
        softmax_pv(sc_ref, mxc_ref, vc_ref[0, 0])
        o_ref[0, pl.ds(_mult(t * tq, tq), tq), :] = (acc_ref[:, :MLA_V] / acc_ref[:, MLA_V:]).astype(o_ref.dtype)
        return carry

    lax.fori_loop(0, nq, tile_body, 0)


def _mla_call(qm, km, vm, kmc, vmc, tq, tk):
    B, H, T, _ = qm.shape
    C = kmc.shape[2]
    VW = vm.shape[3]
    assert (T // tk) % 2 == 0 and T % tq == 0
    bh = lambda b, h: (b, h, 0, 0)
    return pl.pallas_call(
        functools.partial(_mla_kernel, tq=tq, tk=tk),
        out_shape=jax.ShapeDtypeStruct((B, T, H * MLA_V), BF16),
        grid=(B, H),
        in_specs=[pl.BlockSpec((1, 1, T, MLA_QK_PAD), bh),
                  pl.BlockSpec((1, 1, T, MLA_QK_PAD), bh),
                  pl.BlockSpec((1, 1, T, VW), bh),
                  pl.BlockSpec((1, 1, C, MLA_QK_PAD), bh),
                  pl.BlockSpec((1, 1, C, VW), bh)],
        out_specs=pl.BlockSpec((1, T, MLA_V), lambda b, h: (b, 0, h)),
        scratch_shapes=[pltpu.VMEM((tq, LANES), F32), pltpu.VMEM((tq, VW), F32),
                        pltpu.VMEM((tq, tk), F32), pltpu.VMEM((tq, tk), F32), pltpu.VMEM((tq, C), F32),
                        pltpu.VMEM((tq, LANES), F32), pltpu.VMEM((tq, LANES), F32), pltpu.VMEM((tq, LANES), F32)],
        compiler_params=_cparams(("arbitrary", "arbitrary")),
        name="mla_attn",
    )(qm, km, vm, kmc, vmc)


NA_QROWS = 2
NA_TILE = NA_QROWS * GRID_W
NA_BAND = 5
NA_CFGS = 5


def _na_band_start(i, nblk):
    return jnp.clip(i - 2, 0, nblk - NA_BAND)


def _na_kernel(q_ref, k0, k1, k2, k3, k4, v0, v1, v2, v3, v4, kc_ref, vc_ref, bias_ref, o_ref,
               sl_ref, sx_ref):
    k_refs = (k0, k1, k2, k3, k4)
    v_refs = (v0, v1, v2, v3, v4)
    dn = (((1,), (1,)), ((), ()))

    def scores(h, slot):
        q = q_ref[0, h]
        kcat = jnp.concatenate([r[0, h] for r in k_refs], axis=0)
        sl_ref[slot] = lax.dot_general(q, kcat, dn, preferred_element_type=F32) + bias_ref[0, h]
        sx_ref[slot] = lax.dot_general(q, kc_ref[0, h], dn, preferred_element_type=F32)

    def softmax_pv(h, slot):
        s_loc = sl_ref[slot]
        s_ctx = sx_ref[slot]
        m = jnp.maximum(jnp.max(s_loc, axis=-1, keepdims=True), jnp.max(s_ctx, axis=-1, keepdims=True))
        p_loc = jnp.exp2(s_loc - m).astype(BF16)
        p_ctx = jnp.exp2(s_ctx - m).astype(BF16)
        cols = slice(h * LANES, (h + 1) * LANES)
        vcat = jnp.concatenate([r[:, cols] for r in v_refs], axis=0)
        vcat = jnp.concatenate([vcat, jnp.ones(vcat.shape, BF16)], axis=-1)
        vctx = jnp.concatenate([vc_ref[:, cols], jnp.ones((vc_ref.shape[0], LANES), BF16)], axis=-1)
        o = (jnp.dot(p_loc, vcat, preferred_element_type=F32) + jnp.dot(p_ctx, vctx, preferred_element_type=F32))
        o_ref[0, :, cols] = (o[:, :LANES] / o[:, LANES:]).astype(o_ref.dtype)

    scores(0, 0)
    for h in range(NA_HEADS):
        if h + 1 < NA_HEADS:
            scores(h + 1, (h + 1) % 2)
        softmax_pv(h, h % 2)


def _na_call(qn, kn, p_lat, knc, p_ctx, bias):
    B, H, T, _ = qn.shape
    C = knc.shape[2]
    nblk = T // NA_TILE
    tiles_per_b = T // NA_TILE

    def cfg(i):
        return jnp.minimum(i, 2) + jnp.maximum(i - (nblk - 3), 0)

    k_specs = [pl.BlockSpec((1, H, NA_TILE, LANES),
                            lambda b, i, u=u: (b, 0, _na_band_start(i, nblk) + u, 0)) for u in range(NA_BAND)]
    v_specs = [pl.BlockSpec((NA_TILE, NA_W),
                            lambda b, i, u=u: (b * tiles_per_b + _na_band_start(i, nblk) + u, 2))
               for u in range(NA_BAND)]
    return pl.pallas_call(
        _na_kernel,
        out_shape=jax.ShapeDtypeStruct((B, T, NA_W), BF16),
        grid=(B, nblk),
        in_specs=[pl.BlockSpec((1, H, NA_TILE, LANES), lambda b, i: (b, 0, i, 0))] + k_specs + v_specs + [
            pl.BlockSpec((1, H, C, LANES), lambda b, i: (b, 0, 0, 0)),
            pl.BlockSpec((C, NA_W), lambda b, i: (b, 2)),
            pl.BlockSpec((1, H, NA_TILE, NA_BAND * NA_TILE), lambda b, i: (cfg(i), 0, 0, 0))],
        out_specs=pl.BlockSpec((1, NA_TILE, NA_W), lambda b, i: (b, i, 0)),
        scratch_shapes=[pltpu.VMEM((2, NA_TILE, NA_BAND * NA_TILE), F32), pltpu.VMEM((2, NA_TILE, C), F32)],
        compiler_params=_cparams(("arbitrary", "arbitrary")),
        name="na_attn",
    )(qn, *([kn] * NA_BAND), *([p_lat] * NA_BAND), knc, p_ctx, bias)


def _na_bias_table(rpb, rows):
    nblk = rows // NA_QROWS
    reps = [0, 1, 2, nblk - 2, nblk - 1]
    q_r = np.arange(NA_TILE) // GRID_W
    q_c = np.arange(NA_TILE) % GRID_W
    k_r = np.arange(NA_BAND * NA_TILE) // GRID_W
    k_c = np.arange(NA_BAND * NA_TILE) % GRID_W
    n_ro, n_co = 2 * NA_KH - 1, 2 * NA_KW - 1
    col_sel = (np.arange(n_co)[:, None, None]
               == np.arange(GRID_W)[None, None, :] - np.arange(GRID_W)[None, :, None] + NA_KW - 1)
    toep = jnp.einsum('hrd,dqk->hrqk', rpb, col_sel.astype(np.float32), precision=lax.Precision.HIGHEST)
    toep = jnp.pad(toep, ((0, 0), (1, 1), (0, 0), (0, 0)))
    pair = jnp.concatenate([toep[:, :-1], toep[:, 1:]], axis=-1)
    tables, ok = [], []
    for i in reps:
        j0 = int(np.clip(i - 2, 0, nblk - NA_BAND))
        r = NA_QROWS * i + q_r
        rs = np.clip(r - NA_KH // 2, 0, rows - NA_KH)
        cs = np.clip(q_c - NA_KW // 2, 0, GRID_W - NA_KW)
        kr = NA_QROWS * j0 + k_r
        ok.append((kr[None, :] >= rs[:, None]) & (kr[None, :] < rs[:, None] + NA_KH)
                  & (k_c[None, :] >= cs[:, None]) & (k_c[None, :] < cs[:, None] + NA_KW))
        per_qr = []
        for qr in range(NA_QROWS):
            pieces = []
            for kt in range(NA_BAND):
                ro = NA_QROWS * (j0 - i) + NA_QROWS * kt - qr + NA_KH - 1
                pieces.append(pair[:, int(np.clip(ro + 1, 0, n_ro))])
            per_qr.append(jnp.concatenate(pieces, axis=-1))
        tables.append(jnp.concatenate(per_qr, axis=1))
    return jnp.where(np.stack(ok)[:, None], jnp.stack(tables) * LOG2E, NEG).astype(F32)


def _merge_kernel(om_ref, on_ref, x_ref, mod_ref, gom_ref, gon_ref, wout_ref, g2_ref, wrh_ref, wrl_ref,
                  x1_ref, h2_ref, aff_ref):
    ts = x_ref.shape[0] // MERGE_SUB
    for s in range(MERGE_SUB):
        rows = slice(s * ts, (s + 1) * ts)
        a = _rms(om_ref[rows, :].astype(F32), gom_ref[...]).astype(BF16)
        b = _rms(on_ref[rows, :].astype(F32), gon_ref[...]).astype(BF16)
        y = jnp.dot(jnp.concatenate([a, b], axis=-1), wout_ref[...], preferred_element_type=F32)
        x1 = x_ref[rows, :] + mod_ref[0, 2:3, :] * y
        x1_ref[rows, :] = x1
        h2 = _rms(x1, g2_ref[...]) * (1.0 + mod_ref[0, 4:5, :]) + mod_ref[0, 3:4, :]
        h2_ref[rows, :] = h2
        hi = h2.astype(BF16)
        lo = (h2 - hi.astype(F32)).astype(BF16)
        logits = (jnp.dot(hi, wrh_ref[...], preferred_element_type=F32)
                  + jnp.dot(lo, wrh_ref[...], preferred_element_type=F32)
                  + jnp.dot(hi, wrl_ref[...], preferred_element_type=F32))
        lane = lax.broadcasted_iota(jnp.int32, logits.shape, 1)
        logits = jnp.where(lane < N_EXPERTS, logits, NEG)
        e = jnp.exp(logits - jnp.max(logits, axis=-1, keepdims=True))
        aff_ref[rows, :] = e / jnp.sum(e, axis=-1, keepdims=True)


MERGE_SUB = 2


def _merge_call(om, on, x2, mod, gom, gon, wout, g2, wrh, wrl, rows_per_mod, tm):
    M, D = x2.shape
    per = rows_per_mod // tm
    row = lambda i: (i, 0)
    full = lambda i: (0, 0)
    return pl.pallas_call(
        _merge_kernel,
        out_shape=[jax.ShapeDtypeStruct((M, D), F32), jax.ShapeDtypeStruct((M, D), F32),
                   jax.ShapeDtypeStruct((M, LANES), F32)],
        grid=(M // tm,),
        in_specs=[pl.BlockSpec((tm, om.shape[1]), row), pl.BlockSpec((tm, on.shape[1]), row),
                  pl.BlockSpec((tm, D), row),
                  pl.BlockSpec((1, N_MOD, D), lambda i: (i // per, 0, 0)),
                  pl.BlockSpec((1, om.shape[1]), full), pl.BlockSpec((1, on.shape[1]), full),
                  pl.BlockSpec(wout.shape, full), pl.BlockSpec((1, D), full),
                  pl.BlockSpec(wrh.shape, full), pl.BlockSpec(wrl.shape, full)],
        out_specs=[pl.BlockSpec((tm, D), row), pl.BlockSpec((tm, D), row), pl.BlockSpec((tm, LANES), row)],
        compiler_params=_cparams(("arbitrary",)),
        name="merge_router",
    )(om, on, x2, mod, gom, gon, wout, g2, wrh, wrl)


RT_CHUNK = 256


def _select_kernel(aff_ref, slot_ref, afft_ref, bounds_ref, *, cap):
    T = aff_ref.shape[1]
    E = N_EXPERTS
    n_tiles = T // LANES
    capf = float(cap)

    def tr_body(c, carry):
        off = _mult(c * RT_CHUNK, RT_CHUNK)
        afft_ref[0, :, pl.ds(off, RT_CHUNK)] = aff_ref[0, pl.ds(off, RT_CHUNK), :].T[:E]
        return carry

    lax.fori_loop(0, T // RT_CHUNK, tr_body, 0)

    def as_float(bits):
        return lax.bitcast_convert_type(bits, F32)

    def count_ge(v):
        return jnp.sum((afft_ref[0] >= v).astype(F32), axis=1, keepdims=True)

    def bs_body(i, prefix):
        cand = prefix | jnp.left_shift(jnp.int32(1), 30 - i)
        return jnp.where(count_ge(as_float(cand)) >= capf, cand, prefix)

    thr_bits = lax.fori_loop(0, 31, bs_body, jnp.zeros((E, 1), jnp.int32))
    thr = as_float(thr_bits)
    above = as_float(jnp.maximum(thr_bits + 1, jnp.int32(0x00800000)))
    need = capf - count_ge(above)

    ri = lax.broadcasted_iota(jnp.int32, (LANES, LANES), 0)
    ci = lax.broadcasted_iota(jnp.int32, (LANES, LANES), 1)
    upper = (ri <= ci).astype(BF16)
    lane = lax.broadcasted_iota(jnp.int32, (E, LANES), 1)
    run_eq = jnp.zeros((E, 1), F32)
    run_sel = jnp.zeros((E, 1), F32)
    bounds = jnp.zeros((E, LANES), F32)
    for k in range(n_tiles):
        a = afft_ref[0, :, k * LANES:(k + 1) * LANES]
        gt = a >= above
        eq = (a >= thr) & (a < above)
        eqf = eq.astype(F32)
        incl_eq = jnp.dot(eqf.astype(BF16), upper, preferred_element_type=F32)
        sel = gt | (eq & ((incl_eq - eqf + run_eq) < need))
        self_ = sel.astype(F32)
        incl_sel = jnp.dot(self_.astype(BF16), upper, preferred_element_type=F32)
        slot_ref[0, :, k * LANES:(k + 1) * LANES] = jnp.where(sel, incl_sel - self_ + run_sel, -1.0)
        bounds = jnp.where(lane == k, run_sel, bounds)
        run_eq = run_eq + incl_eq[:, LANES - 1:LANES]
        run_sel = run_sel + incl_sel[:, LANES - 1:LANES]
    bounds_ref[0] = jnp.where(lane >= n_tiles, run_sel, bounds).astype(jnp.int32)


def _compact_kernel(bnd_ref, slot_ref, afft_ref, idx_ref, gate_ref, acc_i_ref, acc_g_ref, *, cap):
    b = pl.program_id(0)
    T = slot_ref.shape[2]
    n_tiles = T // LANES
    sub = lax.broadcasted_iota(jnp.int32, (LANES, LANES), 0).astype(F32)
    lane_f = lax.broadcasted_iota(jnp.int32, (1, LANES), 1).astype(F32)

    for e in range(N_EXPERTS):
        acc_i_ref[...] = jnp.zeros(acc_i_ref.shape, F32)
        acc_g_ref[...] = jnp.zeros(acc_g_ref.shape, F32)
        row = (b * N_EXPERTS + e) * LANES

        def tile_body(k, carry, e=e, row=row):
            lo = bnd_ref[row + k]
            hi = bnd_ref[row + k + 1]
            off = _mult(k * LANES, LANES)
            srow = slot_ref[0, e:e + 1, pl.ds(off, LANES)]
            arow = afft_ref[0, e:e + 1, pl.ds(off, LANES)]
            trow = lane_f + _f32(off)

            def block_body(sb, c2):
                s0 = _mult(sb * LANES, LANES)
                hit = srow == (sub + _f32(s0))
                acc_i_ref[pl.ds(s0, LANES), :] += jnp.where(hit, trow, 0.0)
                acc_g_ref[pl.ds(s0, LANES), :] += jnp.where(hit, arow, 0.0)
                return c2

            lax.fori_loop(lo // LANES, (hi + LANES - 1) // LANES, block_body, 0)
            return carry

        lax.fori_loop(0, n_tiles, tile_body, 0)
        idx_ref[0, e] = jnp.sum(acc_i_ref[...].T, axis=0, keepdims=True).astype(jnp.int32)
        gate_ref[0, e] = jnp.sum(acc_g_ref[...].T, axis=0, keepdims=True)


def _route_call(aff, cap):
    B, T, _ = aff.shape
    E = N_EXPERTS
    assert T // LANES < LANES
    bet = lambda b: (b, 0, 0)
    slot_t, aff_t, bounds = pl.pallas_call(
        functools.partial(_select_kernel, cap=cap),
        out_shape=[jax.ShapeDtypeStruct((B, E, T), F32), jax.ShapeDtypeStruct((B, E, T), F32),
                   jax.ShapeDtypeStruct((B, E, LANES), jnp.int32)],
        grid=(B,),
        in_specs=[pl.BlockSpec((1, T, LANES), bet)],
        out_specs=[pl.BlockSpec((1, E, T), bet), pl.BlockSpec((1, E, T), bet), pl.BlockSpec((1, E, LANES), bet)],
        compiler_params=_cparams(("arbitrary",)),
        name="route_select",
    )(aff)
    bounds_flat = bounds.reshape(-1)
    grid_spec = pltpu.PrefetchScalarGridSpec(
        num_scalar_prefetch=1,
        grid=(B,),
        in_specs=[pl.BlockSpec((1, E, T), lambda b, bnd: (b, 0, 0)),
                  pl.BlockSpec((1, E, T), lambda b, bnd: (b, 0, 0))],
        out_specs=[pl.BlockSpec((1, E, 1, cap), lambda b, bnd: (b, 0, 0, 0)),
                   pl.BlockSpec((1, E, 1, cap), lambda b, bnd: (b, 0, 0, 0))],
        scratch_shapes=[pltpu.VMEM((cap, LANES), F32), pltpu.VMEM((cap, LANES), F32)],
    )
    idx, gate = pl.pallas_call(
        functools.partial(_compact_kernel, cap=cap),
        out_shape=[jax.ShapeDtypeStruct((B, E, 1, cap), jnp.int32), jax.ShapeDtypeStruct((B, E, 1, cap), F32)],
        grid_spec=grid_spec,
        compiler_params=_cparams(("arbitrary",)),
        name="route_compact",
    )(bounds_flat, slot_t, aff_t)
    return slot_t, idx, gate, bounds_flat


def _ffn_kernel(idx_ref, h2_hbm, gate_ref, wg_ref, wu_ref, wd_ref, y_ref, xs_ref, xb_ref, hmid_ref, sem,
                *, cap, tokens, n_batch, n_steps):
    e = pl.program_id(0)
    b = pl.program_id(1)
    f = pl.program_id(2)
    n_exp = pl.num_programs(0)
    nf = pl.num_programs(2)
    rows_per_step = cap // n_steps

    def row_copy(bb, ee, s):
        t = idx_ref[(bb * n_exp + ee) * cap + s]
        return pltpu.make_async_copy(h2_hbm.at[pl.ds(bb * tokens + t, 1)], xs_ref.at[pl.ds(s, 1)], sem)

    def wait_rows():
        pltpu.make_async_copy(h2_hbm.at[pl.ds(0, cap)], xs_ref, sem).wait()

    @pl.when((e == 0) & (b == 0) & (f == 0))
    def _():
        def issue(s, carry):
            row_copy(b, e, s).start()
            return carry

        lax.fori_loop(0, cap, issue, 0, unroll=8)

    @pl.when(f == 0)
    def _():
        wait_rows()
        xb_ref[...] = xs_ref[...].astype(BF16)

    wrap_b = b + 1 == n_batch
    nb = jnp.where(wrap_b, 0, b + 1)
    ne = jnp.where(wrap_b, jnp.where(e + 1 == n_exp, 0, e + 1), e)
    for k in range(rows_per_step):
        row_copy(nb, ne, f * rows_per_step + k).start()

    tf = wg_ref.shape[2]
    wg = wg_ref[0].astype(BF16)
    wu = wu_ref[0].astype(BF16)
    hcols = pl.ds(_mult(f * tf, tf), tf)
    for s in range(FFN_SUB):
        rows = slice(s * (cap // FFN_SUB), (s + 1) * (cap // FFN_SUB))
        xb = xb_ref[rows, :]
        a = jnp.dot(xb, wg, preferred_element_type=F32)
        u = jnp.dot(xb, wu, preferred_element_type=F32)
        hmid_ref[rows, hcols] = (a * (1.0 / (1.0 + jnp.exp(-a))) * u).astype(BF16)

    @pl.when(f == nf - 1)
    def _():
        hmid = hmid_ref[...]
        gate = jnp.broadcast_to(gate_ref[0, 0], (LANES, cap)).T[:, :1]
        for n in range(y_ref.shape[3] // FFN_TN):
            cols = slice(n * FFN_TN, (n + 1) * FFN_TN)
            part = jnp.dot(hmid, wd_ref[0, :, cols].astype(BF16), preferred_element_type=F32)
            y_ref[0, 0, :, cols] = (part * gate).astype(y_ref.dtype)

    @pl.when((e == n_exp - 1) & (b == n_batch - 1) & (f == nf - 1))
    def _():
        wait_rows()


FFN_TF = 256
FFN_TN = 512
FFN_SUB = 4


def _ffn_call(idx_flat, h2, gate, wg, wu, wd, B, T, cap):
    E, D, F = wg.shape
    tf = FFN_TF
    grid_spec = pltpu.PrefetchScalarGridSpec(
        num_scalar_prefetch=1,
        grid=(E, B, F // tf),
        in_specs=[pl.BlockSpec(memory_space=pl.ANY),
                  pl.BlockSpec((1, 1, 1, cap), lambda e, b, f, idx: (b, e, 0, 0)),
                  pl.BlockSpec((1, D, tf), lambda e, b, f, idx: (e, 0, f)),
                  pl.BlockSpec((1, D, tf), lambda e, b, f, idx: (e, 0, f)),
                  pl.BlockSpec((1, F, D), lambda e, b, f, idx: (e, 0, 0))],
        out_specs=pl.BlockSpec((1, 1, cap, D), lambda e, b, f, idx: (b, e, 0, 0)),
        scratch_shapes=[pltpu.VMEM((cap, D), F32), pltpu.VMEM((cap, D), BF16), pltpu.VMEM((cap, F), BF16),
                        pltpu.SemaphoreType.DMA(())],
    )
    return pl.pallas_call(
        functools.partial(_ffn_kernel, cap=cap, tokens=T, n_batch=B, n_steps=F // tf),
        out_shape=jax.ShapeDtypeStruct((B, E, cap, D), BF16),
        grid_spec=grid_spec,
        compiler_params=_cparams(("arbitrary", "arbitrary", "arbitrary")),
        name="expert_ffn",
    )(idx_flat, h2, gate, wg, wu, wd)


CB_WIN = 64
CB_ALIGN = 16


def _combine_kernel(bnd_ref, x1_ref, mod_ref, slot_ref, y_hbm, o_ref, buf_ref, acc_ref, sem,
                    *, cap, nch, n_batch):
    b = pl.program_id(0)
    j = pl.program_id(1)
    E = N_EXPERTS
    step = b * nch + j
    cur = step % 2
    sub = lax.broadcasted_iota(jnp.int32, (CB_WIN, RT_CHUNK), 0).astype(F32)

    def tile_windows(bb, jj):
        per_tile = RT_CHUNK // LANES
        aligned, n_pass = [], jnp.int32(0)
        for e in range(E):
            at = (bb * E + e) * LANES + jj * per_tile
            base = bnd_ref[at]
            cnt = bnd_ref[at + per_tile] - base
            al = (base // CB_ALIGN) * CB_ALIGN
            aligned.append(al)
            n_pass = jnp.maximum(n_pass, (base - al + cnt + CB_WIN - 1) // CB_WIN)
        return aligned, n_pass

    def window(al, c):
        lo = al + c * CB_WIN
        return lo, _mult(jnp.minimum(lo, cap - CB_WIN), CB_ALIGN)

    def start_fetch(bb, aligned, c, slot):
        for e in range(E):
            _, st = window(aligned[e], c)
            pltpu.make_async_copy(y_hbm.at[bb, e, pl.ds(st, CB_WIN)],
                                  buf_ref.at[slot, pl.ds(e * CB_WIN, CB_WIN)], sem.at[slot]).start()

    def wait_fetch(slot):
        pltpu.make_async_copy(buf_ref.at[slot], buf_ref.at[slot], sem.at[slot]).wait()

    def scatter_add(aligned, c, slot):
        pieces = []
        for e in range(E):
            lo, st = window(aligned[e], c)
            srow = slot_ref[0, e:e + 1, :]
            pieces.append(((srow == sub + _f32(st)) & (srow >= _f32(lo))).astype(BF16))
        sel_t = jnp.concatenate(pieces, axis=0)
        return lax.dot_general(sel_t, buf_ref[slot], (((0,), (0,)), ((), ())), preferred_element_type=F32)

    aligned, n_pass = tile_windows(b, j)

    @pl.when(step == 0)
    def _():
        start_fetch(b, aligned, 0, cur)

    @pl.when(step + 1 < n_batch * nch)
    def _():
        wrap = j + 1 == nch
        nb = jnp.where(wrap, b + 1, b)
        nj = jnp.where(wrap, 0, j + 1)
        start_fetch(nb, tile_windows(nb, nj)[0], 0, 1 - cur)

    wait_fetch(cur)
    acc_ref[...] = scatter_add(aligned, 0, cur)

    def extra_pass(c, carry):
        start_fetch(b, aligned, c, cur)
        wait_fetch(cur)
        acc_ref[...] += scatter_add(aligned, c, cur)
        return carry

    lax.fori_loop(1, n_pass, extra_pass, 0)
    o_ref[...] = x1_ref[...] + mod_ref[0, 5:6, :] * acc_ref[...]


def _combine_call(bounds_flat, x1, mod, slot_t, y, B, T, cap):
    D = x1.shape[1]
    nch = T // RT_CHUNK
    E = N_EXPERTS
    grid_spec = pltpu.PrefetchScalarGridSpec(
        num_scalar_prefetch=1,
        grid=(B, nch),
        in_specs=[pl.BlockSpec((RT_CHUNK, D), lambda b, j, bnd: (b * nch + j, 0)),
                  pl.BlockSpec((1, N_MOD, D), lambda b, j, bnd: (b, 0, 0)),
                  pl.BlockSpec((1, E, RT_CHUNK), lambda b, j, bnd: (b, 0, j)),
                  pl.BlockSpec(memory_space=pl.ANY)],
        out_specs=pl.BlockSpec((RT_CHUNK, D), lambda b, j, bnd: (b * nch + j, 0)),
        scratch_shapes=[pltpu.VMEM((2, E * CB_WIN, D), BF16), pltpu.VMEM((RT_CHUNK, D), F32),
                        pltpu.SemaphoreType.DMA((2,))],
    )
    return pl.pallas_call(
        functools.partial(_combine_kernel, cap=cap, nch=nch, n_batch=B),
        out_shape=jax.ShapeDtypeStruct(x1.shape, F32),
        grid_spec=grid_spec,
        compiler_params=_cparams(("arbitrary", "arbitrary")),
        name="combine",
    )(bounds_flat, x1, mod, slot_t, y)


def _rope_tables(n_tokens):
    t = np.arange(n_tokens)
    row = (t // GRID_W).astype(np.float64)
    col = (t % GRID_W).astype(np.float64)
    n_freq = MLA_ROPE // 4
    inv = ROPE_THETA ** (-np.arange(n_freq, dtype=np.float64) / n_freq)
    ang = np.concatenate([row[:, None] * inv, col[:, None] * inv], axis=-1)
    cos, sin = np.cos(ang), np.sin(ang)
    z = np.zeros((n_tokens, LANES - MLA_ROPE))
    return (np.concatenate([cos, cos, z], axis=-1).astype(np.float32),
            np.concatenate([-sin, sin, z], axis=-1).astype(np.float32))


def _pad_lanes(v):
    return jnp.concatenate([v, jnp.zeros((LANES - v.shape[0],), v.dtype)])[None, :]


def kernel(x, c, ctx, c_ctx, w_mod, b_mod, g_norm1, w_in, g_qa, w_qb, g_kva, w_kvb, g_q_mla, g_k_mla,
           g_q_na, g_k_na, rpb_na, g_out_mla, g_out_na, w_out, g_norm2, w_router, w_gate, w_up, w_down):
    B, T, D = x.shape
    C = ctx.shape[1]
    assert w_mod.shape[0] == 1, "single-layer problem"
    assert T % GRID_W == 0 and (T // GRID_W) % NA_QROWS == 0 and T // GRID_W >= 2 * NA_BAND
    assert w_in.shape[2] == 3 * NA_W + MLA_Q_RANK + MLA_KV_RANK + MLA_ROPE
    rows = T // GRID_W
    cap = EC_CAPACITY_FACTOR * T // N_EXPERTS
    half = MLA_ROPE // 2

    wi = w_in[0]
    o_q, o_kv, o_pe, o_na = 0, MLA_Q_RANK, MLA_Q_RANK + MLA_KV_RANK, MLA_Q_RANK + MLA_KV_RANK + MLA_ROPE
    w_in_p = jnp.concatenate(
        [wi[:, o_na:], wi[:, o_q:o_pe], wi[:, o_pe:o_na], wi[:, o_pe + half:o_na], wi[:, o_pe:o_pe + half],
         jnp.zeros((D, P_WIDTH - wi.shape[1] - MLA_ROPE), F32)], axis=1).astype(BF16)
    wq3 = w_qb[0].reshape(MLA_Q_RANK, MLA_HEADS, MLA_QK_DIM)
    wq_p = jnp.concatenate([wq3, wq3[..., MLA_NOPE + half:], wq3[..., MLA_NOPE:MLA_NOPE + half]], axis=-1)
    wq_p = wq_p.reshape(MLA_Q_RANK, MLA_HEADS * MLA_QK_PAD).astype(BF16)
    wkv3 = w_kvb[0].reshape(MLA_KV_RANK, MLA_HEADS, MLA_NOPE + MLA_V)
    wk_p = wkv3[..., :MLA_NOPE].reshape(MLA_KV_RANK, MLA_HEADS * MLA_NOPE).astype(BF16)
    wv_p = wkv3[..., MLA_NOPE:].reshape(MLA_KV_RANK, MLA_HEADS * MLA_V).astype(BF16)

    def rope_gains(g):
        gr = g[MLA_NOPE:]
        return g[None, :MLA_NOPE], _pad_lanes(gr), _pad_lanes(jnp.concatenate([gr[half:], gr[:half]]))

    gq0, gqa, gqb = rope_gains(g_q_mla[0])
    gk0, gka, gkb = rope_gains(g_k_mla[0])
    ct, st = _rope_tables(T)
    consts = dict(ct=ct, st=st, g_qa=g_qa, wq=wq_p, g_kva=g_kva, wk=wk_p, wv=wv_p,
                  gq0=gq0, gqa=gqa, gqb=gqb, gk0=gk0, gka=gka, gkb=gkb, g_q_na=g_q_na, g_k_na=g_k_na)
    wr = jnp.concatenate([w_router[0], jnp.zeros((D, LANES - N_EXPERTS), F32)], axis=1)
    wr_hi = wr.astype(BF16)
    wr_lo = (wr - wr_hi.astype(F32)).astype(BF16)
    bias = _na_bias_table(rpb_na[0], rows)

    cvec = jnp.concatenate([c, c_ctx[None, :], jnp.zeros((8 - B - 1, D), F32)], axis=0)
    mod = _mod_call(cvec.T, w_mod[0], b_mod, B + 1)
    mod_lat = mod[:B].reshape(B, N_MOD, D)
    mod_ctx = mod[B:].reshape(1, N_MOD, D)

    x2 = x.reshape(B * T, D)
    p_lat = _inproj_call(x2, mod_lat, g_norm1, w_in_p, T, min(1024, T))
    p_ctx = _inproj_call(ctx.reshape(B * C, D), mod_ctx, g_norm1, w_in_p, B * C, C)
    qm, km, vm, qn, kn = _prep_call(p_lat, B, T, min(512, T), consts, False)
    kmc, vmc, knc = _prep_call(p_ctx, B, C, C, consts, True)
    o_m = _mla_call(qm, km, vm, kmc, vmc, min(1024, T), min(512, T))
    o_n = _na_call(qn, kn, p_lat, knc, p_ctx, bias)
    x1, h2, aff = _merge_call(o_m.reshape(B * T, -1), o_n.reshape(B * T, -1), x2, mod_lat,
                              g_out_mla, g_out_na, w_out[0].astype(BF16), g_norm2, wr_hi, wr_lo,
                              T, min(512, T))

    slot_t, idx, gate, bounds = _route_call(aff.reshape(B, T, LANES), cap)
    y = _ffn_call(idx.reshape(-1), h2, gate, w_gate[0], w_up[0], w_down[0], B, T, cap)
    out = _combine_call(bounds, x1, mod_lat, slot_t, y, B, T, cap)
    return out.reshape(B, T, D)
```

```python
import functools

import numpy as np
import jax
import jax.numpy as jnp
from jax import lax
from jax.experimental import pallas as pl
from jax.experimental.pallas import tpu as pltpu

F32 = jnp.float32
BF16 = jnp.bfloat16

GRID_W = 64
MLA_HEADS = 8
MLA_NOPE = 128
MLA_ROPE = 64
MLA_QK_DIM = MLA_NOPE + MLA_ROPE
MLA_V = 128
MLA_Q_RANK = 512
MLA_KV_RANK = 256
NA_HEADS = 8
NA_HEAD_DIM = 128
NA_KH = 8
NA_KW = 16
N_EXPERTS = 16
EC_CAPACITY_FACTOR = 2
ROPE_THETA = 10000.0
EPS = 1e-6
N_MOD = 6
MLA_SCALE = MLA_QK_DIM ** -0.5
NA_SCALE = NA_HEAD_DIM ** -0.5
LOG2E = 1.4426950408889634

LANES = 128
MLA_QK_PAD = 2 * LANES
NA_W = NA_HEADS * NA_HEAD_DIM
P_WIDTH = 3 * NA_W + 1024
VMEM_LIMIT = 56 * 1024 * 1024
NEG = -1e30


def _f32(v):
    return jnp.asarray(v, dtype=F32)


def _mult(v, m):
    return v if isinstance(v, int) else pl.multiple_of(v, m)


def _cparams(sem, vmem=VMEM_LIMIT, **kw):
    return pltpu.CompilerParams(dimension_semantics=sem, vmem_limit_bytes=vmem, **kw)


def _mod_kernel(ct_ref, w_ref, b_ref, o_ref):
    ct = ct_ref[...]
    s = ct * (1.0 / (1.0 + jnp.exp(-ct)))
    w = w_ref[...]
    rows = []
    for m in range(o_ref.shape[0]):
        rows.append(jnp.sum(w * s[:, m:m + 1], axis=0, keepdims=True))
    o_ref[...] = jnp.concatenate(rows, axis=0) + b_ref[...]


def _mod_call(ct, w_mod, b_mod, n_rows):
    D, N = w_mod.shape
    tn = 512
    return pl.pallas_call(
        _mod_kernel,
        out_shape=jax.ShapeDtypeStruct((n_rows, N), F32),
        grid=(N // tn,),
        in_specs=[pl.BlockSpec((D, 8), lambda j: (0, 0)),
                  pl.BlockSpec((D, tn), lambda j: (0, j)),
                  pl.BlockSpec((1, tn), lambda j: (0, j))],
        out_specs=pl.BlockSpec((n_rows, tn), lambda j: (0, j)),
        compiler_params=_cparams(("arbitrary",)),
        name="mod",
    )(ct, w_mod, b_mod)


def _win_kernel(w_ref, o_ref):
    w = w_ref[...]
    half = MLA_ROPE // 2
    o_pe = MLA_Q_RANK + MLA_KV_RANK
    o_na = o_pe + MLA_ROPE
    pieces = [w[:, o_na:], w[:, :o_pe], w[:, o_pe:o_na], w[:, o_pe + half:o_na], w[:, o_pe:o_pe + half],
              jnp.zeros((w.shape[0], P_WIDTH - w.shape[1] - MLA_ROPE), F32)]
    o_ref[...] = jnp.concatenate(pieces, axis=1).astype(BF16)


def _win_call(w):
    D, N = w.shape
    tr = 256
    return pl.pallas_call(
        _win_kernel,
        out_shape=jax.ShapeDtypeStruct((D, P_WIDTH), BF16),
        grid=(D // tr,),
        in_specs=[pl.BlockSpec((tr, N), lambda i: (i, 0))],
        out_specs=pl.BlockSpec((tr, P_WIDTH), lambda i: (i, 0)),
        compiler_params=_cparams(("arbitrary",)),
        name="w_in_layout",
    )(w)


INPROJ_SUB = 2
INPROJ_TN = 1024


def _inproj_kernel(x_ref, mod_ref, g_ref, w_ref, o_ref):
    ts = x_ref.shape[0] // INPROJ_SUB
    for s in range(INPROJ_SUB):
        rows = slice(s * ts, (s + 1) * ts)
        x = x_ref[rows, :]
        y = x * lax.rsqrt(jnp.mean(x * x, axis=-1, keepdims=True) + EPS) * g_ref[...]
        h = (y * (1.0 + mod_ref[0, 1:2, :]) + mod_ref[0, 0:1, :]).astype(BF16)
        for n in range(w_ref.shape[1] // INPROJ_TN):
            cols = slice(n * INPROJ_TN, (n + 1) * INPROJ_TN)
            o_ref[rows, cols] = jnp.dot(h, w_ref[:, cols], preferred_element_type=F32).astype(BF16)


def _inproj_call(x2, mod, g, w, rows_per_mod, tm):
    M, D = x2.shape
    N = w.shape[1]
    per = rows_per_mod // tm
    return pl.pallas_call(
        _inproj_kernel,
        out_shape=jax.ShapeDtypeStruct((M, N), BF16),
        grid=(M // tm,),
        in_specs=[pl.BlockSpec((tm, D), lambda i: (i, 0)),
                  pl.BlockSpec((1, N_MOD, D), lambda i: (i // per, 0, 0)),
                  pl.BlockSpec((1, D), lambda i: (0, 0)),
                  pl.BlockSpec((D, N), lambda i: (0, 0), pipeline_mode=pl.Buffered(1))],
        out_specs=pl.BlockSpec((tm, N), lambda i: (i, 0)),
        compiler_params=_cparams(("arbitrary",)),
        name="inproj",
    )(x2, mod, g, w)


def _rms(x, g):
    return x * lax.rsqrt(jnp.mean(x * x, axis=-1, keepdims=True) + EPS) * g


def _prep_kernel(*refs, is_ctx):
    if is_ctx:
        (pk_ref, pm_ref, gkva_ref, wk_ref, wv_ref, gk0_ref, gka_ref, gkn_ref,
         km_ref, vm_ref, kn_ref) = refs
    else:
        (pq_ref, pk_ref, pm_ref, ct_ref, st_ref, gqa_ref, wq_ref, gkva_ref, wk_ref, wv_ref,
         gq0_ref, gqa2_ref, gqb2_ref, gk0_ref, gka_ref, gkb_ref, gqn_ref, gkn_ref,
         qm_ref, km_ref, vm_ref, qn_ref, kn_ref) = refs
    tm = pm_ref.shape[0]
    lane = lax.broadcasted_iota(jnp.int32, (1, LANES), 1)
    rope_mask = (lane < MLA_ROPE).astype(F32)
    pm = pm_ref[...].astype(F32)
    inv_qk = 1.0 / MLA_QK_DIM

    ckvn = _rms(pm[:, MLA_Q_RANK:MLA_Q_RANK + MLA_KV_RANK], gkva_ref[...]).astype(BF16)
    kn = jnp.dot(ckvn, wk_ref[...], preferred_element_type=F32)
    vv = jnp.dot(ckvn, wv_ref[...], preferred_element_type=F32)
    kpe = pm[:, MLA_Q_RANK + MLA_KV_RANK:MLA_Q_RANK + MLA_KV_RANK + LANES]
    ss_pe = jnp.sum(kpe * kpe * rope_mask, axis=-1, keepdims=True)
    if is_ctx:
        k_rope = kpe * gka_ref[...]
    else:
        ct = ct_ref[...]
        st = st_ref[...]
        k_rope = kpe * (ct * gka_ref[...]) + pltpu.roll(kpe, MLA_ROPE, 1) * (st * gkb_ref[...])
    for h in range(MLA_HEADS):
        k0 = kn[:, h * LANES:(h + 1) * LANES]
        r = lax.rsqrt((jnp.sum(k0 * k0, axis=-1, keepdims=True) + ss_pe) * inv_qk + EPS)
        km_ref[0, h] = jnp.concatenate([k0 * r * gk0_ref[...], k_rope * r], axis=-1).astype(BF16)
        vm_ref[0, h] = jnp.concatenate([vv[:, h * LANES:(h + 1) * LANES], jnp.ones((tm, LANES), F32)],
                                       axis=-1).astype(BF16)

    pk = pk_ref[...].astype(F32)
    for h in range(NA_HEADS):
        kn_ref[0, h] = _rms(pk[:, h * LANES:(h + 1) * LANES], gkn_ref[...]).astype(BF16)

    if is_ctx:
        return

    cqn = _rms(pm[:, :MLA_Q_RANK], gqa_ref[...]).astype(BF16)
    q = jnp.dot(cqn, wq_ref[...], preferred_element_type=F32)
    qa = ct * gqa2_ref[...]
    qb = st * gqb2_ref[...]
    for h in range(MLA_HEADS):
        t0 = q[:, h * MLA_QK_PAD:h * MLA_QK_PAD + LANES]
        t1 = q[:, h * MLA_QK_PAD + LANES:(h + 1) * MLA_QK_PAD]
        ss = jnp.sum(t0 * t0 + t1 * t1 * rope_mask, axis=-1, keepdims=True)
        r = lax.rsqrt(ss * inv_qk + EPS) * (MLA_SCALE * LOG2E)
        o1 = t1 * qa + pltpu.roll(t1, MLA_ROPE, 1) * qb
        qm_ref[0, h] = jnp.concatenate([t0 * r * gq0_ref[...], o1 * r], axis=-1).astype(BF16)

    pq = pq_ref[...].astype(F32)
    for h in range(NA_HEADS):
        qn_ref[0, h] = (_rms(pq[:, h * LANES:(h + 1) * LANES], gqn_ref[...]) * (NA_SCALE * LOG2E)).astype(BF16)


def _prep_call(p, nb, tok, tm, consts, is_ctx):
    per = tok // tm
    H = MLA_HEADS
    row = lambda i: (i, 0)
    full = lambda i: (0, 0)
    hm = lambda i: (i // per, 0, i % per, 0)
    p_spec = lambda c: pl.BlockSpec((tm, 1024), lambda i, c=c: (i, c))
    vec = lambda n: pl.BlockSpec((1, n), full)
    km_s = jax.ShapeDtypeStruct((nb, H, tok, MLA_QK_PAD), BF16)
    h128_s = jax.ShapeDtypeStruct((nb, H, tok, LANES), BF16)
    km_o = pl.BlockSpec((1, H, tm, MLA_QK_PAD), hm)
    h128_o = pl.BlockSpec((1, H, tm, LANES), hm)
    c = consts
    if is_ctx:
        ins = [p, p, c["g_kva"], c["wk"], c["wv"], c["gk0"], c["gka"], c["g_k_na"]]
        in_specs = [p_spec(1), p_spec(3), vec(MLA_KV_RANK),
                    pl.BlockSpec(c["wk"].shape, full), pl.BlockSpec(c["wv"].shape, full),
                    vec(LANES), vec(LANES), vec(LANES)]
        out_shape = [km_s, km_s, h128_s]
        out_specs = [km_o, km_o, h128_o]
    else:
        ins = [p, p, p, c["ct"], c["st"], c["g_qa"], c["wq"], c["g_kva"], c["wk"], c["wv"],
               c["gq0"], c["gqa"], c["gqb"], c["gk0"], c["gka"], c["gkb"], c["g_q_na"], c["g_k_na"]]
        tab = pl.BlockSpec((tm, LANES), lambda i: (i % per, 0))
        in_specs = [p_spec(0), p_spec(1), p_spec(3), tab, tab, vec(MLA_Q_RANK),
                    pl.BlockSpec(c["wq"].shape, full), vec(MLA_KV_RANK),
                    pl.BlockSpec(c["wk"].shape, full), pl.BlockSpec(c["wv"].shape, full),
                    vec(LANES), vec(LANES), vec(LANES), vec(LANES), vec(LANES), vec(LANES),
                    vec(LANES), vec(LANES)]
        out_shape = [km_s, km_s, km_s, h128_s, h128_s]
        out_specs = [km_o, km_o, km_o, h128_o, h128_o]
    return pl.pallas_call(
        functools.partial(_prep_kernel, is_ctx=is_ctx),
        out_shape=out_shape,
        grid=(nb * per,),
        in_specs=in_specs,
        out_specs=out_specs,
        compiler_params=_cparams(("arbitrary",)),
        name="prep_ctx" if is_ctx else "prep_lat",
    )(*ins)


MLA_UNROLL = 4
MLA_SUB = 2


def _mla_kernel(q_ref, k_ref, v_ref, kc_ref, vc_ref, o_ref, m_ref, acc_ref,
                sa_ref, sb_ref, sc_ref, mxa_ref, mxb_ref, mxc_ref, *, tq, tk):
    ts = tq // MLA_SUB
    T = k_ref.shape[2]
    n = T // tk
    nq = T // tq
    dn = (((1,), (1,)), ((), ()))

    def scores_into(t, keys, s_ref, mx_ref):
        q = q_ref[0, 0, pl.ds(_mult(t * tq, tq), tq), :]
        s = lax.dot_general(q, keys, dn, preferred_element_type=F32)
        s_ref[...] = s
        mx_ref[...] = jnp.broadcast_to(jnp.max(s, axis=-1, keepdims=True), mx_ref.shape)

    def keys(c):
        return k_ref[0, 0, pl.ds(_mult(c * tk, tk), tk), :]

    def values(c):
        return v_ref[0, 0, pl.ds(_mult(c * tk, tk), tk), :]

    def softmax_pv(s_ref, mx_ref, v):
        for u in range(MLA_SUB):
            rows = slice(u * ts, (u + 1) * ts)
            m_old = m_ref[rows, :]
            m_new = jnp.maximum(m_old, mx_ref[rows, :])
            alpha = jnp.exp2(m_old - m_new)
            p = jnp.concatenate([jnp.exp2(s_ref[rows, j * LANES:(j + 1) * LANES] - m_new)
                                 for j in range(s_ref.shape[1] // LANES)], axis=-1).astype(BF16)
            acc_ref[rows, :] = (jnp.concatenate([alpha] * (acc_ref.shape[1] // LANES), axis=-1) * acc_ref[rows, :]
                                + jnp.dot(p, v, preferred_element_type=F32))
            m_ref[rows, :] = m_new

    scores_into(0, keys(0), sa_ref, mxa_ref)

    def tile_body(t, carry):
        m_ref[...] = jnp.full(m_ref.shape, NEG, F32)
        acc_ref[...] = jnp.zeros(acc_ref.shape, F32)

        bufs = ((sa_ref, mxa_ref), (sb_ref, mxb_ref))

        def group_body(i, c2):
            c0 = MLA_UNROLL * i
            for u in range(MLA_UNROLL):
                scores_into(t, keys(c0 + u + 1), *bufs[(u + 1) % 2])
                softmax_pv(*bufs[u % 2], values(c0 + u))
            return c2

        n_loop = (n - 2) // MLA_UNROLL * MLA_UNROLL
        lax.fori_loop(0, n_loop // MLA_UNROLL, group_body, 0)
        for c in range(n_loop, n):
            if c + 1 < n:
                scores_into(t, keys(c + 1), *bufs[(c + 1) % 2])
            else:
                scores_into(t, kc_ref[0, 0], sc_ref, mxc_ref)
            softmax_pv(*bufs[c % 2], values(c))
        scores_into(jnp.minimum(t + 1, nq - 1), keys(0), sa_ref, mxa_ref)
        softmax_pv(sc_ref, mxc_ref, vc_ref[0, 0])
        o_ref[0, pl.ds(_mult(t * tq, tq), tq), :] = (acc_ref[:, :MLA_V] / acc_ref[:, MLA_V:]).astype(o_ref.dtype)
        return carry

    lax.fori_loop(0, nq, tile_body, 0)


def _mla_call(qm, km, vm, kmc, vmc, tq, tk):
    B, H, T, _ = qm.shape
    C = kmc.shape[2]
    VW = vm.shape[3]
    assert (T // tk) % 2 == 0 and T % tq == 0
    bh = lambda b, h: (b, h, 0, 0)
    return pl.pallas_call(
        functools.partial(_mla_kernel, tq=tq, tk=tk),
        out_shape=jax.ShapeDtypeStruct((B, T, H * MLA_V), BF16),
        grid=(B, H),
        in_specs=[pl.BlockSpec((1, 1, T, MLA_QK_PAD), bh),
                  pl.BlockSpec((1, 1, T, MLA_QK_PAD), bh),
                  pl.BlockSpec((1, 1, T, VW), bh),
                  pl.BlockSpec((1, 1, C, MLA_QK_PAD), bh),
                  pl.BlockSpec((1, 1, C, VW), bh)],
        out_specs=pl.BlockSpec((1, T, MLA_V), lambda b, h: (b, 0, h)),
        scratch_shapes=[pltpu.VMEM((tq, LANES), F32), pltpu.VMEM((tq, VW), F32),
                        pltpu.VMEM((tq, tk), F32), pltpu.VMEM((tq, tk), F32), pltpu.VMEM((tq, C), F32),
                        pltpu.VMEM((tq, LANES), F32), pltpu.VMEM((tq, LANES), F32), pltpu.VMEM((tq, LANES), F32)],
        compiler_params=_cparams(("arbitrary", "arbitrary")),
        name="mla_attn",
    )(qm, km, vm, kmc, vmc)


NA_QROWS = 2
NA_TILE = NA_QROWS * GRID_W
NA_BAND = 5
NA_CFGS = 5


def _na_band_start(i, nblk):
    return jnp.clip(i - 2, 0, nblk - NA_BAND)


def _na_kernel(q_ref, k0, k1, k2, k3, k4, v0, v1, v2, v3, v4, kc_ref, vc_ref, bias_ref, o_ref,
               sl_ref, sx_ref):
    k_refs = (k0, k1, k2, k3, k4)
    v_refs = (v0, v1, v2, v3, v4)
    dn = (((1,), (1,)), ((), ()))

    def scores(h, slot):
        q = q_ref[0, h]
        kcat = jnp.concatenate([r[0, h] for r in k_refs], axis=0)
        sl_ref[slot] = lax.dot_general(q, kcat, dn, preferred_element_type=F32) + bias_ref[0, h]
        sx_ref[slot] = lax.dot_general(q, kc_ref[0, h], dn, preferred_element_type=F32)

    def softmax_pv(h, slot):
        s_loc = sl_ref[slot]
        s_ctx = sx_ref[slot]
        m = jnp.maximum(jnp.max(s_loc, axis=-1, keepdims=True), jnp.max(s_ctx, axis=-1, keepdims=True))
        p_loc = jnp.exp2(s_loc - m).astype(BF16)
        p_ctx = jnp.exp2(s_ctx - m).astype(BF16)
        cols = slice(h * LANES, (h + 1) * LANES)
        vcat = jnp.concatenate([r[:, cols] for r in v_refs], axis=0)
        vcat = jnp.concatenate([vcat, jnp.ones(vcat.shape, BF16)], axis=-1)
        vctx = jnp.concatenate([vc_ref[:, cols], jnp.ones((vc_ref.shape[0], LANES), BF16)], axis=-1)
        o = (jnp.dot(p_loc, vcat, preferred_element_type=F32) + jnp.dot(p_ctx, vctx, preferred_element_type=F32))
        o_ref[0, :, cols] = (o[:, :LANES] / o[:, LANES:]).astype(o_ref.dtype)

    scores(0, 0)
    for h in range(NA_HEADS):
        if h + 1 < NA_HEADS:
            scores(h + 1, (h + 1) % 2)
        softmax_pv(h, h % 2)


def _na_call(qn, kn, p_lat, knc, p_ctx, bias):
    B, H, T, _ = qn.shape
    C = knc.shape[2]
    nblk = T // NA_TILE
    tiles_per_b = T // NA_TILE

    def cfg(i):
        return jnp.minimum(i, 2) + jnp.maximum(i - (nblk - 3), 0)

    k_specs = [pl.BlockSpec((1, H, NA_TILE, LANES),
                            lambda b, i, u=u: (b, 0, _na_band_start(i, nblk) + u, 0)) for u in range(NA_BAND)]
    v_specs = [pl.BlockSpec((NA_TILE, NA_W),
                            lambda b, i, u=u: (b * tiles_per_b + _na_band_start(i, nblk) + u, 2))
               for u in range(NA_BAND)]
    return pl.pallas_call(
        _na_kernel,
        out_shape=jax.ShapeDtypeStruct((B, T, NA_W), BF16),
        grid=(B, nblk),
        in_specs=[pl.BlockSpec((1, H, NA_TILE, LANES), lambda b, i: (b, 0, i, 0))] + k_specs + v_specs + [
            pl.BlockSpec((1, H, C, LANES), lambda b, i: (b, 0, 0, 0)),
            pl.BlockSpec((C, NA_W), lambda b, i: (b, 2)),
            pl.BlockSpec((1, H, NA_TILE, NA_BAND * NA_TILE), lambda b, i: (cfg(i), 0, 0, 0))],
        out_specs=pl.BlockSpec((1, NA_TILE, NA_W), lambda b, i: (b, i, 0)),
        scratch_shapes=[pltpu.VMEM((2, NA_TILE, NA_BAND * NA_TILE), F32), pltpu.VMEM((2, NA_TILE, C), F32)],
        compiler_params=_cparams(("arbitrary", "arbitrary")),
        name="na_attn",
    )(qn, *([kn] * NA_BAND), *([p_lat] * NA_BAND), knc, p_ctx, bias)


def _na_bias_table(rpb, rows):
    nblk = rows // NA_QROWS
    reps = [0, 1, 2, nblk - 2, nblk - 1]
    q_r = np.arange(NA_TILE) // GRID_W
    q_c = np.arange(NA_TILE) % GRID_W
    k_r = np.arange(NA_BAND * NA_TILE) // GRID_W
    k_c = np.arange(NA_BAND * NA_TILE) % GRID_W
    n_ro, n_co = 2 * NA_KH - 1, 2 * NA_KW - 1
    col_sel = (np.arange(n_co)[:, None, None]
               == np.arange(GRID_W)[None, None, :] - np.arange(GRID_W)[None, :, None] + NA_KW - 1)
    toep = jnp.einsum('hrd,dqk->hrqk', rpb, col_sel.astype(np.float32), precision=lax.Precision.HIGHEST)
    toep = jnp.pad(toep, ((0, 0), (1, 1), (0, 0), (0, 0)))
    pair = jnp.concatenate([toep[:, :-1], toep[:, 1:]], axis=-1)
    tables, ok = [], []
    for i in reps:
        j0 = int(np.clip(i - 2, 0, nblk - NA_BAND))
        r = NA_QROWS * i + q_r
        rs = np.clip(r - NA_KH // 2, 0, rows - NA_KH)
        cs = np.clip(q_c - NA_KW // 2, 0, GRID_W - NA_KW)
        kr = NA_QROWS * j0 + k_r
        ok.append((kr[None, :] >= rs[:, None]) & (kr[None, :] < rs[:, None] + NA_KH)
                  & (k_c[None, :] >= cs[:, None]) & (k_c[None, :] < cs[:, None] + NA_KW))
        per_qr = []
        for qr in range(NA_QROWS):
            pieces = []
            for kt in range(NA_BAND):
                ro = NA_QROWS * (j0 - i) + NA_QROWS * kt - qr + NA_KH - 1
                pieces.append(pair[:, int(np.clip(ro + 1, 0, n_ro))])
            per_qr.append(jnp.concatenate(pieces, axis=-1))
        tables.append(jnp.concatenate(per_qr, axis=1))
    return jnp.where(np.stack(ok)[:, None], jnp.stack(tables) * LOG2E, NEG).astype(F32)


def _merge_kernel(om_ref, on_ref, x_ref, mod_ref, gom_ref, gon_ref, wout_ref, g2_ref, wr_ref,
                  x1_ref, h2_ref, aff_ref):
    ts = x_ref.shape[0] // MERGE_SUB
    for s in range(MERGE_SUB):
        rows = slice(s * ts, (s + 1) * ts)
        a = _rms(om_ref[rows, :].astype(F32), gom_ref[...]).astype(BF16)
        b = _rms(on_ref[rows, :].astype(F32), gon_ref[...]).astype(BF16)
        y = jnp.dot(jnp.concatenate([a, b], axis=-1), wout_ref[...], preferred_element_type=F32)
        x1 = x_ref[rows, :] + mod_ref[0, 2:3, :] * y
        x1_ref[rows, :] = x1
        h2 = _rms(x1, g2_ref[...]) * (1.0 + mod_ref[0, 4:5, :]) + mod_ref[0, 3:4, :]
        h2_ref[rows, :] = h2
        hi = h2.astype(BF16)
        lo = (h2 - hi.astype(F32)).astype(BF16)
        logits = (jnp.dot(hi, wr_ref[:, :LANES], preferred_element_type=F32)
                  + jnp.dot(lo, wr_ref[:, :LANES], preferred_element_type=F32)
                  + jnp.dot(hi, wr_ref[:, LANES:], preferred_element_type=F32))
        lane = lax.broadcasted_iota(jnp.int32, logits.shape, 1)
        logits = jnp.where(lane < N_EXPERTS, logits, NEG)
        e = jnp.exp(logits - jnp.max(logits, axis=-1, keepdims=True))
        aff_ref[rows, :] = e / jnp.sum(e, axis=-1, keepdims=True)


MERGE_SUB = 2


def _merge_call(om, on, x2, mod, gom, gon, wout, g2, wr, rows_per_mod, tm):
    M, D = x2.shape
    per = rows_per_mod // tm
    row = lambda i: (i, 0)
    full = lambda i: (0, 0)
    return pl.pallas_call(
        _merge_kernel,
        out_shape=[jax.ShapeDtypeStruct((M, D), F32), jax.ShapeDtypeStruct((M, D), F32),
                   jax.ShapeDtypeStruct((M, LANES), F32)],
        grid=(M // tm,),
        in_specs=[pl.BlockSpec((tm, om.shape[1]), row), pl.BlockSpec((tm, on.shape[1]), row),
                  pl.BlockSpec((tm, D), row),
                  pl.BlockSpec((1, N_MOD, D), lambda i: (i // per, 0, 0)),
                  pl.BlockSpec((1, om.shape[1]), full), pl.BlockSpec((1, on.shape[1]), full),
                  pl.BlockSpec(wout.shape, full), pl.BlockSpec((1, D), full),
                  pl.BlockSpec(wr.shape, full)],
        out_specs=[pl.BlockSpec((tm, D), row), pl.BlockSpec((tm, D), row), pl.BlockSpec((tm, LANES), row)],
        compiler_params=_cparams(("arbitrary",)),
        name="merge_router",
    )(om, on, x2, mod, gom, gon, wout, g2, wr)


RT_CHUNK = 256


def _select_kernel(aff_ref, slot_ref, afft_ref, bounds_ref, *, cap):
    T = aff_ref.shape[1]
    E = N_EXPERTS
    n_tiles = T // LANES
    capf = float(cap)

    def tr_body(c, carry):
        off = _mult(c * RT_CHUNK, RT_CHUNK)
        afft_ref[0, :, pl.ds(off, RT_CHUNK)] = aff_ref[0, pl.ds(off, RT_CHUNK), :].T[:E]
        return carry

    lax.fori_loop(0, T // RT_CHUNK, tr_body, 0)

    def as_float(bits):
        return lax.bitcast_convert_type(bits, F32)

    def count_ge(v):
        return jnp.sum((afft_ref[0] >= v).astype(F32), axis=1, keepdims=True)

    def bs_body(i, prefix):
        cand = prefix | jnp.left_shift(jnp.int32(1), 30 - i)
        return jnp.where(count_ge(as_float(cand)) >= capf, cand, prefix)

    thr_bits = lax.fori_loop(0, 31, bs_body, jnp.zeros((E, 1), jnp.int32))
    thr = as_float(thr_bits)
    above = as_float(jnp.maximum(thr_bits + 1, jnp.int32(0x00800000)))
    need = capf - count_ge(above)

    ri = lax.broadcasted_iota(jnp.int32, (LANES, LANES), 0)
    ci = lax.broadcasted_iota(jnp.int32, (LANES, LANES), 1)
    upper = (ri <= ci).astype(BF16)
    lane = lax.broadcasted_iota(jnp.int32, (E, LANES), 1)
    run_eq = jnp.zeros((E, 1), F32)
    run_sel = jnp.zeros((E, 1), F32)
    bounds = jnp.zeros((E, LANES), F32)
    for k in range(n_tiles):
        a = afft_ref[0, :, k * LANES:(k + 1) * LANES]
        gt = a >= above
        eq = (a >= thr) & (a < above)
        eqf = eq.astype(F32)
        incl_eq = jnp.dot(eqf.astype(BF16), upper, preferred_element_type=F32)
        sel = gt | (eq & ((incl_eq - eqf + run_eq) < need))
        self_ = sel.astype(F32)
        incl_sel = jnp.dot(self_.astype(BF16), upper, preferred_element_type=F32)
        slot_ref[0, :, k * LANES:(k + 1) * LANES] = jnp.where(sel, incl_sel - self_ + run_sel, -1.0)
        bounds = jnp.where(lane == k, run_sel, bounds)
        run_eq = run_eq + incl_eq[:, LANES - 1:LANES]
        run_sel = run_sel + incl_sel[:, LANES - 1:LANES]
    bounds_ref[0] = jnp.where(lane >= n_tiles, run_sel, bounds).astype(jnp.int32)


def _compact_kernel(bnd_ref, slot_ref, afft_ref, idx_ref, gate_ref, acc_i_ref, acc_g_ref, *, cap):
    b = pl.program_id(0)
    T = slot_ref.shape[2]
    n_tiles = T // LANES
    sub = lax.broadcasted_iota(jnp.int32, (LANES, LANES), 0).astype(F32)
    lane_f = lax.broadcasted_iota(jnp.int32, (1, LANES), 1).astype(F32)

    for e in range(N_EXPERTS):
        acc_i_ref[...] = jnp.zeros(acc_i_ref.shape, F32)
        acc_g_ref[...] = jnp.zeros(acc_g_ref.shape, F32)
        row = (b * N_EXPERTS + e) * LANES

        def tile_body(k, carry, e=e, row=row):
            lo = bnd_ref[row + k]
            hi = bnd_ref[row + k + 1]
            off = _mult(k * LANES, LANES)
            srow = slot_ref[0, e:e + 1, pl.ds(off, LANES)]
            arow = afft_ref[0, e:e + 1, pl.ds(off, LANES)]
            trow = lane_f + _f32(off)

            def block_body(sb, c2):
                s0 = _mult(sb * LANES, LANES)
                hit = srow == (sub + _f32(s0))
                acc_i_ref[pl.ds(s0, LANES), :] += jnp.where(hit, trow, 0.0)
                acc_g_ref[pl.ds(s0, LANES), :] += jnp.where(hit, arow, 0.0)
                return c2

            lax.fori_loop(lo // LANES, (hi + LANES - 1) // LANES, block_body, 0)
            return carry

        lax.fori_loop(0, n_tiles, tile_body, 0)
        idx_ref[0, e] = jnp.sum(acc_i_ref[...].T, axis=0, keepdims=True).astype(jnp.int32)
        gate_ref[0, e] = jnp.sum(acc_g_ref[...].T, axis=0, keepdims=True)


def _route_call(aff, cap):
    B, T, _ = aff.shape
    E = N_EXPERTS
    assert T // LANES < LANES
    bet = lambda b: (b, 0, 0)
    slot_t, aff_t, bounds = pl.pallas_call(
        functools.partial(_select_kernel, cap=cap),
        out_shape=[jax.ShapeDtypeStruct((B, E, T), F32), jax.ShapeDtypeStruct((B, E, T), F32),
                   jax.ShapeDtypeStruct((B, E, LANES), jnp.int32)],
        grid=(B,),
        in_specs=[pl.BlockSpec((1, T, LANES), bet)],
        out_specs=[pl.BlockSpec((1, E, T), bet), pl.BlockSpec((1, E, T), bet), pl.BlockSpec((1, E, LANES), bet)],
        compiler_params=_cparams(("arbitrary",)),
        name="route_select",
    )(aff)
    bounds_flat = bounds.reshape(-1)
    grid_spec = pltpu.PrefetchScalarGridSpec(
        num_scalar_prefetch=1,
        grid=(B,),
        in_specs=[pl.BlockSpec((1, E, T), lambda b, bnd: (b, 0, 0)),
                  pl.BlockSpec((1, E, T), lambda b, bnd: (b, 0, 0))],
        out_specs=[pl.BlockSpec((1, E, 1, cap), lambda b, bnd: (b, 0, 0, 0)),
                   pl.BlockSpec((1, E, 1, cap), lambda b, bnd: (b, 0, 0, 0))],
        scratch_shapes=[pltpu.VMEM((cap, LANES), F32), pltpu.VMEM((cap, LANES), F32)],
    )
    idx, gate = pl.pallas_call(
        functools.partial(_compact_kernel, cap=cap),
        out_shape=[jax.ShapeDtypeStruct((B, E, 1, cap), jnp.int32), jax.ShapeDtypeStruct((B, E, 1, cap), F32)],
        grid_spec=grid_spec,
        compiler_params=_cparams(("arbitrary",)),
        name="route_compact",
    )(bounds_flat, slot_t, aff_t)
    return slot_t, idx, gate, bounds_flat


def _ffn_kernel(idx_ref, h2_hbm, gate_ref, wg_ref, wu_ref, wd_ref, y_ref, xs_ref, xb_ref, hmid_ref, sem,
                *, cap, tokens, n_batch, n_steps):
    e = pl.program_id(0)
    b = pl.program_id(1)
    f = pl.program_id(2)
    n_exp = pl.num_programs(0)
    nf = pl.num_programs(2)
    rows_per_step = cap // n_steps

    def row_copy(bb, ee, s):
        t = idx_ref[(bb * n_exp + ee) * cap + s]
        return pltpu.make_async_copy(h2_hbm.at[pl.ds(bb * tokens + t, 1)], xs_ref.at[pl.ds(s, 1)], sem)

    def wait_rows():
        pltpu.make_async_copy(h2_hbm.at[pl.ds(0, cap)], xs_ref, sem).wait()

    @pl.when((e == 0) & (b == 0) & (f == 0))
    def _():
        def issue(s, carry):
            row_copy(b, e, s).start()
            return carry

        lax.fori_loop(0, cap, issue, 0, unroll=8)

    @pl.when(f == 0)
    def _():
        wait_rows()
        xb_ref[...] = xs_ref[...].astype(BF16)

    wrap_b = b + 1 == n_batch
    nb = jnp.where(wrap_b, 0, b + 1)
    ne = jnp.where(wrap_b, jnp.where(e + 1 == n_exp, 0, e + 1), e)
    for k in range(rows_per_step):
        row_copy(nb, ne, f * rows_per_step + k).start()

    tf = wg_ref.shape[2]
    wg = wg_ref[0].astype(BF16)
    wu = wu_ref[0].astype(BF16)
    hcols = pl.ds(_mult(f * tf, tf), tf)
    for s in range(FFN_SUB):
        rows = slice(s * (cap // FFN_SUB), (s + 1) * (cap // FFN_SUB))
        xb = xb_ref[rows, :]
        a = jnp.dot(xb, wg, preferred_element_type=F32)
        u = jnp.dot(xb, wu, preferred_element_type=F32)
        hmid_ref[rows, hcols] = (a * (1.0 / (1.0 + jnp.exp(-a))) * u).astype(BF16)

    @pl.when(f == nf - 1)
    def _():
        hmid = hmid_ref[...]
        gate = jnp.broadcast_to(gate_ref[0, 0], (LANES, cap)).T[:, :1]
        for n in range(y_ref.shape[3] // FFN_TN):
            cols = slice(n * FFN_TN, (n + 1) * FFN_TN)
            part = jnp.dot(hmid, wd_ref[0, :, cols].astype(BF16), preferred_element_type=F32)
            y_ref[0, 0, :, cols] = (part * gate).astype(y_ref.dtype)

    @pl.when((e == n_exp - 1) & (b == n_batch - 1) & (f == nf - 1))
    def _():
        wait_rows()


FFN_TF = 256
FFN_TN = 512
FFN_SUB = 4


def _ffn_call(idx_flat, h2, gate, wg, wu, wd, B, T, cap):
    E, D, F = wg.shape
    tf = FFN_TF
    grid_spec = pltpu.PrefetchScalarGridSpec(
        num_scalar_prefetch=1,
        grid=(E, B, F // tf),
        in_specs=[pl.BlockSpec(memory_space=pl.ANY),
                  pl.BlockSpec((1, 1, 1, cap), lambda e, b, f, idx: (b, e, 0, 0)),
                  pl.BlockSpec((1, D, tf), lambda e, b, f, idx: (e, 0, f)),
                  pl.BlockSpec((1, D, tf), lambda e, b, f, idx: (e, 0, f)),
                  pl.BlockSpec((1, F, D), lambda e, b, f, idx: (e, 0, 0))],
        out_specs=pl.BlockSpec((1, 1, cap, D), lambda e, b, f, idx: (b, e, 0, 0)),
        scratch_shapes=[pltpu.VMEM((cap, D), F32), pltpu.VMEM((cap, D), BF16), pltpu.VMEM((cap, F), BF16),
                        pltpu.SemaphoreType.DMA(())],
    )
    return pl.pallas_call(
        functools.partial(_ffn_kernel, cap=cap, tokens=T, n_batch=B, n_steps=F // tf),
        out_shape=jax.ShapeDtypeStruct((B, E, cap, D), BF16),
        grid_spec=grid_spec,
        compiler_params=_cparams(("arbitrary", "arbitrary", "arbitrary")),
        name="expert_ffn",
    )(idx_flat, h2, gate, wg, wu, wd)


CB_WIN = 64
CB_ALIGN = 16


def _combine_kernel(bnd_ref, x1_ref, mod_ref, slot_ref, y_hbm, o_ref, buf_ref, acc_ref, sem,
                    *, cap, nch, n_batch):
    b = pl.program_id(0)
    j = pl.program_id(1)
    E = N_EXPERTS
    step = b * nch + j
    cur = step % 2
    sub = lax.broadcasted_iota(jnp.int32, (CB_WIN, RT_CHUNK), 0).astype(F32)

    def tile_windows(bb, jj):
        per_tile = RT_CHUNK // LANES
        aligned, n_pass = [], jnp.int32(0)
        for e in range(E):
            at = (bb * E + e) * LANES + jj * per_tile
            base = bnd_ref[at]
            cnt = bnd_ref[at + per_tile] - base
            al = (base // CB_ALIGN) * CB_ALIGN
            aligned.append(al)
            n_pass = jnp.maximum(n_pass, (base - al + cnt + CB_WIN - 1) // CB_WIN)
        return aligned, n_pass

    def window(al, c):
        lo = al + c * CB_WIN
        return lo, _mult(jnp.minimum(lo, cap - CB_WIN), CB_ALIGN)

    def start_fetch(bb, aligned, c, slot):
        for e in range(E):
            _, st = window(aligned[e], c)
            pltpu.make_async_copy(y_hbm.at[bb, e, pl.ds(st, CB_WIN)],
                                  buf_ref.at[slot, pl.ds(e * CB_WIN, CB_WIN)], sem.at[slot]).start()

    def wait_fetch(slot):
        pltpu.make_async_copy(buf_ref.at[slot], buf_ref.at[slot], sem.at[slot]).wait()

    def scatter_add(aligned, c, slot):
        pieces = []
        for e in range(E):
            lo, st = window(aligned[e], c)
            srow = slot_ref[0, e:e + 1, :]
            pieces.append(((srow == sub + _f32(st)) & (srow >= _f32(lo))).astype(BF16))
        sel_t = jnp.concatenate(pieces, axis=0)
        return lax.dot_general(sel_t, buf_ref[slot], (((0,), (0,)), ((), ())), preferred_element_type=F32)

    aligned, n_pass = tile_windows(b, j)

    @pl.when(step == 0)
    def _():
        start_fetch(b, aligned, 0, cur)

    @pl.when(step + 1 < n_batch * nch)
    def _():
        wrap = j + 1 == nch
        nb = jnp.where(wrap, b + 1, b)
        nj = jnp.where(wrap, 0, j + 1)
        start_fetch(nb, tile_windows(nb, nj)[0], 0, 1 - cur)

    wait_fetch(cur)
    acc_ref[...] = scatter_add(aligned, 0, cur)

    def extra_pass(c, carry):
        start_fetch(b, aligned, c, cur)
        wait_fetch(cur)
        acc_ref[...] += scatter_add(aligned, c, cur)
        return carry

    lax.fori_loop(1, n_pass, extra_pass, 0)
    o_ref[...] = x1_ref[...] + mod_ref[0, 5:6, :] * acc_ref[...]


def _combine_call(bounds_flat, x1, mod, slot_t, y, B, T, cap):
    D = x1.shape[1]
    nch = T // RT_CHUNK
    E = N_EXPERTS
    grid_spec = pltpu.PrefetchScalarGridSpec(
        num_scalar_prefetch=1,
        grid=(B, nch),
        in_specs=[pl.BlockSpec((RT_CHUNK, D), lambda b, j, bnd: (b * nch + j, 0)),
                  pl.BlockSpec((1, N_MOD, D), lambda b, j, bnd: (b, 0, 0)),
                  pl.BlockSpec((1, E, RT_CHUNK), lambda b, j, bnd: (b, 0, j)),
                  pl.BlockSpec(memory_space=pl.ANY)],
        out_specs=pl.BlockSpec((RT_CHUNK, D), lambda b, j, bnd: (b * nch + j, 0)),
        scratch_shapes=[pltpu.VMEM((2, E * CB_WIN, D), BF16), pltpu.VMEM((RT_CHUNK, D), F32),
                        pltpu.SemaphoreType.DMA((2,))],
    )
    return pl.pallas_call(
        functools.partial(_combine_kernel, cap=cap, nch=nch, n_batch=B),
        out_shape=jax.ShapeDtypeStruct(x1.shape, F32),
        grid_spec=grid_spec,
        compiler_params=_cparams(("arbitrary", "arbitrary")),
        name="combine",
    )(bounds_flat, x1, mod, slot_t, y)


def _rope_tables(n_tokens):
    t = np.arange(n_tokens)
    row = (t // GRID_W).astype(np.float64)
    col = (t % GRID_W).astype(np.float64)
    n_freq = MLA_ROPE // 4
    inv = ROPE_THETA ** (-np.arange(n_freq, dtype=np.float64) / n_freq)
    ang = np.concatenate([row[:, None] * inv, col[:, None] * inv], axis=-1)
    cos, sin = np.cos(ang), np.sin(ang)
    z = np.zeros((n_tokens, LANES - MLA_ROPE))
    return (np.concatenate([cos, cos, z], axis=-1).astype(np.float32),
            np.concatenate([-sin, sin, z], axis=-1).astype(np.float32))


def _pad_lanes(v):
    return jnp.concatenate([v, jnp.zeros((LANES - v.shape[0],), v.dtype)])[None, :]


def kernel(x, c, ctx, c_ctx, w_mod, b_mod, g_norm1, w_in, g_qa, w_qb, g_kva, w_kvb, g_q_mla, g_k_mla,
           g_q_na, g_k_na, rpb_na, g_out_mla, g_out_na, w_out, g_norm2, w_router, w_gate, w_up, w_down):
    B, T, D = x.shape
    C = ctx.shape[1]
    assert w_mod.shape[0] == 1, "single-layer problem"
    assert T % GRID_W == 0 and (T // GRID_W) % NA_QROWS == 0 and T // GRID_W >= 2 * NA_BAND
    assert w_in.shape[2] == 3 * NA_W + MLA_Q_RANK + MLA_KV_RANK + MLA_ROPE
    rows = T // GRID_W
    cap = EC_CAPACITY_FACTOR * T // N_EXPERTS
    half = MLA_ROPE // 2

    w_in_p = _win_call(w_in[0])
    wq3 = w_qb[0].reshape(MLA_Q_RANK, MLA_HEADS, MLA_QK_DIM)
    wq_p = jnp.concatenate([wq3, wq3[..., MLA_NOPE + half:], wq3[..., MLA_NOPE:MLA_NOPE + half]], axis=-1)
    wq_p = wq_p.reshape(MLA_Q_RANK, MLA_HEADS * MLA_QK_PAD).astype(BF16)
    wkv3 = w_kvb[0].reshape(MLA_KV_RANK, MLA_HEADS, MLA_NOPE + MLA_V)
    wk_p = wkv3[..., :MLA_NOPE].reshape(MLA_KV_RANK, MLA_HEADS * MLA_NOPE).astype(BF16)
    wv_p = wkv3[..., MLA_NOPE:].reshape(MLA_KV_RANK, MLA_HEADS * MLA_V).astype(BF16)

    def rope_gains(g):
        gr = g[MLA_NOPE:]
        return g[None, :MLA_NOPE], _pad_lanes(gr), _pad_lanes(jnp.concatenate([gr[half:], gr[:half]]))

    gq0, gqa, gqb = rope_gains(g_q_mla[0])
    gk0, gka, gkb = rope_gains(g_k_mla[0])
    ct, st = _rope_tables(T)
    consts = dict(ct=ct, st=st, g_qa=g_qa, wq=wq_p, g_kva=g_kva, wk=wk_p, wv=wv_p,
                  gq0=gq0, gqa=gqa, gqb=gqb, gk0=gk0, gka=gka, gkb=gkb, g_q_na=g_q_na, g_k_na=g_k_na)
    wr = jnp.concatenate([w_router[0], jnp.zeros((D, LANES - N_EXPERTS), F32)], axis=1)
    wr_hi = wr.astype(BF16)
    wr_split = jnp.concatenate([wr_hi, (wr - wr_hi.astype(F32)).astype(BF16)], axis=1)
    bias = _na_bias_table(rpb_na[0], rows)

    cvec = jnp.concatenate([c, c_ctx[None, :], jnp.zeros((8 - B - 1, D), F32)], axis=0)
    mod = _mod_call(cvec.T, w_mod[0], b_mod, B + 1)
    mod_lat = mod[:B].reshape(B, N_MOD, D)
    mod_ctx = mod[B:].reshape(1, N_MOD, D)

    x2 = x.reshape(B * T, D)
    p_lat = _inproj_call(x2, mod_lat, g_norm1, w_in_p, T, min(512, T))
    p_ctx = _inproj_call(ctx.reshape(B * C, D), mod_ctx, g_norm1, w_in_p, B * C, C)
    qm, km, vm, qn, kn = _prep_call(p_lat, B, T, min(512, T), consts, False)
    kmc, vmc, knc = _prep_call(p_ctx, B, C, C, consts, True)
    o_m = _mla_call(qm, km, vm, kmc, vmc, min(1024, T), min(512, T))
    o_n = _na_call(qn, kn, p_lat, knc, p_ctx, bias)
    x1, h2, aff = _merge_call(o_m.reshape(B * T, -1), o_n.reshape(B * T, -1), x2, mod_lat,
                              g_out_mla, g_out_na, w_out[0].astype(BF16), g_norm2, wr_split,
                              T, min(512, T))

    slot_t, idx, gate, bounds = _route_call(aff.reshape(B, T, LANES), cap)
    y = _ffn_call(idx.reshape(-1), h2, gate, w_gate[0], w_up[0], w_down[0], B, T, cap)
    out = _combine_call(bounds, x1, mod_lat, slot_t, y, B, T, cap)
    return out.reshape(B, T, D)
```

```python
import functools

import numpy as np
import jax
import jax.numpy as jnp
from jax import lax
from jax.experimental import pallas as pl
from jax.experimental.pallas import tpu as pltpu

F32 = jnp.float32
BF16 = jnp.bfloat16

GRID_W = 64
MLA_HEADS = 8
MLA_NOPE = 128
MLA_ROPE = 64
MLA_QK_DIM = MLA_NOPE + MLA_ROPE
MLA_V = 128
MLA_Q_RANK = 512
MLA_KV_RANK = 256
NA_HEADS = 8
NA_HEAD_DIM = 128
NA_KH = 8
NA_KW = 16
N_EXPERTS = 16
EC_CAPACITY_FACTOR = 2
ROPE_THETA = 10000.0
EPS = 1e-6
N_MOD = 6
MLA_SCALE = MLA_QK_DIM ** -0.5
NA_SCALE = NA_HEAD_DIM ** -0.5
LOG2E = 1.4426950408889634

LANES = 128
MLA_QK_PAD = 2 * LANES
NA_W = NA_HEADS * NA_HEAD_DIM
P_WIDTH = 3 * NA_W + 1024
VMEM_LIMIT = 56 * 1024 * 1024
NEG = -1e30


def _f32(v):
    return jnp.asarray(v, dtype=F32)


def _mult(v, m):
    return v if isinstance(v, int) else pl.multiple_of(v, m)


def _cparams(sem, vmem=VMEM_LIMIT, **kw):
    return pltpu.CompilerParams(dimension_semantics=sem, vmem_limit_bytes=vmem, **kw)


def _mod_kernel(ct_ref, w_ref, b_ref, o_ref):
    ct = ct_ref[...]
    s = ct * (1.0 / (1.0 + jnp.exp(-ct)))
    w = w_ref[...]
    rows = []
    for m in range(o_ref.shape[0]):
        rows.append(jnp.sum(w * s[:, m:m + 1], axis=0, keepdims=True))
    o_ref[...] = jnp.concatenate(rows, axis=0) + b_ref[...]


def _mod_call(ct, w_mod, b_mod, n_rows):
    D, N = w_mod.shape
    tn = 512
    return pl.pallas_call(
        _mod_kernel,
        out_shape=jax.ShapeDtypeStruct((n_rows, N), F32),
        grid=(N // tn,),
        in_specs=[pl.BlockSpec((D, 8), lambda j: (0, 0)),
                  pl.BlockSpec((D, tn), lambda j: (0, j)),
                  pl.BlockSpec((1, tn), lambda j: (0, j))],
        out_specs=pl.BlockSpec((n_rows, tn), lambda j: (0, j)),
        compiler_params=_cparams(("arbitrary",)),
        name="mod",
    )(ct, w_mod, b_mod)


def _win_kernel(w_ref, o_ref):
    w = w_ref[...]
    half = MLA_ROPE // 2
    o_pe = MLA_Q_RANK + MLA_KV_RANK
    o_na = o_pe + MLA_ROPE
    pieces = [w[:, o_na:], w[:, :o_pe], w[:, o_pe:o_na], w[:, o_pe + half:o_na], w[:, o_pe:o_pe + half],
              jnp.zeros((w.shape[0], P_WIDTH - w.shape[1] - MLA_ROPE), F32)]
    o_ref[...] = jnp.concatenate(pieces, axis=1).astype(BF16)


def _win_call(w):
    D, N = w.shape
    tr = 256
    return pl.pallas_call(
        _win_kernel,
        out_shape=jax.ShapeDtypeStruct((D, P_WIDTH), BF16),
        grid=(D // tr,),
        in_specs=[pl.BlockSpec((tr, N), lambda i: (i, 0))],
        out_specs=pl.BlockSpec((tr, P_WIDTH), lambda i: (i, 0)),
        compiler_params=_cparams(("arbitrary",)),
        name="w_in_layout",
    )(w)


INPROJ_SUB = 2
INPROJ_TN = 1024


def _inproj_kernel(x_ref, mod_ref, g_ref, w_ref, o_ref):
    ts = x_ref.shape[0] // INPROJ_SUB
    for s in range(INPROJ_SUB):
        rows = slice(s * ts, (s + 1) * ts)
        x = x_ref[rows, :]
        y = x * lax.rsqrt(jnp.mean(x * x, axis=-1, keepdims=True) + EPS) * g_ref[...]
        h = (y * (1.0 + mod_ref[0, 1:2, :]) + mod_ref[0, 0:1, :]).astype(BF16)
        for n in range(w_ref.shape[1] // INPROJ_TN):
            cols = slice(n * INPROJ_TN, (n + 1) * INPROJ_TN)
            o_ref[rows, cols] = jnp.dot(h, w_ref[:, cols], preferred_element_type=F32).astype(BF16)


def _inproj_call(x2, mod, g, w, rows_per_mod, tm):
    M, D = x2.shape
    N = w.shape[1]
    per = rows_per_mod // tm
    return pl.pallas_call(
        _inproj_kernel,
        out_shape=jax.ShapeDtypeStruct((M, N), BF16),
        grid=(M // tm,),
        in_specs=[pl.BlockSpec((tm, D), lambda i: (i, 0)),
                  pl.BlockSpec((1, N_MOD, D), lambda i: (i // per, 0, 0)),
                  pl.BlockSpec((1, D), lambda i: (0, 0)),
                  pl.BlockSpec((D, N), lambda i: (0, 0), pipeline_mode=pl.Buffered(1))],
        out_specs=pl.BlockSpec((tm, N), lambda i: (i, 0)),
        compiler_params=_cparams(("arbitrary",)),
        name="inproj",
    )(x2, mod, g, w)


def _rms(x, g):
    return x * lax.rsqrt(jnp.mean(x * x, axis=-1, keepdims=True) + EPS) * g


def _prep_kernel(*refs, is_ctx):
    if is_ctx:
        (pk_ref, pm_ref, gkva_ref, wk_ref, wv_ref, gk0_ref, gka_ref, gkn_ref,
         km_ref, vm_ref, kn_ref) = refs
    else:
        (pq_ref, pk_ref, pm_ref, ct_ref, st_ref, gqa_ref, wq_ref, gkva_ref, wk_ref, wv_ref,
         gq0_ref, gqa2_ref, gqb2_ref, gk0_ref, gka_ref, gkb_ref, gqn_ref, gkn_ref,
         qm_ref, km_ref, vm_ref, qn_ref, kn_ref) = refs
    tm = pm_ref.shape[0]
    lane = lax.broadcasted_iota(jnp.int32, (1, LANES), 1)
    rope_mask = (lane < MLA_ROPE).astype(F32)
    pm = pm_ref[...].astype(F32)
    inv_qk = 1.0 / MLA_QK_DIM

    ckvn = _rms(pm[:, MLA_Q_RANK:MLA_Q_RANK + MLA_KV_RANK], gkva_ref[...]).astype(BF16)
    kn = jnp.dot(ckvn, wk_ref[...], preferred_element_type=F32)
    vv = jnp.dot(ckvn, wv_ref[...], preferred_element_type=F32)
    kpe = pm[:, MLA_Q_RANK + MLA_KV_RANK:MLA_Q_RANK + MLA_KV_RANK + LANES]
    ss_pe = jnp.sum(kpe * kpe * rope_mask, axis=-1, keepdims=True)
    if is_ctx:
        k_rope = kpe * gka_ref[...]
    else:
        ct = ct_ref[...]
        st = st_ref[...]
        k_rope = kpe * (ct * gka_ref[...]) + pltpu.roll(kpe, MLA_ROPE, 1) * (st * gkb_ref[...])
    for h in range(MLA_HEADS):
        k0 = kn[:, h * LANES:(h + 1) * LANES]
        r = lax.rsqrt((jnp.sum(k0 * k0, axis=-1, keepdims=True) + ss_pe) * inv_qk + EPS)
        km_ref[0, h] = jnp.concatenate([k0 * r * gk0_ref[...], k_rope * r], axis=-1).astype(BF16)
        vm_ref[0, h] = jnp.concatenate([vv[:, h * LANES:(h + 1) * LANES], jnp.ones((tm, LANES), F32)],
                                       axis=-1).astype(BF16)

    pk = pk_ref[...].astype(F32)
    for h in range(NA_HEADS):
        kn_ref[0, h] = _rms(pk[:, h * LANES:(h + 1) * LANES], gkn_ref[...]).astype(BF16)

    if is_ctx:
        return

    cqn = _rms(pm[:, :MLA_Q_RANK], gqa_ref[...]).astype(BF16)
    q = jnp.dot(cqn, wq_ref[...], preferred_element_type=F32)
    qa = ct * gqa2_ref[...]
    qb = st * gqb2_ref[...]
    for h in range(MLA_HEADS):
        t0 = q[:, h * MLA_QK_PAD:h * MLA_QK_PAD + LANES]
        t1 = q[:, h * MLA_QK_PAD + LANES:(h + 1) * MLA_QK_PAD]
        ss = jnp.sum(t0 * t0 + t1 * t1 * rope_mask, axis=-1, keepdims=True)
        r = lax.rsqrt(ss * inv_qk + EPS) * (MLA_SCALE * LOG2E)
        o1 = t1 * qa + pltpu.roll(t1, MLA_ROPE, 1) * qb
        qm_ref[0, h] = jnp.concatenate([t0 * r * gq0_ref[...], o1 * r], axis=-1).astype(BF16)

    pq = pq_ref[...].astype(F32)
    for h in range(NA_HEADS):
        qn_ref[0, h] = (_rms(pq[:, h * LANES:(h + 1) * LANES], gqn_ref[...]) * (NA_SCALE * LOG2E)).astype(BF16)


def _prep_call(p, nb, tok, tm, consts, is_ctx):
    per = tok // tm
    H = MLA_HEADS
    row = lambda i: (i, 0)
    full = lambda i: (0, 0)
    hm = lambda i: (i // per, 0, i % per, 0)
    p_spec = lambda c: pl.BlockSpec((tm, 1024), lambda i, c=c: (i, c))
    vec = lambda n: pl.BlockSpec((1, n), full)
    km_s = jax.ShapeDtypeStruct((nb, H, tok, MLA_QK_PAD), BF16)
    h128_s = jax.ShapeDtypeStruct((nb, H, tok, LANES), BF16)
    km_o = pl.BlockSpec((1, H, tm, MLA_QK_PAD), hm)
    h128_o = pl.BlockSpec((1, H, tm, LANES), hm)
    c = consts
    if is_ctx:
        ins = [p, p, c["g_kva"], c["wk"], c["wv"], c["gk0"], c["gka"], c["g_k_na"]]
        in_specs = [p_spec(1), p_spec(3), vec(MLA_KV_RANK),
                    pl.BlockSpec(c["wk"].shape, full), pl.BlockSpec(c["wv"].shape, full),
                    vec(LANES), vec(LANES), vec(LANES)]
        out_shape = [km_s, km_s, h128_s]
        out_specs = [km_o, km_o, h128_o]
    else:
        ins = [p, p, p, c["ct"], c["st"], c["g_qa"], c["wq"], c["g_kva"], c["wk"], c["wv"],
               c["gq0"], c["gqa"], c["gqb"], c["gk0"], c["gka"], c["gkb"], c["g_q_na"], c["g_k_na"]]
        tab = pl.BlockSpec((tm, LANES), lambda i: (i % per, 0))
        in_specs = [p_spec(0), p_spec(1), p_spec(3), tab, tab, vec(MLA_Q_RANK),
                    pl.BlockSpec(c["wq"].shape, full), vec(MLA_KV_RANK),
                    pl.BlockSpec(c["wk"].shape, full), pl.BlockSpec(c["wv"].shape, full),
                    vec(LANES), vec(LANES), vec(LANES), vec(LANES), vec(LANES), vec(LANES),
                    vec(LANES), vec(LANES)]
        out_shape = [km_s, km_s, km_s, h128_s, h128_s]
        out_specs = [km_o, km_o, km_o, h128_o, h128_o]
    return pl.pallas_call(
        functools.partial(_prep_kernel, is_ctx=is_ctx),
        out_shape=out_shape,
        grid=(nb * per,),
        in_specs=in_specs,
        out_specs=out_specs,
        compiler_params=_cparams(("arbitrary",)),
        name="prep_ctx" if is_ctx else "prep_lat",
    )(*ins)


MLA_UNROLL = 2
MLA_SUB = 2


def _mla_kernel(q_ref, k_ref, v_ref, kc_ref, vc_ref, o_ref, m_ref, acc_ref,
                sa_ref, sb_ref, sc_ref, mxa_ref, mxb_ref, mxc_ref, *, tq, tk):
    ts = tq // MLA_SUB
    T = k_ref.shape[2]
    n = T // tk
    nq = T // tq
    dn = (((1,), (1,)), ((), ()))

    def scores_into(t, keys, s_ref, mx_ref):
        q = q_ref[0, 0, pl.ds(_mult(t * tq, tq), tq), :]
        s = lax.dot_general(q, keys, dn, preferred_element_type=F32)
        s_ref[...] = s
        mx_ref[...] = jnp.broadcast_to(jnp.max(s, axis=-1, keepdims=True), mx_ref.shape)

    def keys(c):
        return k_ref[0, 0, pl.ds(_mult(c * tk, tk), tk), :]

    def values(c):
        return v_ref[0, 0, pl.ds(_mult(c * tk, tk), tk), :]

    def softmax_pv(s_ref, mx_ref, v):
        for u in range(MLA_SUB):
            rows = slice(u * ts, (u + 1) * ts)
            m_old = m_ref[rows, :]
            m_new = jnp.maximum(m_old, mx_ref[rows, :])
            alpha = jnp.exp2(m_old - m_new)
            p = jnp.concatenate([jnp.exp2(s_ref[rows, j * LANES:(j + 1) * LANES] - m_new)
                                 for j in range(s_ref.shape[1] // LANES)], axis=-1).astype(BF16)
            acc_ref[rows, :] = (jnp.concatenate([alpha] * (acc_ref.shape[1] // LANES), axis=-1) * acc_ref[rows, :]
                                + jnp.dot(p, v, preferred_element_type=F32))
            m_ref[rows, :] = m_new

    scores_into(0, keys(0), sa_ref, mxa_ref)

    def tile_body(t, carry):
        m_ref[...] = jnp.full(m_ref.shape, NEG, F32)
        acc_ref[...] = jnp.zeros(acc_ref.shape, F32)

        bufs = ((sa_ref, mxa_ref), (sb_ref, mxb_ref))

        def group_body(i, c2):
            c0 = MLA_UNROLL * i
            for u in range(MLA_UNROLL):
                scores_into(t, keys(c0 + u + 1), *bufs[(u + 1) % 2])
                softmax_pv(*bufs[u % 2], values(c0 + u))
            return c2

        n_loop = (n - 2) // MLA_UNROLL * MLA_UNROLL
        lax.fori_loop(0, n_loop // MLA_UNROLL, group_body, 0)
        for c in range(n_loop, n):
            if c + 1 < n:
                scores_into(t, keys(c + 1), *bufs[(c + 1) % 2])
            else:
                scores_into(t, kc_ref[0, 0], sc_ref, mxc_ref)
            softmax_pv(*bufs[c % 2], values(c))
        scores_into(jnp.minimum(t + 1, nq - 1), keys(0), sa_ref, mxa_ref)
        softmax_pv(sc_ref, mxc_ref, vc_ref[0, 0])
        o_ref[0, pl.ds(_mult(t * tq, tq), tq), :] = (acc_ref[:, :MLA_V] / acc_ref[:, MLA_V:]).astype(o_ref.dtype)
        return carry

    lax.fori_loop(0, nq, tile_body, 0)


def _mla_call(qm, km, vm, kmc, vmc, tq, tk):
    B, H, T, _ = qm.shape
    C = kmc.shape[2]
    VW = vm.shape[3]
    assert (T // tk) % 2 == 0 and T % tq == 0
    bh = lambda b, h: (b, h, 0, 0)
    return pl.pallas_call(
        functools.partial(_mla_kernel, tq=tq, tk=tk),
        out_shape=jax.ShapeDtypeStruct((B, T, H * MLA_V), BF16),
        grid=(B, H),
        in_specs=[pl.BlockSpec((1, 1, T, MLA_QK_PAD), bh),
                  pl.BlockSpec((1, 1, T, MLA_QK_PAD), bh),
                  pl.BlockSpec((1, 1, T, VW), bh),
                  pl.BlockSpec((1, 1, C, MLA_QK_PAD), bh),
                  pl.BlockSpec((1, 1, C, VW), bh)],
        out_specs=pl.BlockSpec((1, T, MLA_V), lambda b, h: (b, 0, h)),
        scratch_shapes=[pltpu.VMEM((tq, LANES), F32), pltpu.VMEM((tq, VW), F32),
                        pltpu.VMEM((tq, tk), F32), pltpu.VMEM((tq, tk), F32), pltpu.VMEM((tq, C), F32),
                        pltpu.VMEM((tq, LANES), F32), pltpu.VMEM((tq, LANES), F32), pltpu.VMEM((tq, LANES), F32)],
        compiler_params=_cparams(("arbitrary", "arbitrary")),
        name="mla_attn",
    )(qm, km, vm, kmc, vmc)


NA_QROWS = 4
NA_KROWS = 2
NA_QTILE = NA_QROWS * GRID_W
NA_KTILE = NA_KROWS * GRID_W
NA_BAND = 6
NA_CFGS = 3


def _na_band_start(i, nkt):
    return jnp.clip((NA_QROWS // NA_KROWS) * i - NA_KH // 2 // NA_KROWS, 0, nkt - NA_BAND)


def _na_kernel(q_ref, *refs):
    k_refs = refs[:NA_BAND]
    v_refs = refs[NA_BAND:2 * NA_BAND]
    kc_ref, vc_ref, bias_ref, o_ref, sl_ref, sx_ref = refs[2 * NA_BAND:]
    dn = (((1,), (1,)), ((), ()))

    def scores(h, slot):
        q = q_ref[0, h]
        kcat = jnp.concatenate([r[0, h] for r in k_refs], axis=0)
        sl_ref[slot] = lax.dot_general(q, kcat, dn, preferred_element_type=F32) + bias_ref[0, h]
        sx_ref[slot] = lax.dot_general(q, kc_ref[0, h], dn, preferred_element_type=F32)

    def softmax_pv(h, slot):
        s_loc = sl_ref[slot]
        s_ctx = sx_ref[slot]
        m = jnp.maximum(jnp.max(s_loc, axis=-1, keepdims=True), jnp.max(s_ctx, axis=-1, keepdims=True))
        p_loc = jnp.exp2(s_loc - m).astype(BF16)
        p_ctx = jnp.exp2(s_ctx - m).astype(BF16)
        cols = slice(h * LANES, (h + 1) * LANES)
        vcat = jnp.concatenate([r[:, cols] for r in v_refs], axis=0)
        vcat = jnp.concatenate([vcat, jnp.ones(vcat.shape, BF16)], axis=-1)
        vctx = jnp.concatenate([vc_ref[:, cols], jnp.ones((vc_ref.shape[0], LANES), BF16)], axis=-1)
        o = (jnp.dot(p_loc, vcat, preferred_element_type=F32) + jnp.dot(p_ctx, vctx, preferred_element_type=F32))
        o_ref[0, :, cols] = (o[:, :LANES] / o[:, LANES:]).astype(o_ref.dtype)

    scores(0, 0)
    for h in range(NA_HEADS):
        if h + 1 < NA_HEADS:
            scores(h + 1, (h + 1) % 2)
        softmax_pv(h, h % 2)


def _na_call(qn, kn, p_lat, knc, p_ctx, bias):
    B, H, T, _ = qn.shape
    C = knc.shape[2]
    nblk = T // NA_QTILE
    nkt = T // NA_KTILE
    assert nblk >= 3 and nkt >= NA_BAND

    def cfg(i):
        return jnp.minimum(i, 1) + jnp.maximum(i - (nblk - 2), 0)

    k_specs = [pl.BlockSpec((1, H, NA_KTILE, LANES),
                            lambda b, i, u=u: (b, 0, _na_band_start(i, nkt) + u, 0)) for u in range(NA_BAND)]
    v_specs = [pl.BlockSpec((NA_KTILE, NA_W),
                            lambda b, i, u=u: (b * nkt + _na_band_start(i, nkt) + u, 2))
               for u in range(NA_BAND)]
    return pl.pallas_call(
        _na_kernel,
        out_shape=jax.ShapeDtypeStruct((B, T, NA_W), BF16),
        grid=(B, nblk),
        in_specs=[pl.BlockSpec((1, H, NA_QTILE, LANES), lambda b, i: (b, 0, i, 0))] + k_specs + v_specs + [
            pl.BlockSpec((1, H, C, LANES), lambda b, i: (b, 0, 0, 0)),
            pl.BlockSpec((C, NA_W), lambda b, i: (b, 2)),
            pl.BlockSpec((1, H, NA_QTILE, NA_BAND * NA_KTILE), lambda b, i: (cfg(i), 0, 0, 0))],
        out_specs=pl.BlockSpec((1, NA_QTILE, NA_W), lambda b, i: (b, i, 0)),
        scratch_shapes=[pltpu.VMEM((2, NA_QTILE, NA_BAND * NA_KTILE), F32), pltpu.VMEM((2, NA_QTILE, C), F32)],
        compiler_params=_cparams(("arbitrary", "arbitrary")),
        name="na_attn",
    )(qn, *([kn] * NA_BAND), *([p_lat] * NA_BAND), knc, p_ctx, bias)


def _na_bias_table(rpb, rows):
    nblk = rows // NA_QROWS
    nkt = rows // NA_KROWS
    reps = [0, 1, nblk - 1]
    q_r = np.arange(NA_QTILE) // GRID_W
    q_c = np.arange(NA_QTILE) % GRID_W
    k_r = np.arange(NA_BAND * NA_KTILE) // GRID_W
    k_c = np.arange(NA_BAND * NA_KTILE) % GRID_W
    n_ro, n_co = 2 * NA_KH - 1, 2 * NA_KW - 1
    col_sel = (np.arange(n_co)[:, None, None]
               == np.arange(GRID_W)[None, None, :] - np.arange(GRID_W)[None, :, None] + NA_KW - 1)
    toep = jnp.einsum('hrd,dqk->hrqk', rpb, col_sel.astype(np.float32), precision=lax.Precision.HIGHEST)
    toep = jnp.pad(toep, ((0, 0), (1, 1), (0, 0), (0, 0)))
    pair = jnp.concatenate([toep[:, :-1], toep[:, 1:]], axis=-1)
    tables, ok = [], []
    for i in reps:
        j0 = int(np.clip((NA_QROWS // NA_KROWS) * i - NA_KH // 2 // NA_KROWS, 0, nkt - NA_BAND))
        r = NA_QROWS * i + q_r
        rs = np.clip(r - NA_KH // 2, 0, rows - NA_KH)
        cs = np.clip(q_c - NA_KW // 2, 0, GRID_W - NA_KW)
        kr = NA_KROWS * j0 + k_r
        ok.append((kr[None, :] >= rs[:, None]) & (kr[None, :] < rs[:, None] + NA_KH)
                  & (k_c[None, :] >= cs[:, None]) & (k_c[None, :] < cs[:, None] + NA_KW))
        per_qr = []
        for qr in range(NA_QROWS):
            pieces = []
            for kt in range(NA_BAND):
                ro = NA_KROWS * (j0 + kt) - (NA_QROWS * i + qr) + NA_KH - 1
                pieces.append(pair[:, int(np.clip(ro + 1, 0, n_ro))])
            per_qr.append(jnp.concatenate(pieces, axis=-1))
        tables.append(jnp.concatenate(per_qr, axis=1))
    return jnp.where(np.stack(ok)[:, None], jnp.stack(tables) * LOG2E, NEG).astype(F32)


def _merge_kernel(om_ref, on_ref, x_ref, mod_ref, gom_ref, gon_ref, wout_ref, g2_ref, wr_ref,
                  x1_ref, h2_ref, aff_ref):
    ts = x_ref.shape[0] // MERGE_SUB
    for s in range(MERGE_SUB):
        rows = slice(s * ts, (s + 1) * ts)
        a = _rms(om_ref[rows, :].astype(F32), gom_ref[...]).astype(BF16)
        b = _rms(on_ref[rows, :].astype(F32), gon_ref[...]).astype(BF16)
        y = jnp.dot(jnp.concatenate([a, b], axis=-1), wout_ref[...], preferred_element_type=F32)
        x1 = x_ref[rows, :] + mod_ref[0, 2:3, :] * y
        x1_ref[rows, :] = x1
        h2 = _rms(x1, g2_ref[...]) * (1.0 + mod_ref[0, 4:5, :]) + mod_ref[0, 3:4, :]
        h2_ref[rows, :] = h2
        hi = h2.astype(BF16)
        lo = (h2 - hi.astype(F32)).astype(BF16)
        logits = (jnp.dot(hi, wr_ref[:, :LANES], preferred_element_type=F32)
                  + jnp.dot(lo, wr_ref[:, :LANES], preferred_element_type=F32)
                  + jnp.dot(hi, wr_ref[:, LANES:], preferred_element_type=F32))
        lane = lax.broadcasted_iota(jnp.int32, logits.shape, 1)
        logits = jnp.where(lane < N_EXPERTS, logits, NEG)
        e = jnp.exp(logits - jnp.max(logits, axis=-1, keepdims=True))
        aff_ref[rows, :] = e / jnp.sum(e, axis=-1, keepdims=True)


MERGE_SUB = 2


def _merge_call(om, on, x2, mod, gom, gon, wout, g2, wr, rows_per_mod, tm):
    M, D = x2.shape
    per = rows_per_mod // tm
    row = lambda i: (i, 0)
    full = lambda i: (0, 0)
    return pl.pallas_call(
        _merge_kernel,
        out_shape=[jax.ShapeDtypeStruct((M, D), F32), jax.ShapeDtypeStruct((M, D), F32),
                   jax.ShapeDtypeStruct((M, LANES), F32)],
        grid=(M // tm,),
        in_specs=[pl.BlockSpec((tm, om.shape[1]), row), pl.BlockSpec((tm, on.shape[1]), row),
                  pl.BlockSpec((tm, D), row),
                  pl.BlockSpec((1, N_MOD, D), lambda i: (i // per, 0, 0)),
                  pl.BlockSpec((1, om.shape[1]), full), pl.BlockSpec((1, on.shape[1]), full),
                  pl.BlockSpec(wout.shape, full), pl.BlockSpec((1, D), full),
                  pl.BlockSpec(wr.shape, full)],
        out_specs=[pl.BlockSpec((tm, D), row), pl.BlockSpec((tm, D), row), pl.BlockSpec((tm, LANES), row)],
        compiler_params=_cparams(("arbitrary",)),
        name="merge_router",
    )(om, on, x2, mod, gom, gon, wout, g2, wr)


RT_CHUNK = 256


def _select_kernel(aff_ref, slot_ref, afft_ref, bounds_ref, *, cap):
    T = aff_ref.shape[1]
    E = N_EXPERTS
    n_tiles = T // LANES
    capf = float(cap)

    def tr_body(c, carry):
        off = _mult(c * RT_CHUNK, RT_CHUNK)
        afft_ref[0, :, pl.ds(off, RT_CHUNK)] = aff_ref[0, pl.ds(off, RT_CHUNK), :].T[:E]
        return carry

    lax.fori_loop(0, T // RT_CHUNK, tr_body, 0)

    def as_float(bits):
        return lax.bitcast_convert_type(bits, F32)

    def count_ge(v):
        return jnp.sum((afft_ref[0] >= v).astype(F32), axis=1, keepdims=True)

    def bs_body(i, prefix):
        cand = prefix | jnp.left_shift(jnp.int32(1), 30 - i)
        return jnp.where(count_ge(as_float(cand)) >= capf, cand, prefix)

    thr_bits = lax.fori_loop(0, 31, bs_body, jnp.zeros((E, 1), jnp.int32))
    thr = as_float(thr_bits)
    above = as_float(jnp.maximum(thr_bits + 1, jnp.int32(0x00800000)))
    need = capf - count_ge(above)

    ri = lax.broadcasted_iota(jnp.int32, (LANES, LANES), 0)
    ci = lax.broadcasted_iota(jnp.int32, (LANES, LANES), 1)
    upper = (ri <= ci).astype(BF16)
    lane = lax.broadcasted_iota(jnp.int32, (E, LANES), 1)
    run_eq = jnp.zeros((E, 1), F32)
    run_sel = jnp.zeros((E, 1), F32)
    bounds = jnp.zeros((E, LANES), F32)
    for k in range(n_tiles):
        a = afft_ref[0, :, k * LANES:(k + 1) * LANES]
        gt = a >= above
        eq = (a >= thr) & (a < above)
        eqf = eq.astype(F32)
        incl_eq = jnp.dot(eqf.astype(BF16), upper, preferred_element_type=F32)
        sel = gt | (eq & ((incl_eq - eqf + run_eq) < need))
        self_ = sel.astype(F32)
        incl_sel = jnp.dot(self_.astype(BF16), upper, preferred_element_type=F32)
        slot_ref[0, :, k * LANES:(k + 1) * LANES] = jnp.where(sel, incl_sel - self_ + run_sel, -1.0)
        bounds = jnp.where(lane == k, run_sel, bounds)
        run_eq = run_eq + incl_eq[:, LANES - 1:LANES]
        run_sel = run_sel + incl_sel[:, LANES - 1:LANES]
    bounds_ref[0] = jnp.where(lane >= n_tiles, run_sel, bounds).astype(jnp.int32)


def _compact_kernel(bnd_ref, slot_ref, afft_ref, idx_ref, gate_ref, acc_i_ref, acc_g_ref, *, cap):
    b = pl.program_id(0)
    T = slot_ref.shape[2]
    n_tiles = T // LANES
    sub = lax.broadcasted_iota(jnp.int32, (LANES, LANES), 0).astype(F32)
    lane_f = lax.broadcasted_iota(jnp.int32, (1, LANES), 1).astype(F32)

    for e in range(N_EXPERTS):
        acc_i_ref[...] = jnp.zeros(acc_i_ref.shape, F32)
        acc_g_ref[...] = jnp.zeros(acc_g_ref.shape, F32)
        row = (b * N_EXPERTS + e) * LANES

        def tile_body(k, carry, e=e, row=row):
            lo = bnd_ref[row + k]
            hi = bnd_ref[row + k + 1]
            off = _mult(k * LANES, LANES)
            srow = slot_ref[0, e:e + 1, pl.ds(off, LANES)]
            arow = afft_ref[0, e:e + 1, pl.ds(off, LANES)]
            trow = lane_f + _f32(off)

            def block_body(sb, c2):
                s0 = _mult(sb * LANES, LANES)
                hit = srow == (sub + _f32(s0))
                acc_i_ref[pl.ds(s0, LANES), :] += jnp.where(hit, trow, 0.0)
                acc_g_ref[pl.ds(s0, LANES), :] += jnp.where(hit, arow, 0.0)
                return c2

            lax.fori_loop(lo // LANES, (hi + LANES - 1) // LANES, block_body, 0)
            return carry

        lax.fori_loop(0, n_tiles, tile_body, 0)
        idx_ref[0, e] = jnp.sum(acc_i_ref[...].T, axis=0, keepdims=True).astype(jnp.int32)
        gate_ref[0, e] = jnp.sum(acc_g_ref[...].T, axis=0, keepdims=True)


def _route_call(aff, cap):
    B, T, _ = aff.shape
    E = N_EXPERTS
    assert T // LANES < LANES
    bet = lambda b: (b, 0, 0)
    slot_t, aff_t, bounds = pl.pallas_call(
        functools.partial(_select_kernel, cap=cap),
        out_shape=[jax.ShapeDtypeStruct((B, E, T), F32), jax.ShapeDtypeStruct((B, E, T), F32),
                   jax.ShapeDtypeStruct((B, E, LANES), jnp.int32)],
        grid=(B,),
        in_specs=[pl.BlockSpec((1, T, LANES), bet)],
        out_specs=[pl.BlockSpec((1, E, T), bet), pl.BlockSpec((1, E, T), bet), pl.BlockSpec((1, E, LANES), bet)],
        compiler_params=_cparams(("arbitrary",)),
        name="route_select",
    )(aff)
    bounds_flat = bounds.reshape(-1)
    grid_spec = pltpu.PrefetchScalarGridSpec(
        num_scalar_prefetch=1,
        grid=(B,),
        in_specs=[pl.BlockSpec((1, E, T), lambda b, bnd: (b, 0, 0)),
                  pl.BlockSpec((1, E, T), lambda b, bnd: (b, 0, 0))],
        out_specs=[pl.BlockSpec((1, E, 1, cap), lambda b, bnd: (b, 0, 0, 0)),
                   pl.BlockSpec((1, E, 1, cap), lambda b, bnd: (b, 0, 0, 0))],
        scratch_shapes=[pltpu.VMEM((cap, LANES), F32), pltpu.VMEM((cap, LANES), F32)],
    )
    idx, gate = pl.pallas_call(
        functools.partial(_compact_kernel, cap=cap),
        out_shape=[jax.ShapeDtypeStruct((B, E, 1, cap), jnp.int32), jax.ShapeDtypeStruct((B, E, 1, cap), F32)],
        grid_spec=grid_spec,
        compiler_params=_cparams(("arbitrary",)),
        name="route_compact",
    )(bounds_flat, slot_t, aff_t)
    return slot_t, idx, gate, bounds_flat


def _ffn_kernel(idx_ref, h2_hbm, gate_ref, wg_ref, wu_ref, wd_ref, y_ref, xs_ref, xb_ref, hmid_ref, sem,
                *, cap, tokens, n_batch, n_steps):
    e = pl.program_id(0)
    b = pl.program_id(1)
    f = pl.program_id(2)
    n_exp = pl.num_programs(0)
    nf = pl.num_programs(2)
    rows_per_step = cap // n_steps

    def row_copy(bb, ee, s):
        t = idx_ref[(bb * n_exp + ee) * cap + s]
        return pltpu.make_async_copy(h2_hbm.at[pl.ds(bb * tokens + t, 1)], xs_ref.at[pl.ds(s, 1)], sem)

    def wait_rows():
        pltpu.make_async_copy(h2_hbm.at[pl.ds(0, cap)], xs_ref, sem).wait()

    @pl.when((e == 0) & (b == 0) & (f == 0))
    def _():
        def issue(s, carry):
            row_copy(b, e, s).start()
            return carry

        lax.fori_loop(0, cap, issue, 0, unroll=8)

    @pl.when(f == 0)
    def _():
        wait_rows()
        xb_ref[...] = xs_ref[...].astype(BF16)

    wrap_b = b + 1 == n_batch
    nb = jnp.where(wrap_b, 0, b + 1)
    ne = jnp.where(wrap_b, jnp.where(e + 1 == n_exp, 0, e + 1), e)
    for k in range(rows_per_step):
        row_copy(nb, ne, f * rows_per_step + k).start()

    tf = wg_ref.shape[2]
    wg = wg_ref[0].astype(BF16)
    wu = wu_ref[0].astype(BF16)
    hcols = pl.ds(_mult(f * tf, tf), tf)
    for s in range(FFN_SUB):
        rows = slice(s * (cap // FFN_SUB), (s + 1) * (cap // FFN_SUB))
        xb = xb_ref[rows, :]
        a = jnp.dot(xb, wg, preferred_element_type=F32)
        u = jnp.dot(xb, wu, preferred_element_type=F32)
        hmid_ref[rows, hcols] = (a * (1.0 / (1.0 + jnp.exp(-a))) * u).astype(BF16)

    @pl.when(f == nf - 1)
    def _():
        hmid = hmid_ref[...]
        gate = jnp.broadcast_to(gate_ref[0, 0], (LANES, cap)).T[:, :1]
        for n in range(y_ref.shape[3] // FFN_TN):
            cols = slice(n * FFN_TN, (n + 1) * FFN_TN)
            part = jnp.dot(hmid, wd_ref[0, :, cols].astype(BF16), preferred_element_type=F32)
            y_ref[0, 0, :, cols] = (part * gate).astype(y_ref.dtype)

    @pl.when((e == n_exp - 1) & (b == n_batch - 1) & (f == nf - 1))
    def _():
        wait_rows()


FFN_TF = 256
FFN_TN = 512
FFN_SUB = 4


def _ffn_call(idx_flat, h2, gate, wg, wu, wd, B, T, cap):
    E, D, F = wg.shape
    tf = FFN_TF
    grid_spec = pltpu.PrefetchScalarGridSpec(
        num_scalar_prefetch=1,
        grid=(E, B, F // tf),
        in_specs=[pl.BlockSpec(memory_space=pl.ANY),
                  pl.BlockSpec((1, 1, 1, cap), lambda e, b, f, idx: (b, e, 0, 0)),
                  pl.BlockSpec((1, D, tf), lambda e, b, f, idx: (e, 0, f)),
                  pl.BlockSpec((1, D, tf), lambda e, b, f, idx: (e, 0, f)),
                  pl.BlockSpec((1, F, D), lambda e, b, f, idx: (e, 0, 0))],
        out_specs=pl.BlockSpec((1, 1, cap, D), lambda e, b, f, idx: (b, e, 0, 0)),
        scratch_shapes=[pltpu.VMEM((cap, D), F32), pltpu.VMEM((cap, D), BF16), pltpu.VMEM((cap, F), BF16),
                        pltpu.SemaphoreType.DMA(())],
    )
    return pl.pallas_call(
        functools.partial(_ffn_kernel, cap=cap, tokens=T, n_batch=B, n_steps=F // tf),
        out_shape=jax.ShapeDtypeStruct((B, E, cap, D), BF16),
        grid_spec=grid_spec,
        compiler_params=_cparams(("arbitrary", "arbitrary", "arbitrary")),
        name="expert_ffn",
    )(idx_flat, h2, gate, wg, wu, wd)


CB_WIN = 64
CB_ALIGN = 16


def _combine_kernel(bnd_ref, x1_ref, mod_ref, slot_ref, y_hbm, o_ref, buf_ref, acc_ref, sem,
                    *, cap, nch, n_batch):
    b = pl.program_id(0)
    j = pl.program_id(1)
    E = N_EXPERTS
    step = b * nch + j
    cur = step % 2
    sub = lax.broadcasted_iota(jnp.int32, (CB_WIN, RT_CHUNK), 0).astype(F32)

    def tile_windows(bb, jj):
        per_tile = RT_CHUNK // LANES
        aligned, n_pass = [], jnp.int32(0)
        for e in range(E):
            at = (bb * E + e) * LANES + jj * per_tile
            base = bnd_ref[at]
            cnt = bnd_ref[at + per_tile] - base
            al = (base // CB_ALIGN) * CB_ALIGN
            aligned.append(al)
            n_pass = jnp.maximum(n_pass, (base - al + cnt + CB_WIN - 1) // CB_WIN)
        return aligned, n_pass

    def window(al, c):
        lo = al + c * CB_WIN
        return lo, _mult(jnp.minimum(lo, cap - CB_WIN), CB_ALIGN)

    def start_fetch(bb, aligned, c, slot):
        for e in range(E):
            _, st = window(aligned[e], c)
            pltpu.make_async_copy(y_hbm.at[bb, e, pl.ds(st, CB_WIN)],
                                  buf_ref.at[slot, pl.ds(e * CB_WIN, CB_WIN)], sem.at[slot]).start()

    def wait_fetch(slot):
        pltpu.make_async_copy(buf_ref.at[slot], buf_ref.at[slot], sem.at[slot]).wait()

    def scatter_add(aligned, c, slot):
        pieces = []
        for e in range(E):
            lo, st = window(aligned[e], c)
            srow = slot_ref[0, e:e + 1, :]
            pieces.append(((srow == sub + _f32(st)) & (srow >= _f32(lo))).astype(BF16))
        sel_t = jnp.concatenate(pieces, axis=0)
        return lax.dot_general(sel_t, buf_ref[slot], (((0,), (0,)), ((), ())), preferred_element_type=F32)

    aligned, n_pass = tile_windows(b, j)

    @pl.when(step == 0)
    def _():
        start_fetch(b, aligned, 0, cur)

    @pl.when(step + 1 < n_batch * nch)
    def _():
        wrap = j + 1 == nch
        nb = jnp.where(wrap, b + 1, b)
        nj = jnp.where(wrap, 0, j + 1)
        start_fetch(nb, tile_windows(nb, nj)[0], 0, 1 - cur)

    wait_fetch(cur)
    acc_ref[...] = scatter_add(aligned, 0, cur)

    def extra_pass(c, carry):
        start_fetch(b, aligned, c, cur)
        wait_fetch(cur)
        acc_ref[...] += scatter_add(aligned, c, cur)
        return carry

    lax.fori_loop(1, n_pass, extra_pass, 0)
    o_ref[...] = x1_ref[...] + mod_ref[0, 5:6, :] * acc_ref[...]


def _combine_call(bounds_flat, x1, mod, slot_t, y, B, T, cap):
    D = x1.shape[1]
    nch = T // RT_CHUNK
    E = N_EXPERTS
    grid_spec = pltpu.PrefetchScalarGridSpec(
        num_scalar_prefetch=1,
        grid=(B, nch),
        in_specs=[pl.BlockSpec((RT_CHUNK, D), lambda b, j, bnd: (b * nch + j, 0)),
                  pl.BlockSpec((1, N_MOD, D), lambda b, j, bnd: (b, 0, 0)),
                  pl.BlockSpec((1, E, RT_CHUNK), lambda b, j, bnd: (b, 0, j)),
                  pl.BlockSpec(memory_space=pl.ANY)],
        out_specs=pl.BlockSpec((RT_CHUNK, D), lambda b, j, bnd: (b * nch + j, 0)),
        scratch_shapes=[pltpu.VMEM((2, E * CB_WIN, D), BF16), pltpu.VMEM((RT_CHUNK, D), F32),
                        pltpu.SemaphoreType.DMA((2,))],
    )
    return pl.pallas_call(
        functools.partial(_combine_kernel, cap=cap, nch=nch, n_batch=B),
        out_shape=jax.ShapeDtypeStruct(x1.shape, F32),
        grid_spec=grid_spec,
        compiler_params=_cparams(("arbitrary", "arbitrary")),
        name="combine",
    )(bounds_flat, x1, mod, slot_t, y)


def _rope_tables(n_tokens):
    t = np.arange(n_tokens)
    row = (t // GRID_W).astype(np.float64)
    col = (t % GRID_W).astype(np.float64)
    n_freq = MLA_ROPE // 4
    inv = ROPE_THETA ** (-np.arange(n_freq, dtype=np.float64) / n_freq)
    ang = np.concatenate([row[:, None] * inv, col[:, None] * inv], axis=-1)
    cos, sin = np.cos(ang), np.sin(ang)
    z = np.zeros((n_tokens, LANES - MLA_ROPE))
    return (np.concatenate([cos, cos, z], axis=-1).astype(np.float32),
            np.concatenate([-sin, sin, z], axis=-1).astype(np.float32))


def _pad_lanes(v):
    return jnp.concatenate([v, jnp.zeros((LANES - v.shape[0],), v.dtype)])[None, :]


def kernel(x, c, ctx, c_ctx, w_mod, b_mod, g_norm1, w_in, g_qa, w_qb, g_kva, w_kvb, g_q_mla, g_k_mla,
           g_q_na, g_k_na, rpb_na, g_out_mla, g_out_na, w_out, g_norm2, w_router, w_gate, w_up, w_down):
    B, T, D = x.shape
    C = ctx.shape[1]
    assert w_mod.shape[0] == 1, "single-layer problem"
    assert T % GRID_W == 0 and (T // GRID_W) % NA_QROWS == 0 and T // GRID_W >= 2 * NA_BAND
    assert w_in.shape[2] == 3 * NA_W + MLA_Q_RANK + MLA_KV_RANK + MLA_ROPE
    rows = T // GRID_W
    cap = EC_CAPACITY_FACTOR * T // N_EXPERTS
    half = MLA_ROPE // 2

    w_in_p = _win_call(w_in[0])
    wq3 = w_qb[0].reshape(MLA_Q_RANK, MLA_HEADS, MLA_QK_DIM)
    wq_p = jnp.concatenate([wq3, wq3[..., MLA_NOPE + half:], wq3[..., MLA_NOPE:MLA_NOPE + half]], axis=-1)
    wq_p = wq_p.reshape(MLA_Q_RANK, MLA_HEADS * MLA_QK_PAD).astype(BF16)
    wkv3 = w_kvb[0].reshape(MLA_KV_RANK, MLA_HEADS, MLA_NOPE + MLA_V)
    wk_p = wkv3[..., :MLA_NOPE].reshape(MLA_KV_RANK, MLA_HEADS * MLA_NOPE).astype(BF16)
    wv_p = wkv3[..., MLA_NOPE:].reshape(MLA_KV_RANK, MLA_HEADS * MLA_V).astype(BF16)

    def rope_gains(g):
        gr = g[MLA_NOPE:]
        return g[None, :MLA_NOPE], _pad_lanes(gr), _pad_lanes(jnp.concatenate([gr[half:], gr[:half]]))

    gq0, gqa, gqb = rope_gains(g_q_mla[0])
    gk0, gka, gkb = rope_gains(g_k_mla[0])
    ct, st = _rope_tables(T)
    consts = dict(ct=ct, st=st, g_qa=g_qa, wq=wq_p, g_kva=g_kva, wk=wk_p, wv=wv_p,
                  gq0=gq0, gqa=gqa, gqb=gqb, gk0=gk0, gka=gka, gkb=gkb, g_q_na=g_q_na, g_k_na=g_k_na)
    wr = jnp.concatenate([w_router[0], jnp.zeros((D, LANES - N_EXPERTS), F32)], axis=1)
    wr_hi = wr.astype(BF16)
    wr_split = jnp.concatenate([wr_hi, (wr - wr_hi.astype(F32)).astype(BF16)], axis=1)
    bias = _na_bias_table(rpb_na[0], rows)

    cvec = jnp.concatenate([c, c_ctx[None, :], jnp.zeros((8 - B - 1, D), F32)], axis=0)
    mod = _mod_call(cvec.T, w_mod[0], b_mod, B + 1)
    mod_lat = mod[:B].reshape(B, N_MOD, D)
    mod_ctx = mod[B:].reshape(1, N_MOD, D)

    x2 = x.reshape(B * T, D)
    p_lat = _inproj_call(x2, mod_lat, g_norm1, w_in_p, T, min(512, T))
    p_ctx = _inproj_call(ctx.reshape(B * C, D), mod_ctx, g_norm1, w_in_p, B * C, C)
    qm, km, vm, qn, kn = _prep_call(p_lat, B, T, min(512, T), consts, False)
    kmc, vmc, knc = _prep_call(p_ctx, B, C, C, consts, True)
    o_m = _mla_call(qm, km, vm, kmc, vmc, min(1024, T), min(1024, T))
    o_n = _na_call(qn, kn, p_lat, knc, p_ctx, bias)
    x1, h2, aff = _merge_call(o_m.reshape(B * T, -1), o_n.reshape(B * T, -1), x2, mod_lat,
                              g_out_mla, g_out_na, w_out[0].astype(BF16), g_norm2, wr_split,
                              T, min(512, T))

    slot_t, idx, gate, bounds = _route_call(aff.reshape(B, T, LANES), cap)
    y = _ffn_call(idx.reshape(-1), h2, gate, w_gate[0], w_up[0], w_down[0], B, T, cap)
    out = _combine_call(bounds, x1, mod_lat, slot_t, y, B, T, cap)
    return out.reshape(B, T, D)
```

```python
import functools

import numpy as np
import jax
import jax.numpy as jnp
from jax import lax
from jax.experimental import pallas as pl
from jax.experimental.pallas import tpu as pltpu

F32 = jnp.float32
BF16 = jnp.bfloat16

GRID_W = 64
MLA_HEADS = 8
MLA_NOPE = 128
MLA_ROPE = 64
MLA_QK_DIM = MLA_NOPE + MLA_ROPE
MLA_V = 128
MLA_Q_RANK = 512
MLA_KV_RANK = 256
NA_HEADS = 8
NA_HEAD_DIM = 128
NA_KH = 8
NA_KW = 16
N_EXPERTS = 16
EC_CAPACITY_FACTOR = 2
ROPE_THETA = 10000.0
EPS = 1e-6
N_MOD = 6
MLA_SCALE = MLA_QK_DIM ** -0.5
NA_SCALE = NA_HEAD_DIM ** -0.5
LOG2E = 1.4426950408889634

LANES = 128
MLA_QK_PAD = 2 * LANES
NA_W = NA_HEADS * NA_HEAD_DIM
P_WIDTH = 3 * NA_W + 1024
VMEM_LIMIT = 56 * 1024 * 1024
NEG = -1e30
F32_MIN_NORMAL_BITS = 0x00800000

WIN_TR = 256
MOD_TN = 1024
INPROJ_TM = 512
PREP_TM = 512
MERGE_TM = 512
MLA_TQ = 1024
MLA_TK = 1024


def _f32(v):
    return jnp.asarray(v, dtype=F32)


def _mult(v, m):
    return v if isinstance(v, int) else pl.multiple_of(v, m)


def _cparams(sem, vmem=VMEM_LIMIT, **kw):
    return pltpu.CompilerParams(dimension_semantics=sem, vmem_limit_bytes=vmem, **kw)


def _mod_kernel(ct_ref, w_ref, b_ref, o_ref):
    ct = ct_ref[...]
    s = ct * (1.0 / (1.0 + jnp.exp(-ct)))
    w = w_ref[...]
    rows = []
    for m in range(o_ref.shape[0]):
        rows.append(jnp.sum(w * s[:, m:m + 1], axis=0, keepdims=True))
    o_ref[...] = jnp.concatenate(rows, axis=0) + b_ref[...]


def _mod_call(ct, w_mod, b_mod, n_rows):
    D, N = w_mod.shape
    tn = MOD_TN
    return pl.pallas_call(
        _mod_kernel,
        out_shape=jax.ShapeDtypeStruct((n_rows, N), F32),
        grid=(N // tn,),
        in_specs=[pl.BlockSpec((D, 8), lambda j: (0, 0)),
                  pl.BlockSpec((D, tn), lambda j: (0, j)),
                  pl.BlockSpec((1, tn), lambda j: (0, j))],
        out_specs=pl.BlockSpec((n_rows, tn), lambda j: (0, j)),
        compiler_params=_cparams(("arbitrary",)),
        name="mod",
    )(ct, w_mod, b_mod)


def _win_kernel(w_ref, o_ref):
    w = w_ref[...]
    half = MLA_ROPE // 2
    o_pe = MLA_Q_RANK + MLA_KV_RANK
    o_na = o_pe + MLA_ROPE
    pieces = [w[:, o_na:], w[:, :o_pe], w[:, o_pe:o_na], w[:, o_pe + half:o_na], w[:, o_pe:o_pe + half],
              jnp.zeros((w.shape[0], P_WIDTH - w.shape[1] - MLA_ROPE), F32)]
    o_ref[...] = jnp.concatenate(pieces, axis=1).astype(BF16)


def _win_call(w):
    D, N = w.shape
    tr = WIN_TR
    return pl.pallas_call(
        _win_kernel,
        out_shape=jax.ShapeDtypeStruct((D, P_WIDTH), BF16),
        grid=(D // tr,),
        in_specs=[pl.BlockSpec((tr, N), lambda i: (i, 0))],
        out_specs=pl.BlockSpec((tr, P_WIDTH), lambda i: (i, 0)),
        compiler_params=_cparams(("arbitrary",)),
        name="w_in_layout",
    )(w)


INPROJ_SUB = 2
INPROJ_TN = 1024


def _inproj_kernel(x_ref, mod_ref, g_ref, w_ref, o_ref):
    ts = x_ref.shape[0] // INPROJ_SUB
    for s in range(INPROJ_SUB):
        rows = slice(s * ts, (s + 1) * ts)
        x = x_ref[rows, :]
        y = x * lax.rsqrt(jnp.mean(x * x, axis=-1, keepdims=True) + EPS) * g_ref[...]
        h = (y * (1.0 + mod_ref[0, 1:2, :]) + mod_ref[0, 0:1, :]).astype(BF16)
        for n in range(w_ref.shape[1] // INPROJ_TN):
            cols = slice(n * INPROJ_TN, (n + 1) * INPROJ_TN)
            o_ref[rows, cols] = jnp.dot(h, w_ref[:, cols], preferred_element_type=F32).astype(BF16)


def _inproj_call(x2, mod, g, w, rows_per_mod, tm):
    M, D = x2.shape
    N = w.shape[1]
    per = rows_per_mod // tm
    return pl.pallas_call(
        _inproj_kernel,
        out_shape=jax.ShapeDtypeStruct((M, N), BF16),
        grid=(M // tm,),
        in_specs=[pl.BlockSpec((tm, D), lambda i: (i, 0)),
                  pl.BlockSpec((1, N_MOD, D), lambda i: (i // per, 0, 0)),
                  pl.BlockSpec((1, D), lambda i: (0, 0)),
                  pl.BlockSpec((D, N), lambda i: (0, 0), pipeline_mode=pl.Buffered(1))],
        out_specs=pl.BlockSpec((tm, N), lambda i: (i, 0)),
        compiler_params=_cparams(("arbitrary",)),
        name="inproj",
    )(x2, mod, g, w)


def _rms(x, g):
    return x * lax.rsqrt(jnp.mean(x * x, axis=-1, keepdims=True) + EPS) * g


def _prep_kernel(*refs, is_ctx):
    if is_ctx:
        (pk_ref, pm_ref, gkva_ref, wk_ref, wv_ref, gk0_ref, gka_ref, gkn_ref,
         km_ref, vm_ref, kn_ref) = refs
    else:
        (pq_ref, pk_ref, pm_ref, ct_ref, st_ref, gqa_ref, wq_ref, gkva_ref, wk_ref, wv_ref,
         gq0_ref, gqa2_ref, gqb2_ref, gk0_ref, gka_ref, gkb_ref, gqn_ref, gkn_ref,
         qm_ref, km_ref, vm_ref, qn_ref, kn_ref) = refs
    tm = pm_ref.shape[0]
    lane = lax.broadcasted_iota(jnp.int32, (1, LANES), 1)
    rope_mask = (lane < MLA_ROPE).astype(F32)
    pm = pm_ref[...].astype(F32)
    inv_qk = 1.0 / MLA_QK_DIM

    ckvn = _rms(pm[:, MLA_Q_RANK:MLA_Q_RANK + MLA_KV_RANK], gkva_ref[...]).astype(BF16)
    kn = jnp.dot(ckvn, wk_ref[...], preferred_element_type=F32)
    vv = jnp.dot(ckvn, wv_ref[...], preferred_element_type=F32)
    kpe = pm[:, MLA_Q_RANK + MLA_KV_RANK:MLA_Q_RANK + MLA_KV_RANK + LANES]
    ss_pe = jnp.sum(kpe * kpe * rope_mask, axis=-1, keepdims=True)
    if is_ctx:
        k_rope = kpe * gka_ref[...]
    else:
        ct = ct_ref[...]
        st = st_ref[...]
        k_rope = kpe * (ct * gka_ref[...]) + pltpu.roll(kpe, MLA_ROPE, 1) * (st * gkb_ref[...])
    for h in range(MLA_HEADS):
        k0 = kn[:, h * LANES:(h + 1) * LANES]
        r = lax.rsqrt((jnp.sum(k0 * k0, axis=-1, keepdims=True) + ss_pe) * inv_qk + EPS)
        km_ref[0, h] = jnp.concatenate([k0 * r * gk0_ref[...], k_rope * r], axis=-1).astype(BF16)
        vm_ref[0, h] = jnp.concatenate([vv[:, h * LANES:(h + 1) * LANES], jnp.ones((tm, LANES), F32)],
                                       axis=-1).astype(BF16)

    pk = pk_ref[...].astype(F32)
    for h in range(NA_HEADS):
        kn_ref[0, h] = _rms(pk[:, h * LANES:(h + 1) * LANES], gkn_ref[...]).astype(BF16)

    if is_ctx:
        return

    cqn = _rms(pm[:, :MLA_Q_RANK], gqa_ref[...]).astype(BF16)
    q = jnp.dot(cqn, wq_ref[...], preferred_element_type=F32)
    qa = ct * gqa2_ref[...]
    qb = st * gqb2_ref[...]
    for h in range(MLA_HEADS):
        t0 = q[:, h * MLA_QK_PAD:h * MLA_QK_PAD + LANES]
        t1 = q[:, h * MLA_QK_PAD + LANES:(h + 1) * MLA_QK_PAD]
        ss = jnp.sum(t0 * t0 + t1 * t1 * rope_mask, axis=-1, keepdims=True)
        r = lax.rsqrt(ss * inv_qk + EPS) * (MLA_SCALE * LOG2E)
        o1 = t1 * qa + pltpu.roll(t1, MLA_ROPE, 1) * qb
        qm_ref[0, h] = jnp.concatenate([t0 * r * gq0_ref[...], o1 * r], axis=-1).astype(BF16)

    pq = pq_ref[...].astype(F32)
    for h in range(NA_HEADS):
        qn_ref[0, h] = (_rms(pq[:, h * LANES:(h + 1) * LANES], gqn_ref[...]) * (NA_SCALE * LOG2E)).astype(BF16)


def _prep_call(p, nb, tok, tm, consts, is_ctx):
    per = tok // tm
    H = MLA_HEADS
    row = lambda i: (i, 0)
    full = lambda i: (0, 0)
    hm = lambda i: (i // per, 0, i % per, 0)
    p_spec = lambda c: pl.BlockSpec((tm, 1024), lambda i, c=c: (i, c))
    vec = lambda n: pl.BlockSpec((1, n), full)
    km_s = jax.ShapeDtypeStruct((nb, H, tok, MLA_QK_PAD), BF16)
    h128_s = jax.ShapeDtypeStruct((nb, H, tok, LANES), BF16)
    km_o = pl.BlockSpec((1, H, tm, MLA_QK_PAD), hm)
    h128_o = pl.BlockSpec((1, H, tm, LANES), hm)
    c = consts
    if is_ctx:
        ins = [p, p, c["g_kva"], c["wk"], c["wv"], c["gk0"], c["gka"], c["g_k_na"]]
        in_specs = [p_spec(1), p_spec(3), vec(MLA_KV_RANK),
                    pl.BlockSpec(c["wk"].shape, full), pl.BlockSpec(c["wv"].shape, full),
                    vec(LANES), vec(LANES), vec(LANES)]
        out_shape = [km_s, km_s, h128_s]
        out_specs = [km_o, km_o, h128_o]
    else:
        ins = [p, p, p, c["ct"], c["st"], c["g_qa"], c["wq"], c["g_kva"], c["wk"], c["wv"],
               c["gq0"], c["gqa"], c["gqb"], c["gk0"], c["gka"], c["gkb"], c["g_q_na"], c["g_k_na"]]
        tab = pl.BlockSpec((tm, LANES), lambda i: (i % per, 0))
        in_specs = [p_spec(0), p_spec(1), p_spec(3), tab, tab, vec(MLA_Q_RANK),
                    pl.BlockSpec(c["wq"].shape, full), vec(MLA_KV_RANK),
                    pl.BlockSpec(c["wk"].shape, full), pl.BlockSpec(c["wv"].shape, full),
                    vec(LANES), vec(LANES), vec(LANES), vec(LANES), vec(LANES), vec(LANES),
                    vec(LANES), vec(LANES)]
        out_shape = [km_s, km_s, km_s, h128_s, h128_s]
        out_specs = [km_o, km_o, km_o, h128_o, h128_o]
    return pl.pallas_call(
        functools.partial(_prep_kernel, is_ctx=is_ctx),
        out_shape=out_shape,
        grid=(nb * per,),
        in_specs=in_specs,
        out_specs=out_specs,
        compiler_params=_cparams(("arbitrary",)),
        name="prep_ctx" if is_ctx else "prep_lat",
    )(*ins)


MLA_UNROLL = 2
MLA_SUB = 2


def _mla_kernel(q_ref, k_ref, v_ref, kc_ref, vc_ref, o_ref, m_ref, acc_ref,
                sa_ref, sb_ref, sc_ref, mxa_ref, mxb_ref, mxc_ref, *, tq, tk):
    ts = tq // MLA_SUB
    T = k_ref.shape[2]
    n = T // tk
    nq = T // tq
    dn = (((1,), (1,)), ((), ()))

    def scores_into(t, keys, s_ref, mx_ref):
        q = q_ref[0, 0, pl.ds(_mult(t * tq, tq), tq), :]
        s = lax.dot_general(q, keys, dn, preferred_element_type=F32)
        s_ref[...] = s
        mx_ref[...] = jnp.broadcast_to(jnp.max(s, axis=-1, keepdims=True), mx_ref.shape)

    def keys(c):
        return k_ref[0, 0, pl.ds(_mult(c * tk, tk), tk), :]

    def values(c):
        return v_ref[0, 0, pl.ds(_mult(c * tk, tk), tk), :]

    def softmax_pv(s_ref, mx_ref, v):
        for u in range(MLA_SUB):
            rows = slice(u * ts, (u + 1) * ts)
            m_old = m_ref[rows, :]
            m_new = jnp.maximum(m_old, mx_ref[rows, :])
            alpha = jnp.exp2(m_old - m_new)
            p = jnp.concatenate([jnp.exp2(s_ref[rows, j * LANES:(j + 1) * LANES] - m_new)
                                 for j in range(s_ref.shape[1] // LANES)], axis=-1).astype(BF16)
            acc_ref[rows, :] = (jnp.concatenate([alpha] * (acc_ref.shape[1] // LANES), axis=-1) * acc_ref[rows, :]
                                + jnp.dot(p, v, preferred_element_type=F32))
            m_ref[rows, :] = m_new

    scores_into(0, keys(0), sa_ref, mxa_ref)

    def tile_body(t, carry):
        m_ref[...] = jnp.full(m_ref.shape, NEG, F32)
        acc_ref[...] = jnp.zeros(acc_ref.shape, F32)

        bufs = ((sa_ref, mxa_ref), (sb_ref, mxb_ref))

        def group_body(i, c2):
            c0 = MLA_UNROLL * i
            for u in range(MLA_UNROLL):
                scores_into(t, keys(c0 + u + 1), *bufs[(u + 1) % 2])
                softmax_pv(*bufs[u % 2], values(c0 + u))
            return c2

        n_loop = (n - 2) // MLA_UNROLL * MLA_UNROLL
        lax.fori_loop(0, n_loop // MLA_UNROLL, group_body, 0)
        for c in range(n_loop, n):
            if c + 1 < n:
                scores_into(t, keys(c + 1), *bufs[(c + 1) % 2])
            else:
                scores_into(t, kc_ref[0, 0], sc_ref, mxc_ref)
            softmax_pv(*bufs[c % 2], values(c))
        scores_into(jnp.minimum(t + 1, nq - 1), keys(0), sa_ref, mxa_ref)
        softmax_pv(sc_ref, mxc_ref, vc_ref[0, 0])
        o_ref[0, pl.ds(_mult(t * tq, tq), tq), :] = (acc_ref[:, :MLA_V] / acc_ref[:, MLA_V:]).astype(o_ref.dtype)
        return carry

    lax.fori_loop(0, nq, tile_body, 0)


def _mla_call(qm, km, vm, kmc, vmc, tq, tk):
    B, H, T, _ = qm.shape
    C = kmc.shape[2]
    VW = vm.shape[3]
    assert (T // tk) % 2 == 0 and T % tq == 0
    bh = lambda b, h: (b, h, 0, 0)
    return pl.pallas_call(
        functools.partial(_mla_kernel, tq=tq, tk=tk),
        out_shape=jax.ShapeDtypeStruct((B, T, H * MLA_V), BF16),
        grid=(B, H),
        in_specs=[pl.BlockSpec((1, 1, T, MLA_QK_PAD), bh),
                  pl.BlockSpec((1, 1, T, MLA_QK_PAD), bh),
                  pl.BlockSpec((1, 1, T, VW), bh),
                  pl.BlockSpec((1, 1, C, MLA_QK_PAD), bh),
                  pl.BlockSpec((1, 1, C, VW), bh)],
        out_specs=pl.BlockSpec((1, T, MLA_V), lambda b, h: (b, 0, h)),
        scratch_shapes=[pltpu.VMEM((tq, LANES), F32), pltpu.VMEM((tq, VW), F32),
                        pltpu.VMEM((tq, tk), F32), pltpu.VMEM((tq, tk), F32), pltpu.VMEM((tq, C), F32),
                        pltpu.VMEM((tq, LANES), F32), pltpu.VMEM((tq, LANES), F32), pltpu.VMEM((tq, LANES), F32)],
        compiler_params=_cparams(("arbitrary", "arbitrary")),
        name="mla_attn",
    )(qm, km, vm, kmc, vmc)


NA_QROWS = 4
NA_KROWS = 2
NA_QTILE = NA_QROWS * GRID_W
NA_KTILE = NA_KROWS * GRID_W
NA_BAND = 6
NA_CFGS = 3


def _na_band_start(i, nkt):
    return jnp.clip((NA_QROWS // NA_KROWS) * i - NA_KH // 2 // NA_KROWS, 0, nkt - NA_BAND)


def _na_kernel(q_ref, *refs):
    k_refs = refs[:NA_BAND]
    v_refs = refs[NA_BAND:2 * NA_BAND]
    kc_ref, vc_ref, bias_ref, o_ref, sl_ref, sx_ref = refs[2 * NA_BAND:]
    dn = (((1,), (1,)), ((), ()))

    def scores(h, slot):
        q = q_ref[0, h]
        kcat = jnp.concatenate([r[0, h] for r in k_refs], axis=0)
        sl_ref[slot] = lax.dot_general(q, kcat, dn, preferred_element_type=F32) + bias_ref[0, h]
        sx_ref[slot] = lax.dot_general(q, kc_ref[0, h], dn, preferred_element_type=F32)

    def softmax_pv(h, slot):
        s_loc = sl_ref[slot]
        s_ctx = sx_ref[slot]
        m = jnp.maximum(jnp.max(s_loc, axis=-1, keepdims=True), jnp.max(s_ctx, axis=-1, keepdims=True))
        p_loc = jnp.exp2(s_loc - m).astype(BF16)
        p_ctx = jnp.exp2(s_ctx - m).astype(BF16)
        cols = slice(h * LANES, (h + 1) * LANES)
        vcat = jnp.concatenate([r[:, cols] for r in v_refs], axis=0)
        vcat = jnp.concatenate([vcat, jnp.ones(vcat.shape, BF16)], axis=-1)
        vctx = jnp.concatenate([vc_ref[:, cols], jnp.ones((vc_ref.shape[0], LANES), BF16)], axis=-1)
        o = (jnp.dot(p_loc, vcat, preferred_element_type=F32) + jnp.dot(p_ctx, vctx, preferred_element_type=F32))
        o_ref[0, :, cols] = (o[:, :LANES] / o[:, LANES:]).astype(o_ref.dtype)

    scores(0, 0)
    for h in range(NA_HEADS):
        if h + 1 < NA_HEADS:
            scores(h + 1, (h + 1) % 2)
        softmax_pv(h, h % 2)


def _na_call(qn, kn, p_lat, knc, p_ctx, bias):
    B, H, T, _ = qn.shape
    C = knc.shape[2]
    nblk = T // NA_QTILE
    nkt = T // NA_KTILE
    assert nblk >= 3 and nkt >= NA_BAND

    def cfg(i):
        return jnp.minimum(i, 1) + jnp.maximum(i - (nblk - 2), 0)

    k_specs = [pl.BlockSpec((1, H, NA_KTILE, LANES),
                            lambda b, i, u=u: (b, 0, _na_band_start(i, nkt) + u, 0)) for u in range(NA_BAND)]
    v_specs = [pl.BlockSpec((NA_KTILE, NA_W),
                            lambda b, i, u=u: (b * nkt + _na_band_start(i, nkt) + u, 2))
               for u in range(NA_BAND)]
    return pl.pallas_call(
        _na_kernel,
        out_shape=jax.ShapeDtypeStruct((B, T, NA_W), BF16),
        grid=(B, nblk),
        in_specs=[pl.BlockSpec((1, H, NA_QTILE, LANES), lambda b, i: (b, 0, i, 0))] + k_specs + v_specs + [
            pl.BlockSpec((1, H, C, LANES), lambda b, i: (b, 0, 0, 0)),
            pl.BlockSpec((C, NA_W), lambda b, i: (b, 2)),
            pl.BlockSpec((1, H, NA_QTILE, NA_BAND * NA_KTILE), lambda b, i: (cfg(i), 0, 0, 0))],
        out_specs=pl.BlockSpec((1, NA_QTILE, NA_W), lambda b, i: (b, i, 0)),
        scratch_shapes=[pltpu.VMEM((2, NA_QTILE, NA_BAND * NA_KTILE), F32), pltpu.VMEM((2, NA_QTILE, C), F32)],
        compiler_params=_cparams(("arbitrary", "arbitrary")),
        name="na_attn",
    )(qn, *([kn] * NA_BAND), *([p_lat] * NA_BAND), knc, p_ctx, bias)


def _na_bias_table(rpb, rows):
    nblk = rows // NA_QROWS
    nkt = rows // NA_KROWS
    reps = [0, 1, nblk - 1]
    q_r = np.arange(NA_QTILE) // GRID_W
    q_c = np.arange(NA_QTILE) % GRID_W
    k_r = np.arange(NA_BAND * NA_KTILE) // GRID_W
    k_c = np.arange(NA_BAND * NA_KTILE) % GRID_W
    n_ro, n_co = 2 * NA_KH - 1, 2 * NA_KW - 1
    col_sel = (np.arange(n_co)[:, None, None]
               == np.arange(GRID_W)[None, None, :] - np.arange(GRID_W)[None, :, None] + NA_KW - 1)
    toep = jnp.einsum('hrd,dqk->hrqk', rpb, col_sel.astype(np.float32), precision=lax.Precision.HIGHEST)
    toep = jnp.pad(toep, ((0, 0), (1, 1), (0, 0), (0, 0)))
    pair = jnp.concatenate([toep[:, :-1], toep[:, 1:]], axis=-1)
    tables, ok = [], []
    for i in reps:
        j0 = int(np.clip((NA_QROWS // NA_KROWS) * i - NA_KH // 2 // NA_KROWS, 0, nkt - NA_BAND))
        r = NA_QROWS * i + q_r
        rs = np.clip(r - NA_KH // 2, 0, rows - NA_KH)
        cs = np.clip(q_c - NA_KW // 2, 0, GRID_W - NA_KW)
        kr = NA_KROWS * j0 + k_r
        ok.append((kr[None, :] >= rs[:, None]) & (kr[None, :] < rs[:, None] + NA_KH)
                  & (k_c[None, :] >= cs[:, None]) & (k_c[None, :] < cs[:, None] + NA_KW))
        per_qr = []
        for qr in range(NA_QROWS):
            pieces = []
            for kt in range(NA_BAND):
                ro = NA_KROWS * (j0 + kt) - (NA_QROWS * i + qr) + NA_KH - 1
                pieces.append(pair[:, int(np.clip(ro + 1, 0, n_ro))])
            per_qr.append(jnp.concatenate(pieces, axis=-1))
        tables.append(jnp.concatenate(per_qr, axis=1))
    return jnp.where(np.stack(ok)[:, None], jnp.stack(tables) * LOG2E, NEG).astype(F32)


MERGE_SUB = 2


def _merge_kernel(om_ref, on_ref, x_ref, mod_ref, gom_ref, gon_ref, wout_ref, g2_ref, wr_ref,
                  x1_ref, h2_ref, aff_ref):
    ts = x_ref.shape[0] // MERGE_SUB
    for s in range(MERGE_SUB):
        rows = slice(s * ts, (s + 1) * ts)
        a = _rms(om_ref[rows, :].astype(F32), gom_ref[...]).astype(BF16)
        b = _rms(on_ref[rows, :].astype(F32), gon_ref[...]).astype(BF16)
        y = jnp.dot(jnp.concatenate([a, b], axis=-1), wout_ref[...], preferred_element_type=F32)
        x1 = x_ref[rows, :] + mod_ref[0, 2:3, :] * y
        x1_ref[rows, :] = x1
        h2 = _rms(x1, g2_ref[...]) * (1.0 + mod_ref[0, 4:5, :]) + mod_ref[0, 3:4, :]
        h2_ref[rows, :] = h2
        hi = h2.astype(BF16)
        lo = (h2 - hi.astype(F32)).astype(BF16)
        logits = (jnp.dot(hi, wr_ref[:, :LANES], preferred_element_type=F32)
                  + jnp.dot(lo, wr_ref[:, :LANES], preferred_element_type=F32)
                  + jnp.dot(hi, wr_ref[:, LANES:], preferred_element_type=F32))
        lane = lax.broadcasted_iota(jnp.int32, logits.shape, 1)
        logits = jnp.where(lane < N_EXPERTS, logits, NEG)
        e = jnp.exp(logits - jnp.max(logits, axis=-1, keepdims=True))
        aff_ref[rows, :] = e / jnp.sum(e, axis=-1, keepdims=True)


def _merge_call(om, on, x2, mod, gom, gon, wout, g2, wr, rows_per_mod, tm):
    M, D = x2.shape
    per = rows_per_mod // tm
    row = lambda i: (i, 0)
    full = lambda i: (0, 0)
    return pl.pallas_call(
        _merge_kernel,
        out_shape=[jax.ShapeDtypeStruct((M, D), F32), jax.ShapeDtypeStruct((M, D), F32),
                   jax.ShapeDtypeStruct((M, LANES), F32)],
        grid=(M // tm,),
        in_specs=[pl.BlockSpec((tm, om.shape[1]), row), pl.BlockSpec((tm, on.shape[1]), row),
                  pl.BlockSpec((tm, D), row),
                  pl.BlockSpec((1, N_MOD, D), lambda i: (i // per, 0, 0)),
                  pl.BlockSpec((1, om.shape[1]), full), pl.BlockSpec((1, on.shape[1]), full),
                  pl.BlockSpec(wout.shape, full), pl.BlockSpec((1, D), full),
                  pl.BlockSpec(wr.shape, full)],
        out_specs=[pl.BlockSpec((tm, D), row), pl.BlockSpec((tm, D), row), pl.BlockSpec((tm, LANES), row)],
        compiler_params=_cparams(("arbitrary",)),
        name="merge_router",
    )(om, on, x2, mod, gom, gon, wout, g2, wr)


RT_CHUNK = 256


def _select_kernel(aff_ref, slot_ref, afft_ref, bounds_ref, *, cap):
    T = aff_ref.shape[1]
    E = N_EXPERTS
    n_tiles = T // LANES
    capf = float(cap)

    def tr_body(c, carry):
        off = _mult(c * RT_CHUNK, RT_CHUNK)
        afft_ref[0, :, pl.ds(off, RT_CHUNK)] = aff_ref[0, pl.ds(off, RT_CHUNK), :].T[:E]
        return carry

    lax.fori_loop(0, T // RT_CHUNK, tr_body, 0)

    def as_float(bits):
        return lax.bitcast_convert_type(bits, F32)

    def count_ge(v):
        return jnp.sum((afft_ref[0] >= v).astype(F32), axis=1, keepdims=True)

    def bs_body(i, prefix):
        cand = prefix | jnp.left_shift(jnp.int32(1), 30 - i)
        return jnp.where(count_ge(as_float(cand)) >= capf, cand, prefix)

    thr_bits = lax.fori_loop(0, 31, bs_body, jnp.zeros((E, 1), jnp.int32))
    thr = as_float(thr_bits)
    above = as_float(jnp.maximum(thr_bits + 1, jnp.int32(F32_MIN_NORMAL_BITS)))
    need = capf - count_ge(above)

    ri = lax.broadcasted_iota(jnp.int32, (LANES, LANES), 0)
    ci = lax.broadcasted_iota(jnp.int32, (LANES, LANES), 1)
    upper = (ri <= ci).astype(BF16)
    lane = lax.broadcasted_iota(jnp.int32, (E, LANES), 1)
    run_eq = jnp.zeros((E, 1), F32)
    run_sel = jnp.zeros((E, 1), F32)
    bounds = jnp.zeros((E, LANES), F32)
    for k in range(n_tiles):
        a = afft_ref[0, :, k * LANES:(k + 1) * LANES]
        gt = a >= above
        eq = (a >= thr) & (a < above)
        eqf = eq.astype(F32)
        incl_eq = jnp.dot(eqf.astype(BF16), upper, preferred_element_type=F32)
        sel = gt | (eq & ((incl_eq - eqf + run_eq) < need))
        self_ = sel.astype(F32)
        incl_sel = jnp.dot(self_.astype(BF16), upper, preferred_element_type=F32)
        slot_ref[0, :, k * LANES:(k + 1) * LANES] = jnp.where(sel, incl_sel - self_ + run_sel, -1.0)
        bounds = jnp.where(lane == k, run_sel, bounds)
        run_eq = run_eq + incl_eq[:, LANES - 1:LANES]
        run_sel = run_sel + incl_sel[:, LANES - 1:LANES]
    bounds_ref[0] = jnp.where(lane >= n_tiles, run_sel, bounds).astype(jnp.int32)


def _compact_kernel(bnd_ref, slot_ref, afft_ref, idx_ref, gate_ref, acc_i_ref, acc_g_ref, *, cap):
    b = pl.program_id(0)
    T = slot_ref.shape[2]
    n_tiles = T // LANES
    sub = lax.broadcasted_iota(jnp.int32, (LANES, LANES), 0).astype(F32)
    lane_f = lax.broadcasted_iota(jnp.int32, (1, LANES), 1).astype(F32)

    for e in range(N_EXPERTS):
        acc_i_ref[...] = jnp.zeros(acc_i_ref.shape, F32)
        acc_g_ref[...] = jnp.zeros(acc_g_ref.shape, F32)
        row = (b * N_EXPERTS + e) * LANES

        def tile_body(k, carry, e=e, row=row):
            lo = bnd_ref[row + k]
            hi = bnd_ref[row + k + 1]
            off = _mult(k * LANES, LANES)
            srow = slot_ref[0, e:e + 1, pl.ds(off, LANES)]
            arow = afft_ref[0, e:e + 1, pl.ds(off, LANES)]
            trow = lane_f + _f32(off)

            def block_body(sb, c2):
                s0 = _mult(sb * LANES, LANES)
                hit = srow == (sub + _f32(s0))
                acc_i_ref[pl.ds(s0, LANES), :] += jnp.where(hit, trow, 0.0)
                acc_g_ref[pl.ds(s0, LANES), :] += jnp.where(hit, arow, 0.0)
                return c2

            lax.fori_loop(lo // LANES, (hi + LANES - 1) // LANES, block_body, 0)
            return carry

        lax.fori_loop(0, n_tiles, tile_body, 0)
        idx_ref[0, e] = jnp.sum(acc_i_ref[...].T, axis=0, keepdims=True).astype(jnp.int32)
        gate_ref[0, e] = jnp.sum(acc_g_ref[...].T, axis=0, keepdims=True)


def _route_call(aff, cap):
    B, T, _ = aff.shape
    E = N_EXPERTS
    assert T // LANES < LANES
    bet = lambda b: (b, 0, 0)
    slot_t, aff_t, bounds = pl.pallas_call(
        functools.partial(_select_kernel, cap=cap),
        out_shape=[jax.ShapeDtypeStruct((B, E, T), F32), jax.ShapeDtypeStruct((B, E, T), F32),
                   jax.ShapeDtypeStruct((B, E, LANES), jnp.int32)],
        grid=(B,),
        in_specs=[pl.BlockSpec((1, T, LANES), bet)],
        out_specs=[pl.BlockSpec((1, E, T), bet), pl.BlockSpec((1, E, T), bet), pl.BlockSpec((1, E, LANES), bet)],
        compiler_params=_cparams(("arbitrary",)),
        name="route_select",
    )(aff)
    bounds_flat = bounds.reshape(-1)
    grid_spec = pltpu.PrefetchScalarGridSpec(
        num_scalar_prefetch=1,
        grid=(B,),
        in_specs=[pl.BlockSpec((1, E, T), lambda b, bnd: (b, 0, 0)),
                  pl.BlockSpec((1, E, T), lambda b, bnd: (b, 0, 0))],
        out_specs=[pl.BlockSpec((1, E, 1, cap), lambda b, bnd: (b, 0, 0, 0)),
                   pl.BlockSpec((1, E, 1, cap), lambda b, bnd: (b, 0, 0, 0))],
        scratch_shapes=[pltpu.VMEM((cap, LANES), F32), pltpu.VMEM((cap, LANES), F32)],
    )
    idx, gate = pl.pallas_call(
        functools.partial(_compact_kernel, cap=cap),
        out_shape=[jax.ShapeDtypeStruct((B, E, 1, cap), jnp.int32), jax.ShapeDtypeStruct((B, E, 1, cap), F32)],
        grid_spec=grid_spec,
        compiler_params=_cparams(("arbitrary",)),
        name="route_compact",
    )(bounds_flat, slot_t, aff_t)
    return slot_t, idx, gate, bounds_flat


FFN_TF = 256
FFN_TN = 512
FFN_SUB = 4


def _ffn_kernel(idx_ref, h2_hbm, gate_ref, wg_ref, wu_ref, wd_ref, y_ref, xs_ref, xb_ref, hmid_ref, sem,
                *, cap, tokens, n_batch, n_steps):
    e = pl.program_id(0)
    b = pl.program_id(1)
    f = pl.program_id(2)
    n_exp = pl.num_programs(0)
    nf = pl.num_programs(2)
    rows_per_step = cap // n_steps

    def row_copy(bb, ee, s):
        t = idx_ref[(bb * n_exp + ee) * cap + s]
        return pltpu.make_async_copy(h2_hbm.at[pl.ds(bb * tokens + t, 1)], xs_ref.at[pl.ds(s, 1)], sem)

    def wait_rows():
        pltpu.make_async_copy(h2_hbm.at[pl.ds(0, cap)], xs_ref, sem).wait()

    @pl.when((e == 0) & (b == 0) & (f == 0))
    def _():
        def issue(s, carry):
            row_copy(b, e, s).start()
            return carry

        lax.fori_loop(0, cap, issue, 0, unroll=8)

    @pl.when(f == 0)
    def _():
        wait_rows()
        xb_ref[...] = xs_ref[...].astype(BF16)

    wrap_b = b + 1 == n_batch
    nb = jnp.where(wrap_b, 0, b + 1)
    ne = jnp.where(wrap_b, jnp.where(e + 1 == n_exp, 0, e + 1), e)
    for k in range(rows_per_step):
        row_copy(nb, ne, f * rows_per_step + k).start()

    tf = wg_ref.shape[2]
    wg = wg_ref[0].astype(BF16)
    wu = wu_ref[0].astype(BF16)
    hcols = pl.ds(_mult(f * tf, tf), tf)
    for s in range(FFN_SUB):
        rows = slice(s * (cap // FFN_SUB), (s + 1) * (cap // FFN_SUB))
        xb = xb_ref[rows, :]
        a = jnp.dot(xb, wg, preferred_element_type=F32)
        u = jnp.dot(xb, wu, preferred_element_type=F32)
        hmid_ref[rows, hcols] = (a * (1.0 / (1.0 + jnp.exp(-a))) * u).astype(BF16)

    @pl.when(f == nf - 1)
    def _():
        hmid = hmid_ref[...]
        gate = jnp.broadcast_to(gate_ref[0, 0], (LANES, cap)).T[:, :1]
        for n in range(y_ref.shape[3] // FFN_TN):
            cols = slice(n * FFN_TN, (n + 1) * FFN_TN)
            part = jnp.dot(hmid, wd_ref[0, :, cols].astype(BF16), preferred_element_type=F32)
            y_ref[0, 0, :, cols] = (part * gate).astype(y_ref.dtype)

    @pl.when((e == n_exp - 1) & (b == n_batch - 1) & (f == nf - 1))
    def _():
        wait_rows()


def _ffn_call(idx_flat, h2, gate, wg, wu, wd, B, T, cap):
    E, D, F = wg.shape
    tf = FFN_TF
    grid_spec = pltpu.PrefetchScalarGridSpec(
        num_scalar_prefetch=1,
        grid=(E, B, F // tf),
        in_specs=[pl.BlockSpec(memory_space=pl.ANY),
                  pl.BlockSpec((1, 1, 1, cap), lambda e, b, f, idx: (b, e, 0, 0)),
                  pl.BlockSpec((1, D, tf), lambda e, b, f, idx: (e, 0, f)),
                  pl.BlockSpec((1, D, tf), lambda e, b, f, idx: (e, 0, f)),
                  pl.BlockSpec((1, F, D), lambda e, b, f, idx: (e, 0, 0))],
        out_specs=pl.BlockSpec((1, 1, cap, D), lambda e, b, f, idx: (b, e, 0, 0)),
        scratch_shapes=[pltpu.VMEM((cap, D), F32), pltpu.VMEM((cap, D), BF16), pltpu.VMEM((cap, F), BF16),
                        pltpu.SemaphoreType.DMA(())],
    )
    return pl.pallas_call(
        functools.partial(_ffn_kernel, cap=cap, tokens=T, n_batch=B, n_steps=F // tf),
        out_shape=jax.ShapeDtypeStruct((B, E, cap, D), BF16),
        grid_spec=grid_spec,
        compiler_params=_cparams(("arbitrary", "arbitrary", "arbitrary")),
        name="expert_ffn",
    )(idx_flat, h2, gate, wg, wu, wd)


CB_WIN = 64
CB_ALIGN = 16


def _combine_kernel(bnd_ref, x1_ref, mod_ref, slot_ref, y_hbm, o_ref, buf_ref, acc_ref, sem,
                    *, cap, nch, n_batch):
    b = pl.program_id(0)
    j = pl.program_id(1)
    E = N_EXPERTS
    step = b * nch + j
    cur = step % 2
    sub = lax.broadcasted_iota(jnp.int32, (CB_WIN, RT_CHUNK), 0).astype(F32)

    def tile_windows(bb, jj):
        per_tile = RT_CHUNK // LANES
        aligned, n_pass = [], jnp.int32(0)
        for e in range(E):
            at = (bb * E + e) * LANES + jj * per_tile
            base = bnd_ref[at]
            cnt = bnd_ref[at + per_tile] - base
            al = (base // CB_ALIGN) * CB_ALIGN
            aligned.append(al)
            n_pass = jnp.maximum(n_pass, (base - al + cnt + CB_WIN - 1) // CB_WIN)
        return aligned, n_pass

    def window(al, c):
        lo = al + c * CB_WIN
        return lo, _mult(jnp.minimum(lo, cap - CB_WIN), CB_ALIGN)

    def start_fetch(bb, aligned, c, slot):
        for e in range(E):
            _, st = window(aligned[e], c)
            pltpu.make_async_copy(y_hbm.at[bb, e, pl.ds(st, CB_WIN)],
                                  buf_ref.at[slot, pl.ds(e * CB_WIN, CB_WIN)], sem.at[slot]).start()

    def wait_fetch(slot):
        pltpu.make_async_copy(buf_ref.at[slot], buf_ref.at[slot], sem.at[slot]).wait()

    def scatter_add(aligned, c, slot):
        pieces = []
        for e in range(E):
            lo, st = window(aligned[e], c)
            srow = slot_ref[0, e:e + 1, :]
            pieces.append(((srow == sub + _f32(st)) & (srow >= _f32(lo))).astype(BF16))
        sel_t = jnp.concatenate(pieces, axis=0)
        return lax.dot_general(sel_t, buf_ref[slot], (((0,), (0,)), ((), ())), preferred_element_type=F32)

    aligned, n_pass = tile_windows(b, j)

    @pl.when(step == 0)
    def _():
        start_fetch(b, aligned, 0, cur)

    @pl.when(step + 1 < n_batch * nch)
    def _():
        wrap = j + 1 == nch
        nb = jnp.where(wrap, b + 1, b)
        nj = jnp.where(wrap, 0, j + 1)
        start_fetch(nb, tile_windows(nb, nj)[0], 0, 1 - cur)

    wait_fetch(cur)
    acc_ref[...] = scatter_add(aligned, 0, cur)

    def extra_pass(c, carry):
        start_fetch(b, aligned, c, cur)
        wait_fetch(cur)
        acc_ref[...] += scatter_add(aligned, c, cur)
        return carry

    lax.fori_loop(1, n_pass, extra_pass, 0)
    o_ref[...] = x1_ref[...] + mod_ref[0, 5:6, :] * acc_ref[...]


def _combine_call(bounds_flat, x1, mod, slot_t, y, B, T, cap):
    D = x1.shape[1]
    nch = T // RT_CHUNK
    E = N_EXPERTS
    grid_spec = pltpu.PrefetchScalarGridSpec(
        num_scalar_prefetch=1,
        grid=(B, nch),
        in_specs=[pl.BlockSpec((RT_CHUNK, D), lambda b, j, bnd: (b * nch + j, 0)),
                  pl.BlockSpec((1, N_MOD, D), lambda b, j, bnd: (b, 0, 0)),
                  pl.BlockSpec((1, E, RT_CHUNK), lambda b, j, bnd: (b, 0, j)),
                  pl.BlockSpec(memory_space=pl.ANY)],
        out_specs=pl.BlockSpec((RT_CHUNK, D), lambda b, j, bnd: (b * nch + j, 0)),
        scratch_shapes=[pltpu.VMEM((2, E * CB_WIN, D), BF16), pltpu.VMEM((RT_CHUNK, D), F32),
                        pltpu.SemaphoreType.DMA((2,))],
    )
    return pl.pallas_call(
        functools.partial(_combine_kernel, cap=cap, nch=nch, n_batch=B),
        out_shape=jax.ShapeDtypeStruct(x1.shape, F32),
        grid_spec=grid_spec,
        compiler_params=_cparams(("arbitrary", "arbitrary")),
        name="combine",
    )(bounds_flat, x1, mod, slot_t, y)


def _rope_tables(n_tokens):
    t = np.arange(n_tokens)
    row = (t // GRID_W).astype(np.float64)
    col = (t % GRID_W).astype(np.float64)
    n_freq = MLA_ROPE // 4
    inv = ROPE_THETA ** (-np.arange(n_freq, dtype=np.float64) / n_freq)
    ang = np.concatenate([row[:, None] * inv, col[:, None] * inv], axis=-1)
    cos, sin = np.cos(ang), np.sin(ang)
    z = np.zeros((n_tokens, LANES - MLA_ROPE))
    return (np.concatenate([cos, cos, z], axis=-1).astype(np.float32),
            np.concatenate([-sin, sin, z], axis=-1).astype(np.float32))


def _pad_lanes(v):
    return jnp.concatenate([v, jnp.zeros((LANES - v.shape[0],), v.dtype)])[None, :]


def kernel(x, c, ctx, c_ctx, w_mod, b_mod, g_norm1, w_in, g_qa, w_qb, g_kva, w_kvb, g_q_mla, g_k_mla,
           g_q_na, g_k_na, rpb_na, g_out_mla, g_out_na, w_out, g_norm2, w_router, w_gate, w_up, w_down):
    B, T, D = x.shape
    C = ctx.shape[1]
    assert w_mod.shape[0] == 1, "single-layer problem"
    assert T % GRID_W == 0 and (T // GRID_W) % NA_QROWS == 0 and T // GRID_W >= 2 * NA_BAND
    assert w_in.shape[2] == 3 * NA_W + MLA_Q_RANK + MLA_KV_RANK + MLA_ROPE
    rows = T // GRID_W
    cap = EC_CAPACITY_FACTOR * T // N_EXPERTS
    half = MLA_ROPE // 2

    w_in_p = _win_call(w_in[0])
    wq3 = w_qb[0].reshape(MLA_Q_RANK, MLA_HEADS, MLA_QK_DIM)
    wq_p = jnp.concatenate([wq3, wq3[..., MLA_NOPE + half:], wq3[..., MLA_NOPE:MLA_NOPE + half]], axis=-1)
    wq_p = wq_p.reshape(MLA_Q_RANK, MLA_HEADS * MLA_QK_PAD).astype(BF16)
    wkv3 = w_kvb[0].reshape(MLA_KV_RANK, MLA_HEADS, MLA_NOPE + MLA_V)
    wk_p = wkv3[..., :MLA_NOPE].reshape(MLA_KV_RANK, MLA_HEADS * MLA_NOPE).astype(BF16)
    wv_p = wkv3[..., MLA_NOPE:].reshape(MLA_KV_RANK, MLA_HEADS * MLA_V).astype(BF16)

    def rope_gains(g):
        gr = g[MLA_NOPE:]
        return g[None, :MLA_NOPE], _pad_lanes(gr), _pad_lanes(jnp.concatenate([gr[half:], gr[:half]]))

    gq0, gqa, gqb = rope_gains(g_q_mla[0])
    gk0, gka, gkb = rope_gains(g_k_mla[0])
    ct, st = _rope_tables(T)
    consts = dict(ct=ct, st=st, g_qa=g_qa, wq=wq_p, g_kva=g_kva, wk=wk_p, wv=wv_p,
                  gq0=gq0, gqa=gqa, gqb=gqb, gk0=gk0, gka=gka, gkb=gkb, g_q_na=g_q_na, g_k_na=g_k_na)
    wr = jnp.concatenate([w_router[0], jnp.zeros((D, LANES - N_EXPERTS), F32)], axis=1)
    wr_hi = wr.astype(BF16)
    wr_split = jnp.concatenate([wr_hi, (wr - wr_hi.astype(F32)).astype(BF16)], axis=1)
    bias = _na_bias_table(rpb_na[0], rows)

    cvec = jnp.concatenate([c, c_ctx[None, :], jnp.zeros((8 - B - 1, D), F32)], axis=0)
    mod = _mod_call(cvec.T, w_mod[0], b_mod, B + 1)
    mod_lat = mod[:B].reshape(B, N_MOD, D)
    mod_ctx = mod[B:].reshape(1, N_MOD, D)

    x2 = x.reshape(B * T, D)
    p_lat = _inproj_call(x2, mod_lat, g_norm1, w_in_p, T, min(INPROJ_TM, T))
    p_ctx = _inproj_call(ctx.reshape(B * C, D), mod_ctx, g_norm1, w_in_p, B * C, C)
    qm, km, vm, qn, kn = _prep_call(p_lat, B, T, min(PREP_TM, T), consts, False)
    kmc, vmc, knc = _prep_call(p_ctx, B, C, C, consts, True)
    o_m = _mla_call(qm, km, vm, kmc, vmc, min(MLA_TQ, T), min(MLA_TK, T))
    o_n = _na_call(qn, kn, p_lat, knc, p_ctx, bias)
    x1, h2, aff = _merge_call(o_m.reshape(B * T, -1), o_n.reshape(B * T, -1), x2, mod_lat,
                              g_out_mla, g_out_na, w_out[0].astype(BF16), g_norm2, wr_split,
                              T, min(MERGE_TM, T))

    slot_t, idx, gate, bounds = _route_call(aff.reshape(B, T, LANES), cap)
    y = _ffn_call(idx.reshape(-1), h2, gate, w_gate[0], w_up[0], w_down[0], B, T, cap)
    out = _combine_call(bounds, x1, mod_lat, slot_t, y, B, T, cap)
    return out.reshape(B, T, D)
```

```python
import functools

import numpy as np
import jax
import jax.numpy as jnp
from jax import lax
from jax.experimental import pallas as pl
from jax.experimental.pallas import tpu as pltpu

F32 = jnp.float32
BF16 = jnp.bfloat16

GRID_W = 64
MLA_HEADS = 8
MLA_NOPE = 128
MLA_ROPE = 64
MLA_QK_DIM = MLA_NOPE + MLA_ROPE
MLA_V = 128
MLA_Q_RANK = 512
MLA_KV_RANK = 256
NA_HEADS = 8
NA_HEAD_DIM = 128
NA_KH = 8
NA_KW = 16
N_EXPERTS = 16
EC_CAPACITY_FACTOR = 2
ROPE_THETA = 10000.0
EPS = 1e-6
N_MOD = 6
MLA_SCALE = MLA_QK_DIM ** -0.5
NA_SCALE = NA_HEAD_DIM ** -0.5
LOG2E = 1.4426950408889634

LANES = 128
MLA_QK_PAD = 2 * LANES
NA_W = NA_HEADS * NA_HEAD_DIM
P_WIDTH = 3 * NA_W + 1024
VMEM_LIMIT = 56 * 1024 * 1024
NEG = -1e30
F32_MIN_NORMAL_BITS = 0x00800000

WIN_TR = 256
MOD_TN = 1024
INPROJ_TM = 512
PREP_TM = 512
MERGE_TM = 512
MLA_TQ = 1024
MLA_TK = 1024


def _f32(v):
    return jnp.asarray(v, dtype=F32)


def _mult(v, m):
    return v if isinstance(v, int) else pl.multiple_of(v, m)


def _cparams(sem, vmem=VMEM_LIMIT, **kw):
    return pltpu.CompilerParams(dimension_semantics=sem, vmem_limit_bytes=vmem, **kw)


def _mod_kernel(ct_ref, w_ref, b_ref, o_ref):
    ct = ct_ref[...]
    s = ct * (1.0 / (1.0 + jnp.exp(-ct)))
    w = w_ref[...]
    rows = []
    for m in range(o_ref.shape[0]):
        rows.append(jnp.sum(w * s[:, m:m + 1], axis=0, keepdims=True))
    o_ref[...] = jnp.concatenate(rows, axis=0) + b_ref[...]


def _mod_call(ct, w_mod, b_mod, n_rows):
    D, N = w_mod.shape
    tn = MOD_TN
    return pl.pallas_call(
        _mod_kernel,
        out_shape=jax.ShapeDtypeStruct((n_rows, N), F32),
        grid=(N // tn,),
        in_specs=[pl.BlockSpec((D, 8), lambda j: (0, 0)),
                  pl.BlockSpec((D, tn), lambda j: (0, j)),
                  pl.BlockSpec((1, tn), lambda j: (0, j))],
        out_specs=pl.BlockSpec((n_rows, tn), lambda j: (0, j)),
        compiler_params=_cparams(("arbitrary",)),
        name="mod",
    )(ct, w_mod, b_mod)


def _win_kernel(w_ref, o_ref):
    w = w_ref[...]
    half = MLA_ROPE // 2
    o_pe = MLA_Q_RANK + MLA_KV_RANK
    o_na = o_pe + MLA_ROPE
    pieces = [w[:, o_na:], w[:, :o_pe], w[:, o_pe:o_na], w[:, o_pe + half:o_na], w[:, o_pe:o_pe + half],
              jnp.zeros((w.shape[0], P_WIDTH - w.shape[1] - MLA_ROPE), F32)]
    o_ref[...] = jnp.concatenate(pieces, axis=1).astype(BF16)


def _win_call(w):
    D, N = w.shape
    tr = WIN_TR
    return pl.pallas_call(
        _win_kernel,
        out_shape=jax.ShapeDtypeStruct((D, P_WIDTH), BF16),
        grid=(D // tr,),
        in_specs=[pl.BlockSpec((tr, N), lambda i: (i, 0))],
        out_specs=pl.BlockSpec((tr, P_WIDTH), lambda i: (i, 0)),
        compiler_params=_cparams(("arbitrary",)),
        name="w_in_layout",
    )(w)


INPROJ_SUB = 2
INPROJ_TN = 1024


def _inproj_kernel(x_ref, mod_ref, g_ref, w_ref, o_ref):
    ts = x_ref.shape[0] // INPROJ_SUB
    for s in range(INPROJ_SUB):
        rows = slice(s * ts, (s + 1) * ts)
        x = x_ref[rows, :]
        y = x * lax.rsqrt(jnp.mean(x * x, axis=-1, keepdims=True) + EPS) * g_ref[...]
        h = (y * (1.0 + mod_ref[0, 1:2, :]) + mod_ref[0, 0:1, :]).astype(BF16)
        for n in range(w_ref.shape[1] // INPROJ_TN):
            cols = slice(n * INPROJ_TN, (n + 1) * INPROJ_TN)
            o_ref[rows, cols] = jnp.dot(h, w_ref[:, cols], preferred_element_type=F32).astype(BF16)


def _inproj_call(x2, mod, g, w, rows_per_mod, tm):
    M, D = x2.shape
    N = w.shape[1]
    per = rows_per_mod // tm
    return pl.pallas_call(
        _inproj_kernel,
        out_shape=jax.ShapeDtypeStruct((M, N), BF16),
        grid=(M // tm,),
        in_specs=[pl.BlockSpec((tm, D), lambda i: (i, 0)),
                  pl.BlockSpec((1, N_MOD, D), lambda i: (i // per, 0, 0)),
                  pl.BlockSpec((1, D), lambda i: (0, 0)),
                  pl.BlockSpec((D, N), lambda i: (0, 0), pipeline_mode=pl.Buffered(1))],
        out_specs=pl.BlockSpec((tm, N), lambda i: (i, 0)),
        compiler_params=_cparams(("arbitrary",)),
        name="inproj",
    )(x2, mod, g, w)


def _rms(x, g):
    return x * lax.rsqrt(jnp.mean(x * x, axis=-1, keepdims=True) + EPS) * g


def _prep_kernel(*refs, is_ctx):
    if is_ctx:
        (pk_ref, pm_ref, gkva_ref, wk_ref, wv_ref, gk0_ref, gka_ref, gkn_ref,
         km_ref, vm_ref, kn_ref) = refs
    else:
        (pq_ref, pk_ref, pm_ref, ct_ref, st_ref, gqa_ref, wq_ref, gkva_ref, wk_ref, wv_ref,
         gq0_ref, gqa2_ref, gqb2_ref, gk0_ref, gka_ref, gkb_ref, gqn_ref, gkn_ref,
         qm_ref, km_ref, vm_ref, qn_ref, kn_ref) = refs
    tm = pm_ref.shape[0]
    lane = lax.broadcasted_iota(jnp.int32, (1, LANES), 1)
    rope_mask = (lane < MLA_ROPE).astype(F32)
    pm = pm_ref[...].astype(F32)
    inv_qk = 1.0 / MLA_QK_DIM

    ckvn = _rms(pm[:, MLA_Q_RANK:MLA_Q_RANK + MLA_KV_RANK], gkva_ref[...]).astype(BF16)
    kn = jnp.dot(ckvn, wk_ref[...], preferred_element_type=F32)
    vv = jnp.dot(ckvn, wv_ref[...], preferred_element_type=F32)
    kpe = pm[:, MLA_Q_RANK + MLA_KV_RANK:MLA_Q_RANK + MLA_KV_RANK + LANES]
    ss_pe = jnp.sum(kpe * kpe * rope_mask, axis=-1, keepdims=True)
    if is_ctx:
        k_rope = kpe * gka_ref[...]
    else:
        ct = ct_ref[...]
        st = st_ref[...]
        k_rope = kpe * (ct * gka_ref[...]) + pltpu.roll(kpe, MLA_ROPE, 1) * (st * gkb_ref[...])
    for h in range(MLA_HEADS):
        k0 = kn[:, h * LANES:(h + 1) * LANES]
        r = lax.rsqrt((jnp.sum(k0 * k0, axis=-1, keepdims=True) + ss_pe) * inv_qk + EPS)
        km_ref[0, h] = jnp.concatenate([k0 * r * gk0_ref[...], k_rope * r], axis=-1).astype(BF16)
        vm_ref[0, h] = jnp.concatenate([vv[:, h * LANES:(h + 1) * LANES], jnp.ones((tm, LANES), F32)],
                                       axis=-1).astype(BF16)

    pk = pk_ref[...].astype(F32)
    for h in range(NA_HEADS):
        kn_ref[0, h] = _rms(pk[:, h * LANES:(h + 1) * LANES], gkn_ref[...]).astype(BF16)

    if is_ctx:
        return

    cqn = _rms(pm[:, :MLA_Q_RANK], gqa_ref[...]).astype(BF16)
    q = jnp.dot(cqn, wq_ref[...], preferred_element_type=F32)
    qa = ct * gqa2_ref[...]
    qb = st * gqb2_ref[...]
    for h in range(MLA_HEADS):
        t0 = q[:, h * MLA_QK_PAD:h * MLA_QK_PAD + LANES]
        t1 = q[:, h * MLA_QK_PAD + LANES:(h + 1) * MLA_QK_PAD]
        ss = jnp.sum(t0 * t0 + t1 * t1 * rope_mask, axis=-1, keepdims=True)
        r = lax.rsqrt(ss * inv_qk + EPS) * (MLA_SCALE * LOG2E)
        o1 = t1 * qa + pltpu.roll(t1, MLA_ROPE, 1) * qb
        qm_ref[0, h] = jnp.concatenate([t0 * r * gq0_ref[...], o1 * r], axis=-1).astype(BF16)

    pq = pq_ref[...].astype(F32)
    for h in range(NA_HEADS):
        qn_ref[0, h] = (_rms(pq[:, h * LANES:(h + 1) * LANES], gqn_ref[...]) * (NA_SCALE * LOG2E)).astype(BF16)


def _prep_call(p, nb, tok, tm, consts, is_ctx):
    per = tok // tm
    H = MLA_HEADS
    row = lambda i: (i, 0)
    full = lambda i: (0, 0)
    hm = lambda i: (i // per, 0, i % per, 0)
    p_spec = lambda c: pl.BlockSpec((tm, 1024), lambda i, c=c: (i, c))
    vec = lambda n: pl.BlockSpec((1, n), full)
    km_s = jax.ShapeDtypeStruct((nb, H, tok, MLA_QK_PAD), BF16)
    h128_s = jax.ShapeDtypeStruct((nb, H, tok, LANES), BF16)
    km_o = pl.BlockSpec((1, H, tm, MLA_QK_PAD), hm)
    h128_o = pl.BlockSpec((1, H, tm, LANES), hm)
    c = consts
    if is_ctx:
        ins = [p, p, c["g_kva"], c["wk"], c["wv"], c["gk0"], c["gka"], c["g_k_na"]]
        in_specs = [p_spec(1), p_spec(3), vec(MLA_KV_RANK),
                    pl.BlockSpec(c["wk"].shape, full), pl.BlockSpec(c["wv"].shape, full),
                    vec(LANES), vec(LANES), vec(LANES)]
        out_shape = [km_s, km_s, h128_s]
        out_specs = [km_o, km_o, h128_o]
    else:
        ins = [p, p, p, c["ct"], c["st"], c["g_qa"], c["wq"], c["g_kva"], c["wk"], c["wv"],
               c["gq0"], c["gqa"], c["gqb"], c["gk0"], c["gka"], c["gkb"], c["g_q_na"], c["g_k_na"]]
        tab = pl.BlockSpec((tm, LANES), lambda i: (i % per, 0))
        in_specs = [p_spec(0), p_spec(1), p_spec(3), tab, tab, vec(MLA_Q_RANK),
                    pl.BlockSpec(c["wq"].shape, full), vec(MLA_KV_RANK),
                    pl.BlockSpec(c["wk"].shape, full), pl.BlockSpec(c["wv"].shape, full),
                    vec(LANES), vec(LANES), vec(LANES), vec(LANES), vec(LANES), vec(LANES),
                    vec(LANES), vec(LANES)]
        out_shape = [km_s, km_s, km_s, h128_s, h128_s]
        out_specs = [km_o, km_o, km_o, h128_o, h128_o]
    return pl.pallas_call(
        functools.partial(_prep_kernel, is_ctx=is_ctx),
        out_shape=out_shape,
        grid=(nb * per,),
        in_specs=in_specs,
        out_specs=out_specs,
        compiler_params=_cparams(("arbitrary",)),
        name="prep_ctx" if is_ctx else "prep_lat",
    )(*ins)


MLA_UNROLL = 4
MLA_SUB = 2


def _mla_kernel(q_ref, k_ref, v_ref, kc_ref, vc_ref, o_ref, m_ref, acc_ref,
                sa_ref, sb_ref, sc_ref, mxa_ref, mxb_ref, mxc_ref, *, tq, tk):
    ts = tq // MLA_SUB
    T = k_ref.shape[2]
    n = T // tk
    nq = T // tq
    dn = (((1,), (1,)), ((), ()))

    def scores_into(t, keys, s_ref, mx_ref):
        q = q_ref[0, 0, pl.ds(_mult(t * tq, tq), tq), :]
        s = lax.dot_general(q, keys, dn, preferred_element_type=F32)
        s_ref[...] = s
        mx_ref[...] = jnp.broadcast_to(jnp.max(s, axis=-1, keepdims=True), mx_ref.shape)

    def keys(c):
        return k_ref[0, 0, pl.ds(_mult(c * tk, tk), tk), :]

    def values(c):
        return v_ref[0, 0, pl.ds(_mult(c * tk, tk), tk), :]

    def softmax_pv(s_ref, mx_ref, v):
        for u in range(MLA_SUB):
            rows = slice(u * ts, (u + 1) * ts)
            m_old = m_ref[rows, :]
            m_new = jnp.maximum(m_old, mx_ref[rows, :])
            alpha = jnp.exp2(m_old - m_new)
            p = jnp.concatenate([jnp.exp2(s_ref[rows, j * LANES:(j + 1) * LANES] - m_new)
                                 for j in range(s_ref.shape[1] // LANES)], axis=-1).astype(BF16)
            acc_ref[rows, :] = (jnp.concatenate([alpha] * (acc_ref.shape[1] // LANES), axis=-1) * acc_ref[rows, :]
                                + jnp.dot(p, v, preferred_element_type=F32))
            m_ref[rows, :] = m_new

    scores_into(0, keys(0), sa_ref, mxa_ref)

    def tile_body(t, carry):
        m_ref[...] = jnp.full(m_ref.shape, NEG, F32)
        acc_ref[...] = jnp.zeros(acc_ref.shape, F32)

        bufs = ((sa_ref, mxa_ref), (sb_ref, mxb_ref))

        def group_body(i, c2):
            c0 = MLA_UNROLL * i
            for u in range(MLA_UNROLL):
                scores_into(t, keys(c0 + u + 1), *bufs[(u + 1) % 2])
                softmax_pv(*bufs[u % 2], values(c0 + u))
            return c2

        n_loop = (n - 2) // MLA_UNROLL * MLA_UNROLL
        lax.fori_loop(0, n_loop // MLA_UNROLL, group_body, 0)
        for c in range(n_loop, n):
            if c + 1 < n:
                scores_into(t, keys(c + 1), *bufs[(c + 1) % 2])
            else:
                scores_into(t, kc_ref[0, 0], sc_ref, mxc_ref)
            softmax_pv(*bufs[c % 2], values(c))
        scores_into(jnp.minimum(t + 1, nq - 1), keys(0), sa_ref, mxa_ref)
        softmax_pv(sc_ref, mxc_ref, vc_ref[0, 0])
        o_ref[0, pl.ds(_mult(t * tq, tq), tq), :] = (acc_ref[:, :MLA_V] / acc_ref[:, MLA_V:]).astype(o_ref.dtype)
        return carry

    lax.fori_loop(0, nq, tile_body, 0)


def _mla_call(qm, km, vm, kmc, vmc, tq, tk):
    B, H, T, _ = qm.shape
    C = kmc.shape[2]
    VW = vm.shape[3]
    assert (T // tk) % 2 == 0 and T % tq == 0
    bh = lambda b, h: (b, h, 0, 0)
    return pl.pallas_call(
        functools.partial(_mla_kernel, tq=tq, tk=tk),
        out_shape=jax.ShapeDtypeStruct((B, T, H * MLA_V), BF16),
        grid=(B, H),
        in_specs=[pl.BlockSpec((1, 1, T, MLA_QK_PAD), bh),
                  pl.BlockSpec((1, 1, T, MLA_QK_PAD), bh),
                  pl.BlockSpec((1, 1, T, VW), bh),
                  pl.BlockSpec((1, 1, C, MLA_QK_PAD), bh),
                  pl.BlockSpec((1, 1, C, VW), bh)],
        out_specs=pl.BlockSpec((1, T, MLA_V), lambda b, h: (b, 0, h)),
        scratch_shapes=[pltpu.VMEM((tq, LANES), F32), pltpu.VMEM((tq, VW), F32),
                        pltpu.VMEM((tq, tk), F32), pltpu.VMEM((tq, tk), F32), pltpu.VMEM((tq, C), F32),
                        pltpu.VMEM((tq, LANES), F32), pltpu.VMEM((tq, LANES), F32), pltpu.VMEM((tq, LANES), F32)],
        compiler_params=_cparams(("arbitrary", "arbitrary")),
        name="mla_attn",
    )(qm, km, vm, kmc, vmc)


NA_QROWS = 4
NA_KROWS = 2
NA_QTILE = NA_QROWS * GRID_W
NA_KTILE = NA_KROWS * GRID_W
NA_BAND = 6
NA_CFGS = 3


def _na_band_start(i, nkt):
    return jnp.clip((NA_QROWS // NA_KROWS) * i - NA_KH // 2 // NA_KROWS, 0, nkt - NA_BAND)


def _na_kernel(q_ref, *refs):
    k_refs = refs[:NA_BAND]
    v_refs = refs[NA_BAND:2 * NA_BAND]
    kc_ref, vc_ref, bias_ref, o_ref, sl_ref, sx_ref = refs[2 * NA_BAND:]
    dn = (((1,), (1,)), ((), ()))

    def scores(h, slot):
        q = q_ref[0, h]
        kcat = jnp.concatenate([r[0, h] for r in k_refs], axis=0)
        sl_ref[slot] = lax.dot_general(q, kcat, dn, preferred_element_type=F32) + bias_ref[0, h]
        sx_ref[slot] = lax.dot_general(q, kc_ref[0, h], dn, preferred_element_type=F32)

    def softmax_pv(h, slot):
        s_loc = sl_ref[slot]
        s_ctx = sx_ref[slot]
        m = jnp.maximum(jnp.max(s_loc, axis=-1, keepdims=True), jnp.max(s_ctx, axis=-1, keepdims=True))
        p_loc = jnp.exp2(s_loc - m).astype(BF16)
        p_ctx = jnp.exp2(s_ctx - m).astype(BF16)
        cols = slice(h * LANES, (h + 1) * LANES)
        vcat = jnp.concatenate([r[:, cols] for r in v_refs], axis=0)
        vcat = jnp.concatenate([vcat, jnp.ones(vcat.shape, BF16)], axis=-1)
        vctx = jnp.concatenate([vc_ref[:, cols], jnp.ones((vc_ref.shape[0], LANES), BF16)], axis=-1)
        o = (jnp.dot(p_loc, vcat, preferred_element_type=F32) + jnp.dot(p_ctx, vctx, preferred_element_type=F32))
        o_ref[0, :, cols] = (o[:, :LANES] / o[:, LANES:]).astype(o_ref.dtype)

    scores(0, 0)
    for h in range(NA_HEADS):
        if h + 1 < NA_HEADS:
            scores(h + 1, (h + 1) % 2)
        softmax_pv(h, h % 2)


def _na_call(qn, kn, p_lat, knc, p_ctx, bias):
    B, H, T, _ = qn.shape
    C = knc.shape[2]
    nblk = T // NA_QTILE
    nkt = T // NA_KTILE
    assert nblk >= 3 and nkt >= NA_BAND

    def cfg(i):
        return jnp.minimum(i, 1) + jnp.maximum(i - (nblk - 2), 0)

    k_specs = [pl.BlockSpec((1, H, NA_KTILE, LANES),
                            lambda b, i, u=u: (b, 0, _na_band_start(i, nkt) + u, 0)) for u in range(NA_BAND)]
    v_specs = [pl.BlockSpec((NA_KTILE, NA_W),
                            lambda b, i, u=u: (b * nkt + _na_band_start(i, nkt) + u, 2))
               for u in range(NA_BAND)]
    return pl.pallas_call(
        _na_kernel,
        out_shape=jax.ShapeDtypeStruct((B, T, NA_W), BF16),
        grid=(B, nblk),
        in_specs=[pl.BlockSpec((1, H, NA_QTILE, LANES), lambda b, i: (b, 0, i, 0))] + k_specs + v_specs + [
            pl.BlockSpec((1, H, C, LANES), lambda b, i: (b, 0, 0, 0)),
            pl.BlockSpec((C, NA_W), lambda b, i: (b, 2)),
            pl.BlockSpec((1, H, NA_QTILE, NA_BAND * NA_KTILE), lambda b, i: (cfg(i), 0, 0, 0))],
        out_specs=pl.BlockSpec((1, NA_QTILE, NA_W), lambda b, i: (b, i, 0)),
        scratch_shapes=[pltpu.VMEM((2, NA_QTILE, NA_BAND * NA_KTILE), F32), pltpu.VMEM((2, NA_QTILE, C), F32)],
        compiler_params=_cparams(("arbitrary", "arbitrary")),
        name="na_attn",
    )(qn, *([kn] * NA_BAND), *([p_lat] * NA_BAND), knc, p_ctx, bias)


def _na_bias_table(rpb, rows):
    nblk = rows // NA_QROWS
    nkt = rows // NA_KROWS
    reps = [0, 1, nblk - 1]
    q_r = np.arange(NA_QTILE) // GRID_W
    q_c = np.arange(NA_QTILE) % GRID_W
    k_r = np.arange(NA_BAND * NA_KTILE) // GRID_W
    k_c = np.arange(NA_BAND * NA_KTILE) % GRID_W
    n_ro, n_co = 2 * NA_KH - 1, 2 * NA_KW - 1
    col_sel = (np.arange(n_co)[:, None, None]
               == np.arange(GRID_W)[None, None, :] - np.arange(GRID_W)[None, :, None] + NA_KW - 1)
    toep = jnp.einsum('hrd,dqk->hrqk', rpb, col_sel.astype(np.float32), precision=lax.Precision.HIGHEST)
    toep = jnp.pad(toep, ((0, 0), (1, 1), (0, 0), (0, 0)))
    pair = jnp.concatenate([toep[:, :-1], toep[:, 1:]], axis=-1)
    tables, ok = [], []
    for i in reps:
        j0 = int(np.clip((NA_QROWS // NA_KROWS) * i - NA_KH // 2 // NA_KROWS, 0, nkt - NA_BAND))
        r = NA_QROWS * i + q_r
        rs = np.clip(r - NA_KH // 2, 0, rows - NA_KH)
        cs = np.clip(q_c - NA_KW // 2, 0, GRID_W - NA_KW)
        kr = NA_KROWS * j0 + k_r
        ok.append((kr[None, :] >= rs[:, None]) & (kr[None, :] < rs[:, None] + NA_KH)
                  & (k_c[None, :] >= cs[:, None]) & (k_c[None, :] < cs[:, None] + NA_KW))
        per_qr = []
        for qr in range(NA_QROWS):
            pieces = []
            for kt in range(NA_BAND):
                ro = NA_KROWS * (j0 + kt) - (NA_QROWS * i + qr) + NA_KH - 1
                pieces.append(pair[:, int(np.clip(ro + 1, 0, n_ro))])
            per_qr.append(jnp.concatenate(pieces, axis=-1))
        tables.append(jnp.concatenate(per_qr, axis=1))
    return jnp.where(np.stack(ok)[:, None], jnp.stack(tables) * LOG2E, NEG).astype(F32)


MERGE_SUB = 2


def _merge_kernel(om_ref, on_ref, x_ref, mod_ref, gom_ref, gon_ref, wout_ref, g2_ref, wr_ref,
                  x1_ref, h2_ref, aff_ref):
    ts = x_ref.shape[0] // MERGE_SUB
    for s in range(MERGE_SUB):
        rows = slice(s * ts, (s + 1) * ts)
        a = _rms(om_ref[rows, :].astype(F32), gom_ref[...]).astype(BF16)
        b = _rms(on_ref[rows, :].astype(F32), gon_ref[...]).astype(BF16)
        y = jnp.dot(jnp.concatenate([a, b], axis=-1), wout_ref[...], preferred_element_type=F32)
        x1 = x_ref[rows, :] + mod_ref[0, 2:3, :] * y
        x1_ref[rows, :] = x1
        h2 = _rms(x1, g2_ref[...]) * (1.0 + mod_ref[0, 4:5, :]) + mod_ref[0, 3:4, :]
        h2_ref[rows, :] = h2
        hi = h2.astype(BF16)
        lo = (h2 - hi.astype(F32)).astype(BF16)
        logits = (jnp.dot(hi, wr_ref[:, :LANES], preferred_element_type=F32)
                  + jnp.dot(lo, wr_ref[:, :LANES], preferred_element_type=F32)
                  + jnp.dot(hi, wr_ref[:, LANES:], preferred_element_type=F32))
        lane = lax.broadcasted_iota(jnp.int32, logits.shape, 1)
        logits = jnp.where(lane < N_EXPERTS, logits, NEG)
        e = jnp.exp(logits - jnp.max(logits, axis=-1, keepdims=True))
        aff_ref[rows, :] = e / jnp.sum(e, axis=-1, keepdims=True)


def _merge_call(om, on, x2, mod, gom, gon, wout, g2, wr, rows_per_mod, tm):
    M, D = x2.shape
    per = rows_per_mod // tm
    row = lambda i: (i, 0)
    full = lambda i: (0, 0)
    return pl.pallas_call(
        _merge_kernel,
        out_shape=[jax.ShapeDtypeStruct((M, D), F32), jax.ShapeDtypeStruct((M, D), F32),
                   jax.ShapeDtypeStruct((M, LANES), F32)],
        grid=(M // tm,),
        in_specs=[pl.BlockSpec((tm, om.shape[1]), row), pl.BlockSpec((tm, on.shape[1]), row),
                  pl.BlockSpec((tm, D), row),
                  pl.BlockSpec((1, N_MOD, D), lambda i: (i // per, 0, 0)),
                  pl.BlockSpec((1, om.shape[1]), full), pl.BlockSpec((1, on.shape[1]), full),
                  pl.BlockSpec(wout.shape, full), pl.BlockSpec((1, D), full),
                  pl.BlockSpec(wr.shape, full)],
        out_specs=[pl.BlockSpec((tm, D), row), pl.BlockSpec((tm, D), row), pl.BlockSpec((tm, LANES), row)],
        compiler_params=_cparams(("arbitrary",)),
        name="merge_router",
    )(om, on, x2, mod, gom, gon, wout, g2, wr)


RT_CHUNK = 256


def _select_kernel(aff_ref, slot_ref, afft_ref, bounds_ref, *, cap):
    T = aff_ref.shape[1]
    E = N_EXPERTS
    n_tiles = T // LANES
    capf = float(cap)

    def tr_body(c, carry):
        off = _mult(c * RT_CHUNK, RT_CHUNK)
        afft_ref[0, :, pl.ds(off, RT_CHUNK)] = aff_ref[0, pl.ds(off, RT_CHUNK), :].T[:E]
        return carry

    lax.fori_loop(0, T // RT_CHUNK, tr_body, 0)

    def as_float(bits):
        return lax.bitcast_convert_type(bits, F32)

    def count_ge(v):
        return jnp.sum((afft_ref[0] >= v).astype(F32), axis=1, keepdims=True)

    def bs_body(i, prefix):
        cand = prefix | jnp.left_shift(jnp.int32(1), 30 - i)
        return jnp.where(count_ge(as_float(cand)) >= capf, cand, prefix)

    thr_bits = lax.fori_loop(0, 31, bs_body, jnp.zeros((E, 1), jnp.int32))
    thr = as_float(thr_bits)
    above = as_float(jnp.maximum(thr_bits + 1, jnp.int32(F32_MIN_NORMAL_BITS)))
    need = capf - count_ge(above)

    ri = lax.broadcasted_iota(jnp.int32, (LANES, LANES), 0)
    ci = lax.broadcasted_iota(jnp.int32, (LANES, LANES), 1)
    upper = (ri <= ci).astype(BF16)
    lane = lax.broadcasted_iota(jnp.int32, (E, LANES), 1)
    run_eq = jnp.zeros((E, 1), F32)
    run_sel = jnp.zeros((E, 1), F32)
    bounds = jnp.zeros((E, LANES), F32)
    for k in range(n_tiles):
        a = afft_ref[0, :, k * LANES:(k + 1) * LANES]
        gt = a >= above
        eq = (a >= thr) & (a < above)
        eqf = eq.astype(F32)
        incl_eq = jnp.dot(eqf.astype(BF16), upper, preferred_element_type=F32)
        sel = gt | (eq & ((incl_eq - eqf + run_eq) < need))
        self_ = sel.astype(F32)
        incl_sel = jnp.dot(self_.astype(BF16), upper, preferred_element_type=F32)
        slot_ref[0, :, k * LANES:(k + 1) * LANES] = jnp.where(sel, incl_sel - self_ + run_sel, -1.0)
        bounds = jnp.where(lane == k, run_sel, bounds)
        run_eq = run_eq + incl_eq[:, LANES - 1:LANES]
        run_sel = run_sel + incl_sel[:, LANES - 1:LANES]
    bounds_ref[0] = jnp.where(lane >= n_tiles, run_sel, bounds).astype(jnp.int32)


def _compact_kernel(bnd_ref, slot_ref, afft_ref, idx_ref, gate_ref, acc_i_ref, acc_g_ref, *, cap):
    b = pl.program_id(0)
    T = slot_ref.shape[2]
    n_tiles = T // LANES
    sub = lax.broadcasted_iota(jnp.int32, (LANES, LANES), 0).astype(F32)
    lane_f = lax.broadcasted_iota(jnp.int32, (1, LANES), 1).astype(F32)

    for e in range(N_EXPERTS):
        acc_i_ref[...] = jnp.zeros(acc_i_ref.shape, F32)
        acc_g_ref[...] = jnp.zeros(acc_g_ref.shape, F32)
        row = (b * N_EXPERTS + e) * LANES

        def tile_body(k, carry, e=e, row=row):
            lo = bnd_ref[row + k]
            hi = bnd_ref[row + k + 1]
            off = _mult(k * LANES, LANES)
            srow = slot_ref[0, e:e + 1, pl.ds(off, LANES)]
            arow = afft_ref[0, e:e + 1, pl.ds(off, LANES)]
            trow = lane_f + _f32(off)

            def block_body(sb, c2):
                s0 = _mult(sb * LANES, LANES)
                hit = srow == (sub + _f32(s0))
                acc_i_ref[pl.ds(s0, LANES), :] += jnp.where(hit, trow, 0.0)
                acc_g_ref[pl.ds(s0, LANES), :] += jnp.where(hit, arow, 0.0)
                return c2

            lax.fori_loop(lo // LANES, (hi + LANES - 1) // LANES, block_body, 0)
            return carry

        lax.fori_loop(0, n_tiles, tile_body, 0)
        idx_ref[0, e] = jnp.sum(acc_i_ref[...].T, axis=0, keepdims=True).astype(jnp.int32)
        gate_ref[0, e] = jnp.sum(acc_g_ref[...].T, axis=0, keepdims=True)


def _route_call(aff, cap):
    B, T, _ = aff.shape
    E = N_EXPERTS
    assert T // LANES < LANES
    bet = lambda b: (b, 0, 0)
    slot_t, aff_t, bounds = pl.pallas_call(
        functools.partial(_select_kernel, cap=cap),
        out_shape=[jax.ShapeDtypeStruct((B, E, T), F32), jax.ShapeDtypeStruct((B, E, T), F32),
                   jax.ShapeDtypeStruct((B, E, LANES), jnp.int32)],
        grid=(B,),
        in_specs=[pl.BlockSpec((1, T, LANES), bet)],
        out_specs=[pl.BlockSpec((1, E, T), bet), pl.BlockSpec((1, E, T), bet), pl.BlockSpec((1, E, LANES), bet)],
        compiler_params=_cparams(("arbitrary",)),
        name="route_select",
    )(aff)
    bounds_flat = bounds.reshape(-1)
    grid_spec = pltpu.PrefetchScalarGridSpec(
        num_scalar_prefetch=1,
        grid=(B,),
        in_specs=[pl.BlockSpec((1, E, T), lambda b, bnd: (b, 0, 0)),
                  pl.BlockSpec((1, E, T), lambda b, bnd: (b, 0, 0))],
        out_specs=[pl.BlockSpec((1, E, 1, cap), lambda b, bnd: (b, 0, 0, 0)),
                   pl.BlockSpec((1, E, 1, cap), lambda b, bnd: (b, 0, 0, 0))],
        scratch_shapes=[pltpu.VMEM((cap, LANES), F32), pltpu.VMEM((cap, LANES), F32)],
    )
    idx, gate = pl.pallas_call(
        functools.partial(_compact_kernel, cap=cap),
        out_shape=[jax.ShapeDtypeStruct((B, E, 1, cap), jnp.int32), jax.ShapeDtypeStruct((B, E, 1, cap), F32)],
        grid_spec=grid_spec,
        compiler_params=_cparams(("arbitrary",)),
        name="route_compact",
    )(bounds_flat, slot_t, aff_t)
    return slot_t, idx, gate, bounds_flat


FFN_TF = 256
FFN_TN = 512
FFN_SUB = 8


def _ffn_kernel(idx_ref, h2_hbm, gate_ref, wg_ref, wu_ref, wd_ref, y_ref, xs_ref, xb_ref, hmid_ref, sem,
                *, cap, tokens, n_batch, n_steps):
    e = pl.program_id(0)
    b = pl.program_id(1)
    f = pl.program_id(2)
    n_exp = pl.num_programs(0)
    nf = pl.num_programs(2)
    rows_per_step = cap // n_steps

    def row_copy(bb, ee, s):
        t = idx_ref[(bb * n_exp + ee) * cap + s]
        return pltpu.make_async_copy(h2_hbm.at[pl.ds(bb * tokens + t, 1)], xs_ref.at[pl.ds(s, 1)], sem)

    def wait_rows():
        pltpu.make_async_copy(h2_hbm.at[pl.ds(0, cap)], xs_ref, sem).wait()

    @pl.when((e == 0) & (b == 0) & (f == 0))
    def _():
        def issue(s, carry):
            row_copy(b, e, s).start()
            return carry

        lax.fori_loop(0, cap, issue, 0, unroll=8)

    @pl.when(f == 0)
    def _():
        wait_rows()
        xb_ref[...] = xs_ref[...].astype(BF16)

    wrap_b = b + 1 == n_batch
    nb = jnp.where(wrap_b, 0, b + 1)
    ne = jnp.where(wrap_b, jnp.where(e + 1 == n_exp, 0, e + 1), e)
    for k in range(rows_per_step):
        row_copy(nb, ne, f * rows_per_step + k).start()

    tf = wg_ref.shape[2]
    wg = wg_ref[0].astype(BF16)
    wu = wu_ref[0].astype(BF16)
    hcols = pl.ds(_mult(f * tf, tf), tf)
    for s in range(FFN_SUB):
        rows = slice(s * (cap // FFN_SUB), (s + 1) * (cap // FFN_SUB))
        xb = xb_ref[rows, :]
        a = jnp.dot(xb, wg, preferred_element_type=F32)
        u = jnp.dot(xb, wu, preferred_element_type=F32)
        hmid_ref[rows, hcols] = (a * (1.0 / (1.0 + jnp.exp(-a))) * u).astype(BF16)

    @pl.when(f == nf - 1)
    def _():
        hmid = hmid_ref[...]
        gate = jnp.broadcast_to(gate_ref[0, 0], (LANES, cap)).T[:, :1]
        for n in range(y_ref.shape[3] // FFN_TN):
            cols = slice(n * FFN_TN, (n + 1) * FFN_TN)
            part = jnp.dot(hmid, wd_ref[0, :, cols].astype(BF16), preferred_element_type=F32)
            y_ref[0, 0, :, cols] = (part * gate).astype(y_ref.dtype)

    @pl.when((e == n_exp - 1) & (b == n_batch - 1) & (f == nf - 1))
    def _():
        wait_rows()


def _ffn_call(idx_flat, h2, gate, wg, wu, wd, B, T, cap):
    E, D, F = wg.shape
    tf = FFN_TF
    grid_spec = pltpu.PrefetchScalarGridSpec(
        num_scalar_prefetch=1,
        grid=(E, B, F // tf),
        in_specs=[pl.BlockSpec(memory_space=pl.ANY),
                  pl.BlockSpec((1, 1, 1, cap), lambda e, b, f, idx: (b, e, 0, 0)),
                  pl.BlockSpec((1, D, tf), lambda e, b, f, idx: (e, 0, f)),
                  pl.BlockSpec((1, D, tf), lambda e, b, f, idx: (e, 0, f)),
                  pl.BlockSpec((1, F, D), lambda e, b, f, idx: (e, 0, 0))],
        out_specs=pl.BlockSpec((1, 1, cap, D), lambda e, b, f, idx: (b, e, 0, 0)),
        scratch_shapes=[pltpu.VMEM((cap, D), F32), pltpu.VMEM((cap, D), BF16), pltpu.VMEM((cap, F), BF16),
                        pltpu.SemaphoreType.DMA(())],
    )
    return pl.pallas_call(
        functools.partial(_ffn_kernel, cap=cap, tokens=T, n_batch=B, n_steps=F // tf),
        out_shape=jax.ShapeDtypeStruct((B, E, cap, D), BF16),
        grid_spec=grid_spec,
        compiler_params=_cparams(("arbitrary", "arbitrary", "arbitrary")),
        name="expert_ffn",
    )(idx_flat, h2, gate, wg, wu, wd)


CB_WIN = 64
CB_ALIGN = 16


def _combine_kernel(bnd_ref, x1_ref, mod_ref, slot_ref, y_hbm, o_ref, buf_ref, acc_ref, sem,
                    *, cap, nch, n_batch):
    b = pl.program_id(0)
    j = pl.program_id(1)
    E = N_EXPERTS
    step = b * nch + j
    cur = step % 2
    sub = lax.broadcasted_iota(jnp.int32, (CB_WIN, RT_CHUNK), 0).astype(F32)

    def tile_windows(bb, jj):
        per_tile = RT_CHUNK // LANES
        aligned, n_pass = [], jnp.int32(0)
        for e in range(E):
            at = (bb * E + e) * LANES + jj * per_tile
            base = bnd_ref[at]
            cnt = bnd_ref[at + per_tile] - base
            al = (base // CB_ALIGN) * CB_ALIGN
            aligned.append(al)
            n_pass = jnp.maximum(n_pass, (base - al + cnt + CB_WIN - 1) // CB_WIN)
        return aligned, n_pass

    def window(al, c):
        lo = al + c * CB_WIN
        return lo, _mult(jnp.minimum(lo, cap - CB_WIN), CB_ALIGN)

    def start_fetch(bb, aligned, c, slot):
        for e in range(E):
            _, st = window(aligned[e], c)
            pltpu.make_async_copy(y_hbm.at[bb, e, pl.ds(st, CB_WIN)],
                                  buf_ref.at[slot, pl.ds(e * CB_WIN, CB_WIN)], sem.at[slot]).start()

    def wait_fetch(slot):
        pltpu.make_async_copy(buf_ref.at[slot], buf_ref.at[slot], sem.at[slot]).wait()

    def scatter_add(aligned, c, slot):
        pieces = []
        for e in range(E):
            lo, st = window(aligned[e], c)
            srow = slot_ref[0, e:e + 1, :]
            pieces.append(((srow == sub + _f32(st)) & (srow >= _f32(lo))).astype(BF16))
        sel_t = jnp.concatenate(pieces, axis=0)
        return lax.dot_general(sel_t, buf_ref[slot], (((0,), (0,)), ((), ())), preferred_element_type=F32)

    aligned, n_pass = tile_windows(b, j)

    @pl.when(step == 0)
    def _():
        start_fetch(b, aligned, 0, cur)

    @pl.when(step + 1 < n_batch * nch)
    def _():
        wrap = j + 1 == nch
        nb = jnp.where(wrap, b + 1, b)
        nj = jnp.where(wrap, 0, j + 1)
        start_fetch(nb, tile_windows(nb, nj)[0], 0, 1 - cur)

    wait_fetch(cur)
    acc_ref[...] = scatter_add(aligned, 0, cur)

    def extra_pass(c, carry):
        start_fetch(b, aligned, c, cur)
        wait_fetch(cur)
        acc_ref[...] += scatter_add(aligned, c, cur)
        return carry

    lax.fori_loop(1, n_pass, extra_pass, 0)
    o_ref[...] = x1_ref[...] + mod_ref[0, 5:6, :] * acc_ref[...]


def _combine_call(bounds_flat, x1, mod, slot_t, y, B, T, cap):
    D = x1.shape[1]
    nch = T // RT_CHUNK
    E = N_EXPERTS
    grid_spec = pltpu.PrefetchScalarGridSpec(
        num_scalar_prefetch=1,
        grid=(B, nch),
        in_specs=[pl.BlockSpec((RT_CHUNK, D), lambda b, j, bnd: (b * nch + j, 0)),
                  pl.BlockSpec((1, N_MOD, D), lambda b, j, bnd: (b, 0, 0)),
                  pl.BlockSpec((1, E, RT_CHUNK), lambda b, j, bnd: (b, 0, j)),
                  pl.BlockSpec(memory_space=pl.ANY)],
        out_specs=pl.BlockSpec((RT_CHUNK, D), lambda b, j, bnd: (b * nch + j, 0)),
        scratch_shapes=[pltpu.VMEM((2, E * CB_WIN, D), BF16), pltpu.VMEM((RT_CHUNK, D), F32),
                        pltpu.SemaphoreType.DMA((2,))],
    )
    return pl.pallas_call(
        functools.partial(_combine_kernel, cap=cap, nch=nch, n_batch=B),
        out_shape=jax.ShapeDtypeStruct(x1.shape, F32),
        grid_spec=grid_spec,
        compiler_params=_cparams(("arbitrary", "arbitrary")),
        name="combine",
    )(bounds_flat, x1, mod, slot_t, y)


def _rope_tables(n_tokens):
    t = np.arange(n_tokens)
    row = (t // GRID_W).astype(np.float64)
    col = (t % GRID_W).astype(np.float64)
    n_freq = MLA_ROPE // 4
    inv = ROPE_THETA ** (-np.arange(n_freq, dtype=np.float64) / n_freq)
    ang = np.concatenate([row[:, None] * inv, col[:, None] * inv], axis=-1)
    cos, sin = np.cos(ang), np.sin(ang)
    z = np.zeros((n_tokens, LANES - MLA_ROPE))
    return (np.concatenate([cos, cos, z], axis=-1).astype(np.float32),
            np.concatenate([-sin, sin, z], axis=-1).astype(np.float32))


def _pad_lanes(v):
    return jnp.concatenate([v, jnp.zeros((LANES - v.shape[0],), v.dtype)])[None, :]


def kernel(x, c, ctx, c_ctx, w_mod, b_mod, g_norm1, w_in, g_qa, w_qb, g_kva, w_kvb, g_q_mla, g_k_mla,
           g_q_na, g_k_na, rpb_na, g_out_mla, g_out_na, w_out, g_norm2, w_router, w_gate, w_up, w_down):
    B, T, D = x.shape
    C = ctx.shape[1]
    assert w_mod.shape[0] == 1, "single-layer problem"
    assert T % GRID_W == 0 and (T // GRID_W) % NA_QROWS == 0 and T // GRID_W >= 2 * NA_BAND
    assert w_in.shape[2] == 3 * NA_W + MLA_Q_RANK + MLA_KV_RANK + MLA_ROPE
    rows = T // GRID_W
    cap = EC_CAPACITY_FACTOR * T // N_EXPERTS
    half = MLA_ROPE // 2

    w_in_p = _win_call(w_in[0])
    wq3 = w_qb[0].reshape(MLA_Q_RANK, MLA_HEADS, MLA_QK_DIM)
    wq_p = jnp.concatenate([wq3, wq3[..., MLA_NOPE + half:], wq3[..., MLA_NOPE:MLA_NOPE + half]], axis=-1)
    wq_p = wq_p.reshape(MLA_Q_RANK, MLA_HEADS * MLA_QK_PAD).astype(BF16)
    wkv3 = w_kvb[0].reshape(MLA_KV_RANK, MLA_HEADS, MLA_NOPE + MLA_V)
    wk_p = wkv3[..., :MLA_NOPE].reshape(MLA_KV_RANK, MLA_HEADS * MLA_NOPE).astype(BF16)
    wv_p = wkv3[..., MLA_NOPE:].reshape(MLA_KV_RANK, MLA_HEADS * MLA_V).astype(BF16)

    def rope_gains(g):
        gr = g[MLA_NOPE:]
        return g[None, :MLA_NOPE], _pad_lanes(gr), _pad_lanes(jnp.concatenate([gr[half:], gr[:half]]))

    gq0, gqa, gqb = rope_gains(g_q_mla[0])
    gk0, gka, gkb = rope_gains(g_k_mla[0])
    ct, st = _rope_tables(T)
    consts = dict(ct=ct, st=st, g_qa=g_qa, wq=wq_p, g_kva=g_kva, wk=wk_p, wv=wv_p,
                  gq0=gq0, gqa=gqa, gqb=gqb, gk0=gk0, gka=gka, gkb=gkb, g_q_na=g_q_na, g_k_na=g_k_na)
    wr = jnp.concatenate([w_router[0], jnp.zeros((D, LANES - N_EXPERTS), F32)], axis=1)
    wr_hi = wr.astype(BF16)
    wr_split = jnp.concatenate([wr_hi, (wr - wr_hi.astype(F32)).astype(BF16)], axis=1)
    bias = _na_bias_table(rpb_na[0], rows)

    cvec = jnp.concatenate([c, c_ctx[None, :], jnp.zeros((8 - B - 1, D), F32)], axis=0)
    mod = _mod_call(cvec.T, w_mod[0], b_mod, B + 1)
    mod_lat = mod[:B].reshape(B, N_MOD, D)
    mod_ctx = mod[B:].reshape(1, N_MOD, D)

    x2 = x.reshape(B * T, D)
    p_lat = _inproj_call(x2, mod_lat, g_norm1, w_in_p, T, min(INPROJ_TM, T))
    p_ctx = _inproj_call(ctx.reshape(B * C, D), mod_ctx, g_norm1, w_in_p, B * C, C)
    qm, km, vm, qn, kn = _prep_call(p_lat, B, T, min(PREP_TM, T), consts, False)
    kmc, vmc, knc = _prep_call(p_ctx, B, C, C, consts, True)
    o_m = _mla_call(qm, km, vm, kmc, vmc, min(MLA_TQ, T), min(MLA_TK, T))
    o_n = _na_call(qn, kn, p_lat, knc, p_ctx, bias)
    x1, h2, aff = _merge_call(o_m.reshape(B * T, -1), o_n.reshape(B * T, -1), x2, mod_lat,
                              g_out_mla, g_out_na, w_out[0].astype(BF16), g_norm2, wr_split,
                              T, min(MERGE_TM, T))

    slot_t, idx, gate, bounds = _route_call(aff.reshape(B, T, LANES), cap)
    y = _ffn_call(idx.reshape(-1), h2, gate, w_gate[0], w_up[0], w_down[0], B, T, cap)
    out = _combine_call(bounds, x1, mod_lat, slot_t, y, B, T, cap)
    return out.reshape(B, T, D)
```
